```python
import math
import jax, jax.numpy as jnp
from jax import lax
import numpy as np


D_MODEL = 2048
BATCH = 1
SEQ = 16384
DEPTH = 2

HEAD_DIM = 128
GRID_W = 64
Q_BLOCK = 128
ROPE_THETA = 10000.0
EPS = 1e-6
A_HEADS = 4
A_KV_HEADS = 2
B_HEADS = 4
B_NOPE = 128
B_ROPE = 64
B_V = HEAD_DIM
B_Q_RANK = 384
B_KV_RANK = 256
C_HEADS = 4
C_KV_HEADS = 2
WINDOW = 128
D_HEADS = 4
D_HALF = HEAD_DIM // 2
N_BUCKETS = 32
MAX_DISTANCE = 128
N_BIAS_HEADS = C_HEADS + D_HEADS
D_FF = 4 * D_MODEL
N_MOD = 6
SPLIT_SIZES = (A_HEADS * HEAD_DIM, A_KV_HEADS * HEAD_DIM, A_KV_HEADS * HEAD_DIM,
               B_Q_RANK, B_KV_RANK, B_ROPE,
               C_HEADS * HEAD_DIM, C_KV_HEADS * HEAD_DIM, C_KV_HEADS * HEAD_DIM,
               D_HEADS * HEAD_DIM, D_HEADS * HEAD_DIM, D_HEADS * HEAD_DIM)
IN_COLS = sum(SPLIT_SIZES)
SPLIT_POINTS = tuple(int(v) for v in np.cumsum(SPLIT_SIZES)[:-1])
MIX_WIDTH = (A_HEADS + B_HEADS + C_HEADS + D_HEADS) * HEAD_DIM

kernel_name = 'hybrid_parallel_head_group_encoder'


def _rms_norm(x, gain):
    xf = x.astype(jnp.float32)
    y = xf * lax.rsqrt(jnp.mean(xf * xf, axis=-1, keepdims=True) + EPS)
    return (y * gain.astype(jnp.float32)).astype(x.dtype)


def _rope_tables(pos, dim):
    inv = ROPE_THETA ** (-jnp.arange(0, dim, 2, dtype=jnp.float32) / dim)
    ang = pos.astype(jnp.float32)[:, None] * inv[None, :]
    ang = jnp.concatenate([ang, ang], axis=-1)
    return jnp.cos(ang), jnp.sin(ang)


def _apply_rope(x, cos, sin):
    xf = x.astype(jnp.float32)
    x1, x2 = jnp.split(xf, 2, axis=-1)
    rot = jnp.concatenate([-x2, x1], axis=-1)
    return (xf * cos + rot * sin).astype(x.dtype)


def _t5_bucket(rel):
    half = N_BUCKETS // 2
    max_exact = half // 2
    n = jnp.abs(rel)
    large = max_exact + (jnp.log(jnp.maximum(n, 1).astype(jnp.float32) / max_exact)
                         / math.log(MAX_DISTANCE / max_exact) * (half - max_exact)).astype(jnp.int32)
    large = jnp.minimum(large, half - 1)
    return jnp.where(rel > 0, half, 0) + jnp.where(n < max_exact, n, large)


def _split_heads(t, n):
    b, s, _ = t.shape
    return t.reshape(b, s, n, -1).transpose(0, 2, 1, 3)


def _dense_attention(q, k, v, scale):
    b, hq, s, dk = q.shape
    hkv, dv = k.shape[1], v.shape[-1]
    g = hq // hkv
    nb = s // Q_BLOCK
    qb = jnp.moveaxis(q.reshape(b, hkv, g, nb, Q_BLOCK, dk), 3, 0)

    def block(qi):
        logits = jnp.einsum('bhgqd,bhkd->bhgqk', qi, k).astype(jnp.float32) * scale
        p = jax.nn.softmax(logits, axis=-1).astype(v.dtype)
        return jnp.einsum('bhgqk,bhkd->bhgqd', p, v)

    o = lax.map(block, qb)
    return jnp.moveaxis(o, 0, 3).reshape(b, hq, s, dv)


def _window_attention(q, k, v, bias_table, sink):
    b, hq, s, d = q.shape
    hkv = k.shape[1]
    g = hq // hkv
    qbk = Q_BLOCK
    nb = s // qbk
    qb = q.reshape(b, hkv, g, nb, qbk, d)

    def band(t):
        tp = jnp.pad(t, ((0, 0), (0, 0), (qbk, qbk), (0, 0))).reshape(b, hkv, nb + 2, qbk, t.shape[-1])
        return jnp.concatenate([tp[:, :, :-2], tp[:, :, 1:-1], tp[:, :, 2:]], axis=3)

    kb, vb = band(k), band(v)
    logits = jnp.einsum('bhgnqd,bhnkd->bhgnqk', qb, kb).astype(jnp.float32) * (d ** -0.5)
    qi = jnp.arange(qbk)
    kj = jnp.arange(3 * qbk)
    rel = kj[None, :] - qbk - qi[:, None]
    bias = jnp.moveaxis(bias_table[_t5_bucket(rel)], -1, 0).astype(jnp.float32).reshape(hkv, g, 1, qbk, 3 * qbk)
    kpos = jnp.arange(nb)[:, None] * qbk - qbk + kj[None, :]
    valid = (jnp.abs(rel) <= WINDOW)[None] & ((kpos >= 0) & (kpos < s))[:, None, :]
    logits = jnp.where(valid, logits + bias, -jnp.inf)
    sink_col = jnp.broadcast_to(sink.astype(jnp.float32).reshape(1, hkv, g, 1, 1, 1), logits.shape[:-1] + (1,))
    p = jax.nn.softmax(jnp.concatenate([logits, sink_col], axis=-1), axis=-1)[..., :-1]
    o = jnp.einsum('bhgnqk,bhnkd->bhgnqd', p.astype(v.dtype), vb)
    return o.reshape(b, hq, s, d)


def _diff_attention(q1, q2, k1, k2, v, bias_table, lam):
    b, h, s, dk = q1.shape
    nb = s // Q_BLOCK
    scale = dk ** -0.5
    qb1 = jnp.moveaxis(q1.reshape(b, h, nb, Q_BLOCK, dk), 2, 0)
    qb2 = jnp.moveaxis(q2.reshape(b, h, nb, Q_BLOCK, dk), 2, 0)
    kpos = jnp.arange(s)

    def block(args):
        qi1, qi2, idx = args
        qpos = idx * Q_BLOCK + jnp.arange(Q_BLOCK)
        bias = jnp.moveaxis(bias_table[_t5_bucket(kpos[None, :] - qpos[:, None])], -1, 0).astype(jnp.float32)
        a1 = jax.nn.softmax(jnp.einsum('bhqd,bhkd->bhqk', qi1, k1).astype(jnp.float32) * scale + bias, axis=-1)
        a2 = jax.nn.softmax(jnp.einsum('bhqd,bhkd->bhqk', qi2, k2).astype(jnp.float32) * scale + bias, axis=-1)
        return jnp.einsum('bhqk,bhkd->bhqd', (a1 - lam * a2).astype(v.dtype), v)

    o = lax.map(block, (qb1, qb2, jnp.arange(nb)))
    return jnp.moveaxis(o, 0, 2).reshape(b, h, s, v.shape[-1])


def _mixer(h, w_in, a_q_norm, a_k_norm, b_q_norm, b_kv_norm, b_w_uq, b_w_ukv, c_sink,
           d_lambda, d_sub_norm, w_out, rel_bias, rope_row, rope_col, rope_seq, lam_init):
    b, s, _ = h.shape
    (a_q, a_k, a_v, b_cq, b_ckv, b_kr, c_q, c_k, c_v, d_q, d_k, d_v) = jnp.split(h @ w_in, SPLIT_POINTS, axis=-1)
    half = HEAD_DIM // 2

    def axial(t):
        return jnp.concatenate([_apply_rope(t[..., :half], *rope_row), _apply_rope(t[..., half:], *rope_col)], axis=-1)
    qa = axial(_rms_norm(_split_heads(a_q, A_HEADS), a_q_norm))
    ka = axial(_rms_norm(_split_heads(a_k, A_KV_HEADS), a_k_norm))
    out_a = _dense_attention(qa, ka, _split_heads(a_v, A_KV_HEADS), HEAD_DIM ** -0.5)

    qb = _split_heads(_rms_norm(b_cq, b_q_norm) @ b_w_uq, B_HEADS)
    qb = jnp.concatenate([qb[..., :B_NOPE], _apply_rope(qb[..., B_NOPE:], *rope_seq)], axis=-1)
    kvb = _split_heads(_rms_norm(b_ckv, b_kv_norm) @ b_w_ukv, B_HEADS)
    kr = _apply_rope(b_kr[:, None], *rope_seq)
    kb = jnp.concatenate([kvb[..., :B_NOPE], jnp.broadcast_to(kr, (b, B_HEADS, s, B_ROPE))], axis=-1)
    out_b = _dense_attention(qb, kb, kvb[..., B_NOPE:], (B_NOPE + B_ROPE) ** -0.5)

    out_c = _window_attention(_split_heads(c_q, C_HEADS), _split_heads(c_k, C_KV_HEADS),
                              _split_heads(c_v, C_KV_HEADS), rel_bias[:, :C_HEADS], c_sink)

    qd = _split_heads(d_q, D_HEADS)
    kd = _split_heads(d_k, D_HEADS)
    lf = d_lambda.astype(jnp.float32)
    lam = jnp.exp(jnp.sum(lf[0] * lf[1])) - jnp.exp(jnp.sum(lf[2] * lf[3])) + lam_init
    od = _diff_attention(qd[..., :D_HALF], qd[..., D_HALF:], kd[..., :D_HALF], kd[..., D_HALF:],
                         _split_heads(d_v, D_HEADS), rel_bias[:, C_HEADS:], lam)
    out_d = _rms_norm(od, d_sub_norm) * (1.0 - lam_init)

    o = jnp.concatenate([out_a, out_b, out_c, out_d], axis=1)
    return o.transpose(0, 2, 1, 3).reshape(b, s, MIX_WIDTH) @ w_out


def _sq_relu_mlp(h, w1, w2):
    return jnp.square(jax.nn.relu(h @ w1)) @ w2


def setup_inputs(seed: int = 0) -> dict:
    key = jax.random.key(seed)
    ks = jax.random.split(key, 21)

    def nrm(k, shape, scale):
        return jax.random.normal(k, shape, jnp.float32) * scale

    def gain(k, shape):
        return 1.0 + 0.02 * jax.random.normal(k, shape, jnp.float32)

    return {
        'x': nrm(ks[0], (BATCH, SEQ, D_MODEL), 1.0),
        'c': nrm(ks[1], (BATCH, D_MODEL), 1.0),
        'w_ada': nrm(ks[2], (DEPTH, D_MODEL, N_MOD * D_MODEL), 0.5 * D_MODEL ** -0.5),
        'b_ada': nrm(ks[3], (DEPTH, N_MOD * D_MODEL), 0.02),
        'norm1': gain(ks[4], (DEPTH, D_MODEL)),
        'w_in': nrm(ks[5], (DEPTH, D_MODEL, IN_COLS), D_MODEL ** -0.5),
        'a_q_norm': gain(ks[6], (DEPTH, HEAD_DIM)),
        'a_k_norm': gain(ks[7], (DEPTH, HEAD_DIM)),
        'b_q_norm': gain(ks[8], (DEPTH, B_Q_RANK)),
        'b_kv_norm': gain(ks[9], (DEPTH, B_KV_RANK)),
        'b_w_uq': nrm(ks[10], (DEPTH, B_Q_RANK, B_HEADS * (B_NOPE + B_ROPE)), B_Q_RANK ** -0.5),
        'b_w_ukv': nrm(ks[11], (DEPTH, B_KV_RANK, B_HEADS * (B_NOPE + B_V)), B_KV_RANK ** -0.5),
        'c_sink': nrm(ks[12], (DEPTH, C_HEADS), 1.0),
        'd_lambda': nrm(ks[13], (DEPTH, 4, D_HALF), 0.1),
        'd_sub_norm': gain(ks[14], (DEPTH, HEAD_DIM)),
        'w_out': nrm(ks[15], (DEPTH, MIX_WIDTH, D_MODEL), MIX_WIDTH ** -0.5),
        'norm2': gain(ks[16], (DEPTH, D_MODEL)),
        'w_ff1': nrm(ks[17], (DEPTH, D_MODEL, D_FF), D_MODEL ** -0.5),
        'w_ff2': nrm(ks[18], (DEPTH, D_FF, D_MODEL), D_FF ** -0.5),
        'rel_bias': nrm(ks[19], (N_BUCKETS, N_BIAS_HEADS), 0.5),
        'final_norm': gain(ks[20], (D_MODEL,)),
    }


def reference(x, c, w_ada, b_ada, norm1, w_in, a_q_norm, a_k_norm, b_q_norm, b_kv_norm,
              b_w_uq, b_w_ukv, c_sink, d_lambda, d_sub_norm, w_out, norm2, w_ff1, w_ff2,
              rel_bias, final_norm):
    s = x.shape[1]
    rows = s // GRID_W
    row_idx = jnp.repeat(jnp.arange(rows), GRID_W)
    col_idx = jnp.tile(jnp.arange(GRID_W), rows)
    half = HEAD_DIM // 2
    rope_row = _rope_tables(row_idx, half)
    rope_col = _rope_tables(col_idx, half)
    rope_seq = _rope_tables(jnp.arange(s), B_ROPE)
    cs = jax.nn.silu(c)
    for l in range(DEPTH):
        lam_init = 0.8 - 0.6 * math.exp(-0.3 * l)
        mod = cs @ w_ada[l] + b_ada[l]
        shift1, scale1, gate1, shift2, scale2, gate2 = jnp.split(mod[:, None, :], N_MOD, axis=-1)
        h = _rms_norm(x, norm1[l]) * (1 + scale1) + shift1
        x = x + gate1 * _mixer(h, w_in[l], a_q_norm[l], a_k_norm[l], b_q_norm[l], b_kv_norm[l],
                               b_w_uq[l], b_w_ukv[l], c_sink[l], d_lambda[l], d_sub_norm[l],
                               w_out[l], rel_bias, rope_row, rope_col, rope_seq, lam_init)
        h = _rms_norm(x, norm2[l]) * (1 + scale2) + shift2
        x = x + gate2 * _sq_relu_mlp(h, w_ff1[l], w_ff2[l])
    return _rms_norm(x, final_norm)
```

```python
import functools
import math

import jax
import jax.numpy as jnp
from jax import lax
from jax.experimental import pallas as pl
from jax.experimental.pallas import tpu as pltpu

F32 = jnp.float32
BF16 = jnp.bfloat16

HEAD_DIM = 128
GRID_W = 64
ROPE_THETA = 10000.0
EPS = 1e-6
A_HEADS, A_KV_HEADS = 4, 2
B_HEADS, B_NOPE, B_ROPE, B_Q_RANK, B_KV_RANK = 4, 128, 64, 384, 256
C_HEADS, C_KV_HEADS, WINDOW = 4, 2, 128
D_HEADS = 4
D_HALF = HEAD_DIM // 2
N_BUCKETS, MAX_DISTANCE = 32, 128
N_MOD = 6

LOG2E = math.log2(math.e)
NEG_BIG = -1e30

LANES = 128
VMEM_LIMIT = 56 * 1024 * 1024

B_KR_PAD = LANES - B_ROPE
COL_A_Q = 0
COL_A_K = COL_A_Q + A_HEADS * HEAD_DIM
COL_A_V = COL_A_K + A_KV_HEADS * HEAD_DIM
COL_B_CQ = COL_A_V + A_KV_HEADS * HEAD_DIM
COL_B_CKV = COL_B_CQ + B_Q_RANK
COL_B_KR = COL_B_CKV + B_KV_RANK
COL_C_Q = COL_B_KR + B_ROPE + B_KR_PAD
COL_C_K = COL_C_Q + C_HEADS * HEAD_DIM
COL_C_V = COL_C_K + C_KV_HEADS * HEAD_DIM
COL_D_Q = COL_C_V + C_KV_HEADS * HEAD_DIM
COL_D_K = COL_D_Q + D_HEADS * HEAD_DIM
COL_D_V = COL_D_K + D_HEADS * HEAD_DIM
IN_COLS_PAD = COL_D_V + D_HEADS * HEAD_DIM
B_QK_PAD = 2 * LANES


def _params(sem):
    return pltpu.CompilerParams(dimension_semantics=sem, vmem_limit_bytes=VMEM_LIMIT)


def _ada_kernel(c_ref, w_ref, b_ref, o_ref):
    c = c_ref[...]
    cs = c * (1.0 / (1.0 + jnp.exp(-c)))
    cs8 = jnp.broadcast_to(cs, (8, cs.shape[1])).astype(BF16)
    r = jnp.dot(cs8, w_ref[0].astype(BF16), preferred_element_type=F32)
    o_ref[0] = r[0:1] + b_ref[0]


def _ada_mod(c, w_ada, b_ada, tn=1024):
    depth, d, n = w_ada.shape
    out = pl.pallas_call(
        _ada_kernel,
        grid=(depth, n // tn),
        in_specs=[pl.BlockSpec((1, d), lambda l, j: (0, 0)),
                  pl.BlockSpec((1, d, tn), lambda l, j: (l, 0, j)),
                  pl.BlockSpec((1, 1, tn), lambda l, j: (l, 0, j))],
        out_specs=pl.BlockSpec((1, 1, tn), lambda l, j: (l, 0, j)),
        out_shape=jax.ShapeDtypeStruct((depth, 1, n), F32),
        compiler_params=_params(("arbitrary", "arbitrary")),
        name="ada_mod",
    )(c, w_ada, b_ada.reshape(depth, 1, n))
    return out.reshape(depth, N_MOD, d)


def _norm_mm_kernel(x_ref, g_ref, mod_ref, w_ref, o_ref, h_ref, *, shift_row, scale_row, relu2):
    @pl.when(pl.program_id(1) == 0)
    def _():
        x = x_ref[...]
        y = x * lax.rsqrt(jnp.mean(x * x, axis=-1, keepdims=True) + EPS) * g_ref[...]
        h = y * (1.0 + mod_ref[scale_row:scale_row + 1, :]) + mod_ref[shift_row:shift_row + 1, :]
        h_ref[...] = h.astype(BF16)

    acc = jnp.dot(h_ref[...], w_ref[...], preferred_element_type=F32)
    if relu2:
        acc = jnp.square(jnp.maximum(acc, 0.0))
    o_ref[...] = acc.astype(o_ref.dtype)


def _norm_matmul(x, gain, mod, w, *, shift_row, scale_row, relu2, out_dtype, tm, tn, name):
    s, d = x.shape
    n = w.shape[1]
    return pl.pallas_call(
        functools.partial(_norm_mm_kernel, shift_row=shift_row, scale_row=scale_row, relu2=relu2),
        grid=(s // tm, n // tn),
        in_specs=[pl.BlockSpec((tm, d), lambda i, j: (i, 0)),
                  pl.BlockSpec((1, d), lambda i, j: (0, 0)),
                  pl.BlockSpec((N_MOD, d), lambda i, j: (0, 0)),
                  pl.BlockSpec((d, tn), lambda i, j: (0, j))],
        out_specs=pl.BlockSpec((tm, tn), lambda i, j: (i, j)),
        out_shape=jax.ShapeDtypeStruct((s, n), out_dtype),
        scratch_shapes=[pltpu.VMEM((tm, d), BF16)],
        compiler_params=_params(("arbitrary", "arbitrary")),
        name=name,
    )(x, gain.reshape(1, d), mod, w)


def _mm_res_kernel(*refs, n_in, gate_row):
    a_refs, w_refs = refs[:n_in], refs[n_in:2 * n_in]
    x_ref, mod_ref, o_ref = refs[2 * n_in:]
    acc = jnp.dot(a_refs[0][...], w_refs[0][...], preferred_element_type=F32)
    for a_ref, w_ref in zip(a_refs[1:], w_refs[1:]):
        acc = acc + jnp.dot(a_ref[...], w_ref[...], preferred_element_type=F32)
    o_ref[...] = x_ref[...] + mod_ref[gate_row:gate_row + 1, :] * acc


def _matmul_res(a_list, w_list, x, mod, *, gate_row, tm, tn, name):
    s, d = x.shape
    n_in = len(a_list)
    in_specs = ([pl.BlockSpec((tm, a.shape[1]), lambda i, j: (i, 0)) for a in a_list]
                + [pl.BlockSpec((w.shape[0], tn), lambda i, j: (0, j)) for w in w_list]
                + [pl.BlockSpec((tm, tn), lambda i, j: (i, j)),
                   pl.BlockSpec((N_MOD, tn), lambda i, j: (0, j))])
    return pl.pallas_call(
        functools.partial(_mm_res_kernel, n_in=n_in, gate_row=gate_row),
        grid=(s // tm, d // tn),
        in_specs=in_specs,
        out_specs=pl.BlockSpec((tm, tn), lambda i, j: (i, j)),
        out_shape=jax.ShapeDtypeStruct((s, d), F32),
        compiler_params=_params(("arbitrary", "arbitrary")),
        name=name,
    )(*a_list, *w_list, x, mod)


def _rope(x, cos, sin_lo, sin_hi):
    return (x * cos + pltpu.roll(x, LANES - 32, 1) * sin_lo + pltpu.roll(x, 32, 1) * sin_hi)


def _rms(x, gain):
    return x * lax.rsqrt(jnp.mean(x * x, axis=-1, keepdims=True) + EPS) * gain


def _prep_kernel(qkv_ref, ax_ref, sq_ref, aqn_ref, akn_ref, bqn_ref, bkvn_ref, wuq_ref, wukv_ref,
                 qa_ref, ka_ref, va_ref, qb_ref, kb_ref, vb_ref, qc_ref, kc_ref, vc_ref,
                 qd_ref, kd_ref, vd_ref):
    hd = HEAD_DIM
    ax_cos, ax_lo, ax_hi = ax_ref[0], ax_ref[1], ax_ref[2]
    sq_cos, sq_lo, sq_hi = sq_ref[0], sq_ref[1], sq_ref[2]

    def col(start, width=hd):
        return qkv_ref[:, start:start + width]

    sc_a = HEAD_DIM ** -0.5 * LOG2E
    for h in range(A_HEADS):
        q = _rope(_rms(col(COL_A_Q + h * hd), aqn_ref[...]), ax_cos, ax_lo, ax_hi)
        qa_ref[:, h * hd:(h + 1) * hd] = (q * sc_a).astype(BF16)
    for h in range(A_KV_HEADS):
        k = _rope(_rms(col(COL_A_K + h * hd), akn_ref[...]), ax_cos, ax_lo, ax_hi)
        ka_ref[:, h * hd:(h + 1) * hd] = k.astype(BF16)
    va_ref[...] = col(COL_A_V, A_KV_HEADS * hd).astype(BF16)

    sc_b = (B_NOPE + B_ROPE) ** -0.5 * LOG2E
    cq = _rms(col(COL_B_CQ, B_Q_RANK), bqn_ref[...]).astype(BF16)
    qb = jnp.dot(cq, wuq_ref[...], preferred_element_type=F32)
    ckv = _rms(col(COL_B_CKV, B_KV_RANK), bkvn_ref[...]).astype(BF16)
    kvb = jnp.dot(ckv, wukv_ref[...], preferred_element_type=F32)
    kr = _rope(col(COL_B_KR, LANES), sq_cos, sq_lo, sq_hi).astype(BF16)
    for h in range(B_HEADS):
        base = h * B_QK_PAD
        qb_ref[:, base:base + hd] = (qb[:, base:base + hd] * sc_b).astype(BF16)
        q_r = _rope(qb[:, base + hd:base + 2 * hd], sq_cos, sq_lo, sq_hi)
        qb_ref[:, base + hd:base + 2 * hd] = (q_r * sc_b).astype(BF16)
        kb_ref[:, base:base + hd] = kvb[:, base:base + hd].astype(BF16)
        kb_ref[:, base + hd:base + 2 * hd] = kr
        vb_ref[:, h * hd:(h + 1) * hd] = kvb[:, base + hd:base + 2 * hd].astype(BF16)

    sc_c = HEAD_DIM ** -0.5 * LOG2E
    qc_ref[...] = (col(COL_C_Q, C_HEADS * hd) * sc_c).astype(BF16)
    kc_ref[...] = col(COL_C_K, C_KV_HEADS * hd).astype(BF16)
    vc_ref[...] = col(COL_C_V, C_KV_HEADS * hd).astype(BF16)

    sc_d = D_HALF ** -0.5 * LOG2E
    lane = lax.broadcasted_iota(jnp.int32, (qkv_ref.shape[0], hd), 1)
    lo = lane < D_HALF
    for h in range(D_HEADS):
        q = col(COL_D_Q + h * hd) * sc_d
        qd_ref[:, 2 * h * hd:(2 * h + 1) * hd] = jnp.where(lo, q, 0.0).astype(BF16)
        qd_ref[:, (2 * h + 1) * hd:(2 * h + 2) * hd] = jnp.where(lo, 0.0, q).astype(BF16)
    kd_ref[...] = col(COL_D_K, D_HEADS * hd).astype(BF16)
    vd_ref[...] = col(COL_D_V, D_HEADS * hd).astype(BF16)


def _prep(qkv, ax_tab, sq_tab, aqn, akn, bqn, bkvn, wuq, wukv, tm=512):
    s = qkv.shape[0]
    hd = HEAD_DIM
    widths = [A_HEADS * hd, A_KV_HEADS * hd, A_KV_HEADS * hd,
              B_HEADS * B_QK_PAD, B_HEADS * B_QK_PAD, B_HEADS * hd,
              C_HEADS * hd, C_KV_HEADS * hd, C_KV_HEADS * hd,
              2 * D_HEADS * hd, D_HEADS * hd, D_HEADS * hd]
    row = lambda i: (i, 0)
    full = lambda i: (0, 0)
    return pl.pallas_call(
        _prep_kernel,
        grid=(s // tm,),
        in_specs=[pl.BlockSpec((tm, IN_COLS_PAD), row),
                  pl.BlockSpec((3, tm, LANES), lambda i: (0, i, 0)),
                  pl.BlockSpec((3, tm, LANES), lambda i: (0, i, 0)),
                  pl.BlockSpec((1, hd), full), pl.BlockSpec((1, hd), full),
                  pl.BlockSpec((1, B_Q_RANK), full), pl.BlockSpec((1, B_KV_RANK), full),
                  pl.BlockSpec(wuq.shape, full), pl.BlockSpec(wukv.shape, full)],
        out_specs=[pl.BlockSpec((tm, w), row) for w in widths],
        out_shape=[jax.ShapeDtypeStruct((s, w), BF16) for w in widths],
        compiler_params=_params(("arbitrary",)),
        name="prep",
    )(qkv, ax_tab, sq_tab, aqn.reshape(1, hd), akn.reshape(1, hd),
      bqn.reshape(1, B_Q_RANK), bkvn.reshape(1, B_KV_RANK), wuq, wukv)


def _flash_kernel(*refs, groups_stacked, dk, tq, tk, seq, diff):
    if diff:
        q_ref, k_ref, v_ref, bias_ref, far_ref, scal_ref, gain_ref, o_ref = refs
    else:
        q_ref, k_ref, v_ref, o_ref = refs
    r_stack = groups_stacked
    rows = r_stack * tq
    g = pl.program_id(0)
    i = pl.program_id(1)
    n_chunks = seq // tk
    q = jnp.concatenate([q_ref[:, r * dk:(r + 1) * dk] for r in range(r_stack)], axis=0)

    def step(c, carry, bias_tile, offset):
        m, l, acc = carry
        start = pl.multiple_of(c * tk, tk)
        kc = k_ref[pl.ds(start, tk), :]
        vc = v_ref[pl.ds(start, tk), :]
        s = lax.dot_general(q, kc, (((1,), (1,)), ((), ())), preferred_element_type=F32)
        if bias_tile is not None:
            s = (s.reshape(r_stack, tq, tk) + bias_tile[None]).reshape(rows, tk)
        m_cur = jnp.max(s, axis=-1, keepdims=True)
        if offset is not None:
            m_cur = m_cur + offset
        m_new = jnp.maximum(m, m_cur)
        alpha = jnp.exp2(m - m_new)
        sub = m_new if offset is None else m_new - offset
        p = jnp.exp2(s - sub)
        l = alpha * l + jnp.sum(p, axis=-1, keepdims=True)
        acc = alpha * acc + jnp.dot(p.astype(BF16), vc, preferred_element_type=F32)
        return m_new, l, acc

    init = (jnp.full((rows, 1), NEG_BIG, F32), jnp.zeros((rows, 1), F32),
            jnp.zeros((rows, HEAD_DIM), F32))

    if not diff:
        m, l, acc = lax.fori_loop(0, n_chunks, lambda c, cr: step(c, cr, None, None), init)
        o = acc / l
        for r in range(r_stack):
            o_ref[:, r * HEAD_DIM:(r + 1) * HEAD_DIM] = o[r * tq:(r + 1) * tq].astype(o_ref.dtype)
        return

    q0 = i * tq
    n_left = jnp.maximum(q0 - MAX_DISTANCE, 0) // tk
    first_right = jnp.minimum((q0 + tq + MAX_DISTANCE + tk - 1) // tk, n_chunks)
    far_neg = far_ref[g, 0]
    far_pos = far_ref[g, 1]

    def near_step(c, cr):
        o_idx = (c * tk - q0 + tk) // tq
        return step(c, cr, bias_ref[0, o_idx], None)

    carry = lax.fori_loop(0, n_left, lambda c, cr: step(c, cr, None, far_neg), init)
    carry = lax.fori_loop(n_left, first_right, near_step, carry)
    m, l, acc = lax.fori_loop(first_right, n_chunks, lambda c, cr: step(c, cr, None, far_pos), carry)
    o = acc / l
    lam = scal_ref[0]
    od = o[:tq] - lam * o[tq:]
    y = od * lax.rsqrt(jnp.mean(od * od, axis=-1, keepdims=True) + EPS) * gain_ref[...]
    o_ref[...] = (y * scal_ref[1]).astype(o_ref.dtype)


def _flash(q, k, v, *, groups, groups_stacked, dk, tq, tk, name, diff_args=None):
    s = q.shape[0]
    diff = diff_args is not None
    kern = functools.partial(_flash_kernel, groups_stacked=groups_stacked, dk=dk, tq=tq, tk=tk,
                             seq=s, diff=diff)
    in_specs = [pl.BlockSpec((tq, groups_stacked * dk), lambda g, i: (i, g)),
                pl.BlockSpec((s, dk), lambda g, i: (0, g)),
                pl.BlockSpec((s, HEAD_DIM), lambda g, i: (0, g))]
    args = [q, k, v]
    if diff:
        bias, far, scal, gain = diff_args
        in_specs += [pl.BlockSpec((1,) + bias.shape[1:], lambda g, i: (g, 0, 0, 0)),
                     pl.BlockSpec(memory_space=pltpu.SMEM),
                     pl.BlockSpec(memory_space=pltpu.SMEM),
                     pl.BlockSpec((1, HEAD_DIM), lambda g, i: (0, 0))]
        args += [bias, far, scal, gain.reshape(1, HEAD_DIM)]
        out_w = HEAD_DIM
    else:
        out_w = groups_stacked * HEAD_DIM
    return pl.pallas_call(
        kern,
        grid=(groups, s // tq),
        in_specs=in_specs,
        out_specs=pl.BlockSpec((tq, out_w), lambda g, i: (i, g)),
        out_shape=jax.ShapeDtypeStruct((s, groups * out_w), BF16),
        compiler_params=_params(("arbitrary", "arbitrary")),
        name=name,
    )(*args)


def _window_kernel(q_ref, kp_ref, kc_ref, kn_ref, vp_ref, vc_ref, vn_ref, bias_ref, sink_ref, o_ref,
                   kbuf, vbuf, *, tw, seq):
    w = WINDOW
    h = pl.program_id(0)
    i = pl.program_id(1)
    kbuf[0:w] = kp_ref[tw - w:tw]
    kbuf[w:w + tw] = kc_ref[...]
    kbuf[w + tw:w + tw + w] = kn_ref[0:w]
    vbuf[0:w] = vp_ref[tw - w:tw]
    vbuf[w:w + tw] = vc_ref[...]
    vbuf[w + tw:w + tw + w] = vn_ref[0:w]
    sink = sink_ref[h] * LOG2E
    bias = bias_ref[0]
    col = lax.broadcasted_iota(jnp.int32, (w, 3 * w), 1)
    for j in range(tw // w):
        qj = q_ref[j * w:(j + 1) * w, :]
        kw = kbuf[j * w:(j + 3) * w]
        vw = vbuf[j * w:(j + 3) * w]
        s = lax.dot_general(qj, kw, (((1,), (1,)), ((), ())), preferred_element_type=F32) + bias
        kpos = i * tw + (j - 1) * w + col
        s = jnp.where((kpos >= 0) & (kpos < seq), s, NEG_BIG)
        m = jnp.maximum(jnp.max(s, axis=-1, keepdims=True), sink)
        p = jnp.exp2(s - m)
        denom = jnp.sum(p, axis=-1, keepdims=True) + jnp.exp2(sink - m)
        o = jnp.dot(p.astype(BF16), vw, preferred_element_type=F32) / denom
        o_ref[j * w:(j + 1) * w, :] = o.astype(o_ref.dtype)


def _window(q, k, v, bias, sink, tw=1024):
    s = q.shape[0]
    hd = HEAD_DIM
    nb = s // tw
    grp = C_HEADS // C_KV_HEADS
    prev = lambda h, i: (jnp.maximum(i - 1, 0), h // grp)
    cur = lambda h, i: (i, h // grp)
    nxt = lambda h, i: (jnp.minimum(i + 1, nb - 1), h // grp)
    kv_specs = [pl.BlockSpec((tw, hd), f) for f in (prev, cur, nxt)]
    return pl.pallas_call(
        functools.partial(_window_kernel, tw=tw, seq=s),
        grid=(C_HEADS, nb),
        in_specs=[pl.BlockSpec((tw, hd), lambda h, i: (i, h))] + kv_specs + kv_specs
                 + [pl.BlockSpec((1, WINDOW, 3 * WINDOW), lambda h, i: (h, 0, 0)),
                    pl.BlockSpec(memory_space=pltpu.SMEM)],
        out_specs=pl.BlockSpec((tw, hd), lambda h, i: (i, h)),
        out_shape=jax.ShapeDtypeStruct((s, C_HEADS * hd), BF16),
        scratch_shapes=[pltpu.VMEM((tw + 2 * WINDOW, hd), BF16),
                        pltpu.VMEM((tw + 2 * WINDOW, hd), BF16)],
        compiler_params=_params(("arbitrary", "arbitrary")),
        name="window_attn",
    )(q, k, k, k, v, v, v, bias, sink)


def _final_norm_kernel(x_ref, g_ref, o_ref):
    o_ref[...] = _rms(x_ref[...], g_ref[...])


def _final_norm(x, gain, tm=512):
    s, d = x.shape
    return pl.pallas_call(
        _final_norm_kernel,
        grid=(s // tm,),
        in_specs=[pl.BlockSpec((tm, d), lambda i: (i, 0)), pl.BlockSpec((1, d), lambda i: (0, 0))],
        out_specs=pl.BlockSpec((tm, d), lambda i: (i, 0)),
        out_shape=jax.ShapeDtypeStruct((s, d), F32),
        compiler_params=_params(("arbitrary",)),
        name="final_norm",
    )(x, gain.reshape(1, d))


def _rope_tables(pos, dim):
    inv = ROPE_THETA ** (-jnp.arange(0, dim, 2, dtype=F32) / dim)
    ang = pos.astype(F32)[:, None] * inv[None, :]
    ang = jnp.concatenate([ang, ang], axis=-1)
    return jnp.cos(ang), jnp.sin(ang)


def _rope_pack(cos, sin, half):
    width = cos.shape[1]
    lane = jnp.arange(width)
    first = (lane % (2 * half)) < half
    tabs = [cos, jnp.where(first, -sin, 0.0), jnp.where(first, 0.0, sin)]
    return jnp.stack([jnp.pad(t, ((0, 0), (0, LANES - width))) for t in tabs])


def _t5_bucket(rel):
    half = N_BUCKETS // 2
    max_exact = half // 2
    n = jnp.abs(rel)
    large = max_exact + (jnp.log(jnp.maximum(n, 1).astype(F32) / max_exact)
                         / math.log(MAX_DISTANCE / max_exact) * (half - max_exact)).astype(jnp.int32)
    large = jnp.minimum(large, half - 1)
    return jnp.where(rel > 0, half, 0) + jnp.where(n < max_exact, n, large)


def _toeplitz(fn, rows, cols, offset):
    n = rows + cols + 1
    vec = fn(jnp.arange(n) - rows + offset)
    flat = jnp.tile(vec, (1, rows))[:, :rows * (n - 1)]
    return flat.reshape(vec.shape[0], rows, n - 1)[:, :, rows:rows + cols]


TQ_A, TQ_B, TQ_D = 256, 512, 256
TK_DENSE = 1024


def kernel(x, c, w_ada, b_ada, norm1, w_in, a_q_norm, a_k_norm, b_q_norm, b_kv_norm, b_w_uq, b_w_ukv,
           c_sink, d_lambda, d_sub_norm, w_out, norm2, w_ff1, w_ff2, rel_bias, final_norm):
    _, s, d = x.shape
    depth = w_ada.shape[0]
    hd = HEAD_DIM
    x2 = x.reshape(s, d)

    t = jnp.arange(s)
    cos_r, sin_r = _rope_tables(t // GRID_W, hd // 2)
    cos_c, sin_c = _rope_tables(t % GRID_W, hd // 2)
    ax_tab = _rope_pack(jnp.concatenate([cos_r, cos_c], -1), jnp.concatenate([sin_r, sin_c], -1), hd // 4)
    sq_tab = _rope_pack(*_rope_tables(t, B_ROPE), B_ROPE // 2)

    bias_fn = lambda heads: (lambda rel: (rel_bias[_t5_bucket(rel)][:, heads] * LOG2E).T)
    c_heads = jnp.arange(C_HEADS)
    d_heads = C_HEADS + jnp.arange(D_HEADS)
    rel_c = jnp.arange(3 * WINDOW)[None, :] - WINDOW - jnp.arange(WINDOW)[:, None]
    bias_c = jnp.where(jnp.abs(rel_c)[None] <= WINDOW, _toeplitz(bias_fn(c_heads), WINDOW, 3 * WINDOW, -WINDOW),
                       NEG_BIG)
    n_off = TK_DENSE // TQ_D + 2
    bias_d = jnp.stack([_toeplitz(bias_fn(d_heads), TQ_D, TK_DENSE, o * TQ_D - TK_DENSE)
                        for o in range(n_off)], axis=1)
    far_d = bias_fn(d_heads)(jnp.array([-MAX_DISTANCE, MAX_DISTANCE]))

    mod_all = _ada_mod(c, w_ada, b_ada)

    for l in range(depth):
        mod = mod_all[l]
        lam_init = 0.8 - 0.6 * math.exp(-0.3 * l)
        lf = d_lambda[l].astype(F32)
        lam = jnp.exp(jnp.sum(lf[0] * lf[1])) - jnp.exp(jnp.sum(lf[2] * lf[3])) + lam_init
        scal = jnp.stack([lam, jnp.asarray(1.0 - lam_init, F32)])

        w_in_l = w_in[l]
        w_in_p = jnp.concatenate([w_in_l[:, :COL_B_KR + B_ROPE], jnp.zeros((d, B_KR_PAD), F32),
                                  w_in_l[:, COL_B_KR + B_ROPE:]], axis=1).astype(BF16)
        wuq = jnp.pad(b_w_uq[l].reshape(B_Q_RANK, B_HEADS, B_NOPE + B_ROPE),
                      ((0, 0), (0, 0), (0, B_QK_PAD - B_NOPE - B_ROPE))).reshape(B_Q_RANK, -1).astype(BF16)
        wukv = b_w_ukv[l].astype(BF16)

        qkv = _norm_matmul(x2, norm1[l], mod, w_in_p, shift_row=0, scale_row=1, relu2=False,
                           out_dtype=F32, tm=512, tn=IN_COLS_PAD // 2, name="in_proj")
        (qa, ka, va, qb, kb, vb, qc, kc, vc, qd, kd, vd) = _prep(
            qkv, ax_tab, sq_tab, a_q_norm[l], a_k_norm[l], b_q_norm[l], b_kv_norm[l], wuq, wukv)

        oa = _flash(qa, ka, va, groups=A_KV_HEADS, groups_stacked=A_HEADS // A_KV_HEADS, dk=hd,
                    tq=TQ_A, tk=TK_DENSE, name="flash_a")
        ob = _flash(qb, kb, vb, groups=B_HEADS, groups_stacked=1, dk=B_QK_PAD,
                    tq=TQ_B, tk=TK_DENSE, name="flash_b")
        oc = _window(qc, kc, vc, bias_c, c_sink[l])
        od = _flash(qd, kd, vd, groups=D_HEADS, groups_stacked=2, dk=hd, tq=TQ_D, tk=TK_DENSE,
                    name="flash_d", diff_args=(bias_d, far_d, scal, d_sub_norm[l]))

        w_out_l = w_out[l].astype(BF16)
        heads = [oa, ob, oc, od]
        w_parts = [w_out_l[k * 4 * hd:(k + 1) * 4 * hd] for k in range(4)]
        x2 = _matmul_res(heads, w_parts, x2, mod, gate_row=2, tm=512, tn=1024, name="out_proj")

        hid = _norm_matmul(x2, norm2[l], mod, w_ff1[l].astype(BF16), shift_row=3, scale_row=4, relu2=True,
                           out_dtype=BF16, tm=512, tn=1024, name="ff1")
        x2 = _matmul_res([hid], [w_ff2[l].astype(BF16)], x2, mod, gate_row=5, tm=512, tn=512, name="ff2")

    return _final_norm(x2, final_norm).reshape(1, s, d)
```

```python
import functools
import math

import jax
import jax.numpy as jnp
from jax import lax
from jax.experimental import pallas as pl
from jax.experimental.pallas import tpu as pltpu

F32 = jnp.float32
BF16 = jnp.bfloat16

HEAD_DIM = 128
GRID_W = 64
ROPE_THETA = 10000.0
EPS = 1e-6
A_HEADS, A_KV_HEADS = 4, 2
B_HEADS, B_NOPE, B_ROPE, B_Q_RANK, B_KV_RANK = 4, 128, 64, 384, 256
C_HEADS, C_KV_HEADS, WINDOW = 4, 2, 128
D_HEADS = 4
D_HALF = HEAD_DIM // 2
N_BUCKETS, MAX_DISTANCE = 32, 128
N_MOD = 6

LOG2E = math.log2(math.e)
NEG_BIG = -1e30

LANES = 128
VMEM_LIMIT = 56 * 1024 * 1024

B_KR_PAD = LANES - B_ROPE
COL_A_Q = 0
COL_A_K = COL_A_Q + A_HEADS * HEAD_DIM
COL_A_V = COL_A_K + A_KV_HEADS * HEAD_DIM
COL_B_CQ = COL_A_V + A_KV_HEADS * HEAD_DIM
COL_B_CKV = COL_B_CQ + B_Q_RANK
COL_B_KR = COL_B_CKV + B_KV_RANK
COL_C_Q = COL_B_KR + B_ROPE + B_KR_PAD
COL_C_K = COL_C_Q + C_HEADS * HEAD_DIM
COL_C_V = COL_C_K + C_KV_HEADS * HEAD_DIM
COL_D_Q = COL_C_V + C_KV_HEADS * HEAD_DIM
COL_D_K = COL_D_Q + D_HEADS * HEAD_DIM
COL_D_V = COL_D_K + D_HEADS * HEAD_DIM
IN_COLS_PAD = COL_D_V + D_HEADS * HEAD_DIM
B_QK_PAD = 2 * LANES


def _params(sem):
    return pltpu.CompilerParams(dimension_semantics=sem, vmem_limit_bytes=VMEM_LIMIT)


def _ada_kernel(c_ref, w_ref, b_ref, o_ref):
    c = c_ref[...]
    cs = c * (1.0 / (1.0 + jnp.exp(-c)))
    cs8 = jnp.broadcast_to(cs, (8, cs.shape[1])).astype(BF16)
    r = jnp.dot(cs8, w_ref[0].astype(BF16), preferred_element_type=F32)
    o_ref[0] = r[0:1] + b_ref[0]


def _ada_mod(c, w_ada, b_ada, tn=1024):
    depth, d, n = w_ada.shape
    out = pl.pallas_call(
        _ada_kernel,
        grid=(depth, n // tn),
        in_specs=[pl.BlockSpec((1, d), lambda l, j: (0, 0)),
                  pl.BlockSpec((1, d, tn), lambda l, j: (l, 0, j)),
                  pl.BlockSpec((1, 1, tn), lambda l, j: (l, 0, j))],
        out_specs=pl.BlockSpec((1, 1, tn), lambda l, j: (l, 0, j)),
        out_shape=jax.ShapeDtypeStruct((depth, 1, n), F32),
        compiler_params=_params(("arbitrary", "arbitrary")),
        name="ada_mod",
    )(c, w_ada, b_ada.reshape(depth, 1, n))
    return out.reshape(depth, N_MOD, d)


def _norm_mm_kernel(x_ref, g_ref, mod_ref, w_ref, o_ref, h_ref, *, shift_row, scale_row, relu2):
    @pl.when(pl.program_id(1) == 0)
    def _():
        x = x_ref[...]
        y = x * lax.rsqrt(jnp.mean(x * x, axis=-1, keepdims=True) + EPS) * g_ref[...]
        h = y * (1.0 + mod_ref[scale_row:scale_row + 1, :]) + mod_ref[shift_row:shift_row + 1, :]
        h_ref[...] = h.astype(BF16)

    acc = jnp.dot(h_ref[...], w_ref[...], preferred_element_type=F32)
    if relu2:
        acc = jnp.square(jnp.maximum(acc, 0.0))
    o_ref[...] = acc.astype(o_ref.dtype)


def _norm_matmul(x, gain, mod, w, *, shift_row, scale_row, relu2, out_dtype, tm, tn, name):
    s, d = x.shape
    n = w.shape[1]
    return pl.pallas_call(
        functools.partial(_norm_mm_kernel, shift_row=shift_row, scale_row=scale_row, relu2=relu2),
        grid=(s // tm, n // tn),
        in_specs=[pl.BlockSpec((tm, d), lambda i, j: (i, 0)),
                  pl.BlockSpec((1, d), lambda i, j: (0, 0)),
                  pl.BlockSpec((N_MOD, d), lambda i, j: (0, 0)),
                  pl.BlockSpec((d, tn), lambda i, j: (0, j))],
        out_specs=pl.BlockSpec((tm, tn), lambda i, j: (i, j)),
        out_shape=jax.ShapeDtypeStruct((s, n), out_dtype),
        scratch_shapes=[pltpu.VMEM((tm, d), BF16)],
        compiler_params=_params(("arbitrary", "arbitrary")),
        name=name,
    )(x, gain.reshape(1, d), mod, w)


def _mm_res_kernel(*refs, n_in, gate_row):
    a_refs, w_refs = refs[:n_in], refs[n_in:2 * n_in]
    x_ref, mod_ref, o_ref = refs[2 * n_in:]
    acc = jnp.dot(a_refs[0][...], w_refs[0][...], preferred_element_type=F32)
    for a_ref, w_ref in zip(a_refs[1:], w_refs[1:]):
        acc = acc + jnp.dot(a_ref[...], w_ref[...], preferred_element_type=F32)
    o_ref[...] = x_ref[...] + mod_ref[gate_row:gate_row + 1, :] * acc


def _matmul_res(a_list, w_list, x, mod, *, gate_row, tm, tn, name):
    s, d = x.shape
    n_in = len(a_list)
    in_specs = ([pl.BlockSpec((tm, a.shape[1]), lambda i, j: (i, 0)) for a in a_list]
                + [pl.BlockSpec((w.shape[0], tn), lambda i, j: (0, j)) for w in w_list]
                + [pl.BlockSpec((tm, tn), lambda i, j: (i, j)),
                   pl.BlockSpec((N_MOD, tn), lambda i, j: (0, j))])
    return pl.pallas_call(
        functools.partial(_mm_res_kernel, n_in=n_in, gate_row=gate_row),
        grid=(s // tm, d // tn),
        in_specs=in_specs,
        out_specs=pl.BlockSpec((tm, tn), lambda i, j: (i, j)),
        out_shape=jax.ShapeDtypeStruct((s, d), F32),
        compiler_params=_params(("arbitrary", "arbitrary")),
        name=name,
    )(*a_list, *w_list, x, mod)


def _rope(x, cos, sin_lo, sin_hi):
    return (x * cos + pltpu.roll(x, LANES - 32, 1) * sin_lo + pltpu.roll(x, 32, 1) * sin_hi)


def _rms(x, gain):
    return x * lax.rsqrt(jnp.mean(x * x, axis=-1, keepdims=True) + EPS) * gain


def _prep_kernel(qkv_ref, ax_ref, sq_ref, aqn_ref, akn_ref, bqn_ref, bkvn_ref, wuq_ref, wukv_ref,
                 qa_ref, ka_ref, va_ref, qb_ref, kb_ref, vb_ref, qc_ref, kc_ref, vc_ref,
                 qd_ref, kd_ref, vd_ref):
    hd = HEAD_DIM
    ax_cos, ax_lo, ax_hi = ax_ref[0], ax_ref[1], ax_ref[2]
    sq_cos, sq_lo, sq_hi = sq_ref[0], sq_ref[1], sq_ref[2]

    def col(start, width=hd):
        return qkv_ref[:, start:start + width]

    sc_a = HEAD_DIM ** -0.5 * LOG2E
    for h in range(A_HEADS):
        q = _rope(_rms(col(COL_A_Q + h * hd), aqn_ref[...]), ax_cos, ax_lo, ax_hi)
        qa_ref[h * hd:(h + 1) * hd, :] = (q * sc_a).T.astype(BF16)
    for h in range(A_KV_HEADS):
        k = _rope(_rms(col(COL_A_K + h * hd), akn_ref[...]), ax_cos, ax_lo, ax_hi)
        ka_ref[:, h * hd:(h + 1) * hd] = k.astype(BF16)
        va_ref[h, 0] = col(COL_A_V + h * hd).T.astype(BF16)

    sc_b = (B_NOPE + B_ROPE) ** -0.5 * LOG2E
    cq = _rms(col(COL_B_CQ, B_Q_RANK), bqn_ref[...]).astype(BF16)
    qb = jnp.dot(cq, wuq_ref[...], preferred_element_type=F32)
    ckv = _rms(col(COL_B_CKV, B_KV_RANK), bkvn_ref[...]).astype(BF16)
    kvb = jnp.dot(ckv, wukv_ref[...], preferred_element_type=F32)
    kr = _rope(col(COL_B_KR, LANES), sq_cos, sq_lo, sq_hi).astype(BF16)
    for h in range(B_HEADS):
        base = h * B_QK_PAD
        qb_ref[base:base + hd, :] = (qb[:, base:base + hd] * sc_b).T.astype(BF16)
        q_r = _rope(qb[:, base + hd:base + 2 * hd], sq_cos, sq_lo, sq_hi)
        qb_ref[base + hd:base + 2 * hd, :] = (q_r * sc_b).T.astype(BF16)
        kb_ref[:, base:base + hd] = kvb[:, base:base + hd].astype(BF16)
        kb_ref[:, base + hd:base + 2 * hd] = kr
        vb_ref[h, 0] = kvb[:, base + hd:base + 2 * hd].T.astype(BF16)

    sc_c = HEAD_DIM ** -0.5 * LOG2E
    qc_ref[...] = (col(COL_C_Q, C_HEADS * hd) * sc_c).astype(BF16)
    kc_ref[...] = col(COL_C_K, C_KV_HEADS * hd).astype(BF16)
    vc_ref[...] = col(COL_C_V, C_KV_HEADS * hd).astype(BF16)

    sc_d = D_HALF ** -0.5 * LOG2E
    lane = lax.broadcasted_iota(jnp.int32, (qkv_ref.shape[0], hd), 1)
    lo = lane < D_HALF
    for h in range(D_HEADS):
        q = col(COL_D_Q + h * hd) * sc_d
        qd_ref[2 * h * hd:(2 * h + 1) * hd, :] = jnp.where(lo, q, 0.0).T.astype(BF16)
        qd_ref[(2 * h + 1) * hd:(2 * h + 2) * hd, :] = jnp.where(lo, 0.0, q).T.astype(BF16)
        vd_ref[h, 0] = col(COL_D_V + h * hd).T.astype(BF16)
    kd_ref[...] = col(COL_D_K, D_HEADS * hd).astype(BF16)


def _prep(qkv, ax_tab, sq_tab, aqn, akn, bqn, bkvn, wuq, wukv, tm=512):
    s = qkv.shape[0]
    hd = HEAD_DIM
    row = lambda i: (i, 0)
    full = lambda i: (0, 0)

    def row_major(w):
        return pl.BlockSpec((tm, w), row), jax.ShapeDtypeStruct((s, w), BF16)

    def feat_major(w):
        return pl.BlockSpec((w, tm), lambda i: (0, i)), jax.ShapeDtypeStruct((w, s), BF16)

    def val_t(h):
        return (pl.BlockSpec((h, 1, hd, tm), lambda i: (0, i, 0, 0)),
                jax.ShapeDtypeStruct((h, s // tm, hd, tm), BF16))

    outs = [feat_major(A_HEADS * hd), row_major(A_KV_HEADS * hd), val_t(A_KV_HEADS),
            feat_major(B_HEADS * B_QK_PAD), row_major(B_HEADS * B_QK_PAD), val_t(B_HEADS),
            row_major(C_HEADS * hd), row_major(C_KV_HEADS * hd), row_major(C_KV_HEADS * hd),
            feat_major(2 * D_HEADS * hd), row_major(D_HEADS * hd), val_t(D_HEADS)]
    return pl.pallas_call(
        _prep_kernel,
        grid=(s // tm,),
        in_specs=[pl.BlockSpec((tm, IN_COLS_PAD), row),
                  pl.BlockSpec((3, tm, LANES), lambda i: (0, i, 0)),
                  pl.BlockSpec((3, tm, LANES), lambda i: (0, i, 0)),
                  pl.BlockSpec((1, hd), full), pl.BlockSpec((1, hd), full),
                  pl.BlockSpec((1, B_Q_RANK), full), pl.BlockSpec((1, B_KV_RANK), full),
                  pl.BlockSpec(wuq.shape, full), pl.BlockSpec(wukv.shape, full)],
        out_specs=[spec for spec, _ in outs],
        out_shape=[shape for _, shape in outs],
        compiler_params=_params(("arbitrary",)),
        name="prep",
    )(qkv, ax_tab, sq_tab, aqn.reshape(1, hd), akn.reshape(1, hd),
      bqn.reshape(1, B_Q_RANK), bkvn.reshape(1, B_KV_RANK), wuq, wukv)


def _flash_kernel(*refs, groups_stacked, dk, tq, tk, tv, seq, diff):
    if diff:
        q_ref, k_ref, v_ref, bias_ref, scal_ref, gain_ref, o_ref, s_scr = refs
    else:
        q_ref, k_ref, v_ref, o_ref, s_scr = refs
    r_stack = groups_stacked
    cols = r_stack * tq
    i = pl.program_id(1)
    n_chunks = seq // tk
    n_sub = tk // tv
    q0 = i * tq
    q_t = jnp.concatenate([q_ref[r * dk:(r + 1) * dk, :] for r in range(r_stack)], axis=1)

    def scores(c, slot):
        start = pl.multiple_of(c * tk, tk)
        s_scr[slot] = jnp.dot(k_ref[pl.ds(start, tk), :], q_t, preferred_element_type=F32)

    def consume(c, slot, carry):
        m, l, acc = carry
        s = s_scr[slot]
        if diff:
            n_near = tk // tq + 2
            off = c * tk - q0
            idx = jnp.where(off < -tk, n_near, jnp.where(off > tq, n_near + 1, (off + tk) // tq))
            tile = bias_ref[0, idx]
            s = jnp.concatenate([s[:, r * tq:(r + 1) * tq] + tile for r in range(r_stack)], axis=1)
        m_new = jnp.maximum(m, jnp.max(s, axis=0, keepdims=True))
        alpha = jnp.exp2(m - m_new)
        p = jnp.exp2(s - m_new)
        l = alpha * l + jnp.sum(p, axis=0, keepdims=True)
        pb = p.astype(BF16)
        pv = jnp.dot(v_ref[0, c * n_sub], pb[0:tv], preferred_element_type=F32)
        for j in range(1, n_sub):
            pv = pv + jnp.dot(v_ref[0, c * n_sub + j], pb[j * tv:(j + 1) * tv], preferred_element_type=F32)
        return m_new, l, alpha * acc + pv

    init = (jnp.full((1, cols), NEG_BIG, F32), jnp.zeros((1, cols), F32),
            jnp.zeros((HEAD_DIM, cols), F32))

    scores(0, 0)

    def pair(cc, carry):
        c0 = 2 * cc
        scores(c0 + 1, 1)
        carry = consume(c0, 0, carry)
        scores(c0 + 2, 0)
        return consume(c0 + 1, 1, carry)

    carry = lax.fori_loop(0, n_chunks // 2 - 1, pair, init)
    scores(n_chunks - 1, 1)
    carry = consume(n_chunks - 2, 0, carry)
    m, l, acc = consume(n_chunks - 1, 1, carry)

    o_t = acc / l
    o = [o_t[:, r * tq:(r + 1) * tq].T for r in range(r_stack)]
    if not diff:
        for r in range(r_stack):
            o_ref[:, r * HEAD_DIM:(r + 1) * HEAD_DIM] = o[r].astype(o_ref.dtype)
        return
    lam = scal_ref[0]
    od = o[0] - lam * o[1]
    y = od * lax.rsqrt(jnp.mean(od * od, axis=-1, keepdims=True) + EPS) * gain_ref[...]
    o_ref[...] = (y * scal_ref[1]).astype(o_ref.dtype)


def _flash(q_t, k, v_t, *, groups, groups_stacked, dk, tq, tk, name, diff_args=None):
    s = k.shape[0]
    n_vblk, tv = v_t.shape[1], v_t.shape[3]
    assert (s // tk) % 2 == 0 and s // tk >= 2 and tk % tv == 0
    diff = diff_args is not None
    kern = functools.partial(_flash_kernel, groups_stacked=groups_stacked, dk=dk, tq=tq, tk=tk, tv=tv,
                             seq=s, diff=diff)
    in_specs = [pl.BlockSpec((groups_stacked * dk, tq), lambda g, i: (g, i)),
                pl.BlockSpec((s, dk), lambda g, i: (0, g)),
                pl.BlockSpec((1, n_vblk, HEAD_DIM, tv), lambda g, i: (g, 0, 0, 0))]
    args = [q_t, k, v_t]
    if diff:
        bias, scal, gain = diff_args
        in_specs += [pl.BlockSpec((1,) + bias.shape[1:], lambda g, i: (g, 0, 0, 0)),
                     pl.BlockSpec(memory_space=pltpu.SMEM),
                     pl.BlockSpec((1, HEAD_DIM), lambda g, i: (0, 0))]
        args += [bias, scal, gain.reshape(1, HEAD_DIM)]
        out_w = HEAD_DIM
    else:
        out_w = groups_stacked * HEAD_DIM
    return pl.pallas_call(
        kern,
        grid=(groups, s // tq),
        in_specs=in_specs,
        out_specs=pl.BlockSpec((tq, out_w), lambda g, i: (i, g)),
        out_shape=jax.ShapeDtypeStruct((s, groups * out_w), BF16),
        scratch_shapes=[pltpu.VMEM((2, tk, groups_stacked * tq), F32)],
        compiler_params=_params(("arbitrary", "arbitrary")),
        name=name,
    )(*args)


def _window_kernel(q_ref, kp_ref, kc_ref, kn_ref, vp_ref, vc_ref, vn_ref, bias_ref, sink_ref, o_ref,
                   kbuf, vbuf, *, tw, seq):
    w = WINDOW
    h = pl.program_id(0)
    i = pl.program_id(1)
    kbuf[0:w] = kp_ref[tw - w:tw]
    kbuf[w:w + tw] = kc_ref[...]
    kbuf[w + tw:w + tw + w] = kn_ref[0:w]
    vbuf[0:w] = vp_ref[tw - w:tw]
    vbuf[w:w + tw] = vc_ref[...]
    vbuf[w + tw:w + tw + w] = vn_ref[0:w]
    sink = sink_ref[h] * LOG2E
    bias = bias_ref[0]
    col = lax.broadcasted_iota(jnp.int32, (w, 3 * w), 1)
    for j in range(tw // w):
        qj = q_ref[j * w:(j + 1) * w, :]
        kw = kbuf[j * w:(j + 3) * w]
        vw = vbuf[j * w:(j + 3) * w]
        s = lax.dot_general(qj, kw, (((1,), (1,)), ((), ())), preferred_element_type=F32) + bias
        kpos = i * tw + (j - 1) * w + col
        s = jnp.where((kpos >= 0) & (kpos < seq), s, NEG_BIG)
        m = jnp.maximum(jnp.max(s, axis=-1, keepdims=True), sink)
        p = jnp.exp2(s - m)
        denom = jnp.sum(p, axis=-1, keepdims=True) + jnp.exp2(sink - m)
        o = jnp.dot(p.astype(BF16), vw, preferred_element_type=F32) / denom
        o_ref[j * w:(j + 1) * w, :] = o.astype(o_ref.dtype)


def _window(q, k, v, bias, sink, tw=1024):
    s = q.shape[0]
    hd = HEAD_DIM
    nb = s // tw
    grp = C_HEADS // C_KV_HEADS
    prev = lambda h, i: (jnp.maximum(i - 1, 0), h // grp)
    cur = lambda h, i: (i, h // grp)
    nxt = lambda h, i: (jnp.minimum(i + 1, nb - 1), h // grp)
    kv_specs = [pl.BlockSpec((tw, hd), f) for f in (prev, cur, nxt)]
    return pl.pallas_call(
        functools.partial(_window_kernel, tw=tw, seq=s),
        grid=(C_HEADS, nb),
        in_specs=[pl.BlockSpec((tw, hd), lambda h, i: (i, h))] + kv_specs + kv_specs
                 + [pl.BlockSpec((1, WINDOW, 3 * WINDOW), lambda h, i: (h, 0, 0)),
                    pl.BlockSpec(memory_space=pltpu.SMEM)],
        out_specs=pl.BlockSpec((tw, hd), lambda h, i: (i, h)),
        out_shape=jax.ShapeDtypeStruct((s, C_HEADS * hd), BF16),
        scratch_shapes=[pltpu.VMEM((tw + 2 * WINDOW, hd), BF16),
                        pltpu.VMEM((tw + 2 * WINDOW, hd), BF16)],
        compiler_params=_params(("arbitrary", "arbitrary")),
        name="window_attn",
    )(q, k, k, k, v, v, v, bias, sink)


def _final_norm_kernel(x_ref, g_ref, o_ref):
    o_ref[...] = _rms(x_ref[...], g_ref[...])


def _final_norm(x, gain, tm=512):
    s, d = x.shape
    return pl.pallas_call(
        _final_norm_kernel,
        grid=(s // tm,),
        in_specs=[pl.BlockSpec((tm, d), lambda i: (i, 0)), pl.BlockSpec((1, d), lambda i: (0, 0))],
        out_specs=pl.BlockSpec((tm, d), lambda i: (i, 0)),
        out_shape=jax.ShapeDtypeStruct((s, d), F32),
        compiler_params=_params(("arbitrary",)),
        name="final_norm",
    )(x, gain.reshape(1, d))


def _rope_tables(pos, dim):
    inv = ROPE_THETA ** (-jnp.arange(0, dim, 2, dtype=F32) / dim)
    ang = pos.astype(F32)[:, None] * inv[None, :]
    ang = jnp.concatenate([ang, ang], axis=-1)
    return jnp.cos(ang), jnp.sin(ang)


def _rope_pack(cos, sin, half):
    width = cos.shape[1]
    lane = jnp.arange(width)
    first = (lane % (2 * half)) < half
    tabs = [cos, jnp.where(first, -sin, 0.0), jnp.where(first, 0.0, sin)]
    return jnp.stack([jnp.pad(t, ((0, 0), (0, LANES - width))) for t in tabs])


def _t5_bucket(rel):
    half = N_BUCKETS // 2
    max_exact = half // 2
    n = jnp.abs(rel)
    large = max_exact + (jnp.log(jnp.maximum(n, 1).astype(F32) / max_exact)
                         / math.log(MAX_DISTANCE / max_exact) * (half - max_exact)).astype(jnp.int32)
    large = jnp.minimum(large, half - 1)
    return jnp.where(rel > 0, half, 0) + jnp.where(n < max_exact, n, large)


def _toeplitz(fn, rows, cols, offset):
    n = rows + cols + 1
    vec = fn(jnp.arange(n) - rows + offset)
    flat = jnp.tile(vec, (1, rows))[:, :rows * (n - 1)]
    return flat.reshape(vec.shape[0], rows, n - 1)[:, :, rows:rows + cols]


TQ_A, TQ_B, TQ_D = 256, 512, 256
TK_DENSE = 1024


def kernel(x, c, w_ada, b_ada, norm1, w_in, a_q_norm, a_k_norm, b_q_norm, b_kv_norm, b_w_uq, b_w_ukv,
           c_sink, d_lambda, d_sub_norm, w_out, norm2, w_ff1, w_ff2, rel_bias, final_norm):
    _, s, d = x.shape
    depth = w_ada.shape[0]
    hd = HEAD_DIM
    x2 = x.reshape(s, d)

    t = jnp.arange(s)
    cos_r, sin_r = _rope_tables(t // GRID_W, hd // 2)
    cos_c, sin_c = _rope_tables(t % GRID_W, hd // 2)
    ax_tab = _rope_pack(jnp.concatenate([cos_r, cos_c], -1), jnp.concatenate([sin_r, sin_c], -1), hd // 4)
    sq_tab = _rope_pack(*_rope_tables(t, B_ROPE), B_ROPE // 2)

    bias_fn = lambda heads: (lambda rel: (rel_bias[_t5_bucket(rel)][:, heads] * LOG2E).T)
    c_heads = jnp.arange(C_HEADS)
    d_heads = C_HEADS + jnp.arange(D_HEADS)
    rel_c = jnp.arange(3 * WINDOW)[None, :] - WINDOW - jnp.arange(WINDOW)[:, None]
    bias_c = jnp.where(jnp.abs(rel_c)[None] <= WINDOW, _toeplitz(bias_fn(c_heads), WINDOW, 3 * WINDOW, -WINDOW),
                       NEG_BIG)
    n_off = TK_DENSE // TQ_D + 2
    bias_d = jnp.stack([_toeplitz(bias_fn(d_heads), TQ_D, TK_DENSE, o * TQ_D - TK_DENSE)
                        for o in range(n_off)], axis=1)
    far_d = bias_fn(d_heads)(jnp.array([-MAX_DISTANCE, MAX_DISTANCE]))
    bias_d = jnp.concatenate(
        [bias_d, jnp.broadcast_to(far_d[:, :, None, None], (D_HEADS, 2, TQ_D, TK_DENSE))], axis=1)
    bias_d = jnp.swapaxes(bias_d, 2, 3)

    mod_all = _ada_mod(c, w_ada, b_ada)

    for l in range(depth):
        mod = mod_all[l]
        lam_init = 0.8 - 0.6 * math.exp(-0.3 * l)
        lf = d_lambda[l].astype(F32)
        lam = jnp.exp(jnp.sum(lf[0] * lf[1])) - jnp.exp(jnp.sum(lf[2] * lf[3])) + lam_init
        scal = jnp.stack([lam, jnp.asarray(1.0 - lam_init, F32)])

        w_in_l = w_in[l]
        w_in_p = jnp.concatenate([w_in_l[:, :COL_B_KR + B_ROPE], jnp.zeros((d, B_KR_PAD), F32),
                                  w_in_l[:, COL_B_KR + B_ROPE:]], axis=1).astype(BF16)
        wuq = jnp.pad(b_w_uq[l].reshape(B_Q_RANK, B_HEADS, B_NOPE + B_ROPE),
                      ((0, 0), (0, 0), (0, B_QK_PAD - B_NOPE - B_ROPE))).reshape(B_Q_RANK, -1).astype(BF16)
        wukv = b_w_ukv[l].astype(BF16)

        qkv = _norm_matmul(x2, norm1[l], mod, w_in_p, shift_row=0, scale_row=1, relu2=False,
                           out_dtype=F32, tm=512, tn=IN_COLS_PAD // 2, name="in_proj")
        (qa, ka, va, qb, kb, vb, qc, kc, vc, qd, kd, vd) = _prep(
            qkv, ax_tab, sq_tab, a_q_norm[l], a_k_norm[l], b_q_norm[l], b_kv_norm[l], wuq, wukv)

        oa = _flash(qa, ka, va, groups=A_KV_HEADS, groups_stacked=A_HEADS // A_KV_HEADS, dk=hd,
                    tq=TQ_A, tk=TK_DENSE, name="flash_a")
        ob = _flash(qb, kb, vb, groups=B_HEADS, groups_stacked=1, dk=B_QK_PAD,
                    tq=TQ_B, tk=TK_DENSE, name="flash_b")
        oc = _window(qc, kc, vc, bias_c, c_sink[l])
        od = _flash(qd, kd, vd, groups=D_HEADS, groups_stacked=2, dk=hd, tq=TQ_D, tk=TK_DENSE,
                    name="flash_d", diff_args=(bias_d, scal, d_sub_norm[l]))

        w_out_l = w_out[l].astype(BF16)
        heads = [oa, ob, oc, od]
        w_parts = [w_out_l[k * 4 * hd:(k + 1) * 4 * hd] for k in range(4)]
        x2 = _matmul_res(heads, w_parts, x2, mod, gate_row=2, tm=512, tn=1024, name="out_proj")

        hid = _norm_matmul(x2, norm2[l], mod, w_ff1[l].astype(BF16), shift_row=3, scale_row=4, relu2=True,
                           out_dtype=BF16, tm=512, tn=1024, name="ff1")
        x2 = _matmul_res([hid], [w_ff2[l].astype(BF16)], x2, mod, gate_row=5, tm=512, tn=512, name="ff2")

    return _final_norm(x2, final_norm).reshape(1, s, d)
```

```python
import functools
import math

import jax
import jax.numpy as jnp
from jax import lax
from jax.experimental import pallas as pl
from jax.experimental.pallas import tpu as pltpu

F32 = jnp.float32
BF16 = jnp.bfloat16

HEAD_DIM = 128
GRID_W = 64
ROPE_THETA = 10000.0
EPS = 1e-6
A_HEADS, A_KV_HEADS = 4, 2
B_HEADS, B_NOPE, B_ROPE, B_Q_RANK, B_KV_RANK = 4, 128, 64, 384, 256
C_HEADS, C_KV_HEADS, WINDOW = 4, 2, 128
D_HEADS = 4
D_HALF = HEAD_DIM // 2
N_BUCKETS, MAX_DISTANCE = 32, 128
N_MOD = 6

LOG2E = math.log2(math.e)
NEG_BIG = -1e30

LANES = 128
MXU_DIM = 256
VMEM_LIMIT = 56 * 1024 * 1024

TV = MXU_DIM

B_KR_PAD = LANES - B_ROPE
COL_A_Q = 0
COL_A_K = COL_A_Q + A_HEADS * HEAD_DIM
COL_A_V = COL_A_K + A_KV_HEADS * HEAD_DIM
COL_B_CQ = COL_A_V + A_KV_HEADS * HEAD_DIM
COL_B_CKV = COL_B_CQ + B_Q_RANK
COL_B_KR = COL_B_CKV + B_KV_RANK
COL_C_Q = COL_B_KR + B_ROPE + B_KR_PAD
COL_C_K = COL_C_Q + C_HEADS * HEAD_DIM
COL_C_V = COL_C_K + C_KV_HEADS * HEAD_DIM
COL_D_Q = COL_C_V + C_KV_HEADS * HEAD_DIM
COL_D_K = COL_D_Q + D_HEADS * HEAD_DIM
COL_D_V = COL_D_K + D_HEADS * HEAD_DIM
IN_COLS_PAD = COL_D_V + D_HEADS * HEAD_DIM
B_QK_PAD = 2 * LANES


def _params(sem, flags=None):
    return pltpu.CompilerParams(dimension_semantics=sem, vmem_limit_bytes=VMEM_LIMIT, flags=flags)


def _ada_kernel(c_ref, w_ref, b_ref, o_ref):
    c = c_ref[...]
    cs = c * (1.0 / (1.0 + jnp.exp(-c)))
    cs8 = jnp.broadcast_to(cs, (8, cs.shape[1])).astype(BF16)
    r = jnp.dot(cs8, w_ref[0].astype(BF16), preferred_element_type=F32)
    o_ref[0] = r[0:1] + b_ref[0]


def _ada_mod(c, w_ada, b_ada, tn=1024):
    depth, d, n = w_ada.shape
    out = pl.pallas_call(
        _ada_kernel,
        grid=(depth, n // tn),
        in_specs=[pl.BlockSpec((1, d), lambda l, j: (0, 0)),
                  pl.BlockSpec((1, d, tn), lambda l, j: (l, 0, j)),
                  pl.BlockSpec((1, 1, tn), lambda l, j: (l, 0, j))],
        out_specs=pl.BlockSpec((1, 1, tn), lambda l, j: (l, 0, j)),
        out_shape=jax.ShapeDtypeStruct((depth, 1, n), F32),
        compiler_params=_params(("arbitrary", "arbitrary")),
        name="ada_mod",
    )(c, w_ada, b_ada.reshape(depth, 1, n))
    return out.reshape(depth, N_MOD, d)


def _norm_mm_kernel(x_ref, g_ref, mod_ref, w_ref, o_ref, h_ref, *, shift_row, scale_row, relu2):
    @pl.when(pl.program_id(1) == 0)
    def _():
        x = x_ref[...]
        y = x * lax.rsqrt(jnp.mean(x * x, axis=-1, keepdims=True) + EPS) * g_ref[...]
        h = y * (1.0 + mod_ref[scale_row:scale_row + 1, :]) + mod_ref[shift_row:shift_row + 1, :]
        h_ref[...] = h.astype(BF16)

    acc = jnp.dot(h_ref[...], w_ref[...], preferred_element_type=F32)
    if relu2:
        acc = jnp.square(jnp.maximum(acc, 0.0))
    o_ref[...] = acc.astype(o_ref.dtype)


def _norm_matmul(x, gain, mod, w, *, shift_row, scale_row, relu2, out_dtype, tm, tn, name):
    s, d = x.shape
    n = w.shape[1]
    return pl.pallas_call(
        functools.partial(_norm_mm_kernel, shift_row=shift_row, scale_row=scale_row, relu2=relu2),
        grid=(s // tm, n // tn),
        in_specs=[pl.BlockSpec((tm, d), lambda i, j: (i, 0)),
                  pl.BlockSpec((1, d), lambda i, j: (0, 0)),
                  pl.BlockSpec((N_MOD, d), lambda i, j: (0, 0)),
                  pl.BlockSpec((d, tn), lambda i, j: (0, j))],
        out_specs=pl.BlockSpec((tm, tn), lambda i, j: (i, j)),
        out_shape=jax.ShapeDtypeStruct((s, n), out_dtype),
        scratch_shapes=[pltpu.VMEM((tm, d), BF16)],
        compiler_params=_params(("arbitrary", "arbitrary")),
        name=name,
    )(x, gain.reshape(1, d), mod, w)


def _mm_res_kernel(*refs, n_in, gate_row):
    a_refs, w_refs = refs[:n_in], refs[n_in:2 * n_in]
    x_ref, mod_ref, o_ref = refs[2 * n_in:]
    acc = jnp.dot(a_refs[0][...], w_refs[0][...], preferred_element_type=F32)
    for a_ref, w_ref in zip(a_refs[1:], w_refs[1:]):
        acc = acc + jnp.dot(a_ref[...], w_ref[...], preferred_element_type=F32)
    o_ref[...] = x_ref[...] + mod_ref[gate_row:gate_row + 1, :] * acc


def _matmul_res(a_list, w_list, x, mod, *, gate_row, tm, tn, name):
    s, d = x.shape
    n_in = len(a_list)
    in_specs = ([pl.BlockSpec((tm, a.shape[1]), lambda i, j: (i, 0)) for a in a_list]
                + [pl.BlockSpec((w.shape[0], tn), lambda i, j: (0, j)) for w in w_list]
                + [pl.BlockSpec((tm, tn), lambda i, j: (i, j)),
                   pl.BlockSpec((N_MOD, tn), lambda i, j: (0, j))])
    return pl.pallas_call(
        functools.partial(_mm_res_kernel, n_in=n_in, gate_row=gate_row),
        grid=(s // tm, d // tn),
        in_specs=in_specs,
        out_specs=pl.BlockSpec((tm, tn), lambda i, j: (i, j)),
        out_shape=jax.ShapeDtypeStruct((s, d), F32),
        compiler_params=_params(("arbitrary", "arbitrary")),
        name=name,
    )(*a_list, *w_list, x, mod)


def _rope(x, cos, sin_lo, sin_hi):
    return (x * cos + pltpu.roll(x, LANES - 32, 1) * sin_lo + pltpu.roll(x, 32, 1) * sin_hi)


def _rms(x, gain):
    return x * lax.rsqrt(jnp.mean(x * x, axis=-1, keepdims=True) + EPS) * gain


def _store_val_t(ref, h, v):
    v_t = v.T.astype(BF16)
    n_blk, tv = ref.shape[1], ref.shape[3]
    for j in range(n_blk):
        ref[h, j] = v_t[:, j * tv:(j + 1) * tv]


def _prep_kernel(qkv_ref, ax_ref, sq_ref, aqn_ref, akn_ref, bqn_ref, bkvn_ref, wuq_ref, wukv_ref,
                 qa_ref, ka_ref, va_ref, qb_ref, kb_ref, vb_ref, qc_ref, kc_ref, vc_ref,
                 qd_ref, kd_ref, vd_ref):
    hd = HEAD_DIM
    ax_cos, ax_lo, ax_hi = ax_ref[0], ax_ref[1], ax_ref[2]
    sq_cos, sq_lo, sq_hi = sq_ref[0], sq_ref[1], sq_ref[2]

    def col(start, width=hd):
        return qkv_ref[:, start:start + width]

    sc_a = HEAD_DIM ** -0.5 * LOG2E
    for h in range(A_HEADS):
        q = _rope(_rms(col(COL_A_Q + h * hd), aqn_ref[...]), ax_cos, ax_lo, ax_hi)
        qa_ref[h * hd:(h + 1) * hd, :] = (q * sc_a).T.astype(BF16)
    for h in range(A_KV_HEADS):
        k = _rope(_rms(col(COL_A_K + h * hd), akn_ref[...]), ax_cos, ax_lo, ax_hi)
        ka_ref[:, h * hd:(h + 1) * hd] = k.astype(BF16)
        _store_val_t(va_ref, h, col(COL_A_V + h * hd))

    sc_b = (B_NOPE + B_ROPE) ** -0.5 * LOG2E
    cq = _rms(col(COL_B_CQ, B_Q_RANK), bqn_ref[...]).astype(BF16)
    qb = jnp.dot(cq, wuq_ref[...], preferred_element_type=F32)
    ckv = _rms(col(COL_B_CKV, B_KV_RANK), bkvn_ref[...]).astype(BF16)
    kvb = jnp.dot(ckv, wukv_ref[...], preferred_element_type=F32)
    kr = _rope(col(COL_B_KR, LANES), sq_cos, sq_lo, sq_hi).astype(BF16)
    for h in range(B_HEADS):
        base = h * B_QK_PAD
        qb_ref[base:base + hd, :] = (qb[:, base:base + hd] * sc_b).T.astype(BF16)
        q_r = _rope(qb[:, base + hd:base + 2 * hd], sq_cos, sq_lo, sq_hi)
        qb_ref[base + hd:base + 2 * hd, :] = (q_r * sc_b).T.astype(BF16)
        kb_ref[:, base:base + hd] = kvb[:, base:base + hd].astype(BF16)
        kb_ref[:, base + hd:base + 2 * hd] = kr
        _store_val_t(vb_ref, h, kvb[:, base + hd:base + 2 * hd])

    sc_c = HEAD_DIM ** -0.5 * LOG2E
    qc_ref[...] = (col(COL_C_Q, C_HEADS * hd) * sc_c).astype(BF16)
    kc_ref[...] = col(COL_C_K, C_KV_HEADS * hd).astype(BF16)
    vc_ref[...] = col(COL_C_V, C_KV_HEADS * hd).astype(BF16)

    sc_d = D_HALF ** -0.5 * LOG2E
    lane = lax.broadcasted_iota(jnp.int32, (qkv_ref.shape[0], hd), 1)
    lo = lane < D_HALF
    for h in range(D_HEADS):
        q = col(COL_D_Q + h * hd) * sc_d
        qd_ref[2 * h * hd:(2 * h + 1) * hd, :] = jnp.where(lo, q, 0.0).T.astype(BF16)
        qd_ref[(2 * h + 1) * hd:(2 * h + 2) * hd, :] = jnp.where(lo, 0.0, q).T.astype(BF16)
        _store_val_t(vd_ref, h, col(COL_D_V + h * hd))
    kd_ref[...] = col(COL_D_K, D_HEADS * hd).astype(BF16)


def _prep(qkv, ax_tab, sq_tab, aqn, akn, bqn, bkvn, wuq, wukv, tm=512):
    s = qkv.shape[0]
    hd = HEAD_DIM
    row = lambda i: (i, 0)
    full = lambda i: (0, 0)

    def row_major(w):
        return pl.BlockSpec((tm, w), row), jax.ShapeDtypeStruct((s, w), BF16)

    def feat_major(w):
        return pl.BlockSpec((w, tm), lambda i: (0, i)), jax.ShapeDtypeStruct((w, s), BF16)

    def val_t(h):
        return (pl.BlockSpec((h, tm // TV, hd, TV), lambda i: (0, i, 0, 0)),
                jax.ShapeDtypeStruct((h, s // TV, hd, TV), BF16))

    outs = [feat_major(A_HEADS * hd), row_major(A_KV_HEADS * hd), val_t(A_KV_HEADS),
            feat_major(B_HEADS * B_QK_PAD), row_major(B_HEADS * B_QK_PAD), val_t(B_HEADS),
            row_major(C_HEADS * hd), row_major(C_KV_HEADS * hd), row_major(C_KV_HEADS * hd),
            feat_major(2 * D_HEADS * hd), row_major(D_HEADS * hd), val_t(D_HEADS)]
    return pl.pallas_call(
        _prep_kernel,
        grid=(s // tm,),
        in_specs=[pl.BlockSpec((tm, IN_COLS_PAD), row),
                  pl.BlockSpec((3, tm, LANES), lambda i: (0, i, 0)),
                  pl.BlockSpec((3, tm, LANES), lambda i: (0, i, 0)),
                  pl.BlockSpec((1, hd), full), pl.BlockSpec((1, hd), full),
                  pl.BlockSpec((1, B_Q_RANK), full), pl.BlockSpec((1, B_KV_RANK), full),
                  pl.BlockSpec(wuq.shape, full), pl.BlockSpec(wukv.shape, full)],
        out_specs=[spec for spec, _ in outs],
        out_shape=[shape for _, shape in outs],
        compiler_params=_params(("arbitrary",)),
        name="prep",
    )(qkv, ax_tab, sq_tab, aqn.reshape(1, hd), akn.reshape(1, hd),
      bqn.reshape(1, B_Q_RANK), bkvn.reshape(1, B_KV_RANK), wuq, wukv)


def _flash_kernel(*refs, groups_stacked, dk, tq, tk, tv, seq, unroll, diff):
    if diff:
        q_ref, k_ref, v_ref, bias_ref, scal_ref, gain_ref, o_ref = refs[:7]
    else:
        q_ref, k_ref, v_ref, o_ref = refs[:4]
    s_scr = refs[-2:]
    r_stack = groups_stacked
    cols = r_stack * tq
    i = pl.program_id(1)
    n_chunks = seq // tk
    n_sub = tk // tv
    q0 = i * tq
    q_t = jnp.concatenate([q_ref[r * dk:(r + 1) * dk, :] for r in range(r_stack)], axis=1)

    def scores(c, slot):
        start = c * tk
        if not isinstance(start, int):
            start = pl.multiple_of(start, tk)
        s = jnp.dot(k_ref[pl.ds(start, tk), :], q_t, preferred_element_type=F32)
        if not diff:
            s_scr[slot][...] = s
            return jnp.max(s, axis=0, keepdims=True)
        n_near = tk // tq + 2
        off = c * tk - q0
        idx = jnp.where(off < -tk, n_near, jnp.where(off > tq, n_near + 1, (off + tk) // tq))
        tile = bias_ref[0, idx]
        m_parts = []
        for r in range(r_stack):
            s_r = s[:, r * tq:(r + 1) * tq] + tile
            s_scr[slot][:, r * tq:(r + 1) * tq] = s_r
            m_parts.append(jnp.max(s_r, axis=0, keepdims=True))
        return jnp.concatenate(m_parts, axis=1)

    def step(c, par, m_chunk, state, do_scores):
        m, l, acc = state
        m_new = jnp.maximum(m, m_chunk)
        alpha = jnp.exp2(m - m_new)
        l = alpha * l
        acc = alpha * acc
        m_next = scores(c + 1, 1 - par) if do_scores else None
        for j in range(n_sub):
            p = jnp.exp2(s_scr[par][j * tv:(j + 1) * tv, :] - m_new)
            l = l + jnp.sum(p, axis=0, keepdims=True)
            acc = acc + jnp.dot(v_ref[0, c * n_sub + j], p.astype(BF16), preferred_element_type=F32)
        return m_next, (m_new, l, acc)

    def group(base, carry, last):
        m_chunk, state = carry
        for u in range(unroll):
            m_chunk, state = step(base + u, u % 2, m_chunk, state, not (last and u == unroll - 1))
        return m_chunk, state

    carry = (scores(0, 0), (jnp.full((1, cols), NEG_BIG, F32), jnp.zeros((1, cols), F32),
                       jnp.zeros((HEAD_DIM, cols), F32)))
    n_groups = n_chunks // unroll
    if n_groups > 1:
        carry = lax.fori_loop(0, n_groups - 1, lambda t, cr: group(t * unroll, cr, False), carry)
    _, (m, l, acc) = group((n_groups - 1) * unroll, carry, True)

    o_t = acc / l
    o = [o_t[:, r * tq:(r + 1) * tq].T for r in range(r_stack)]
    if not diff:
        for r in range(r_stack):
            o_ref[:, r * HEAD_DIM:(r + 1) * HEAD_DIM] = o[r].astype(o_ref.dtype)
        return
    lam = scal_ref[0]
    od = o[0] - lam * o[1]
    y = od * lax.rsqrt(jnp.mean(od * od, axis=-1, keepdims=True) + EPS) * gain_ref[...]
    o_ref[...] = (y * scal_ref[1]).astype(o_ref.dtype)


def _flash(q_t, k, v_t, *, groups, groups_stacked, dk, tq, tk, name, diff_args=None):
    s = k.shape[0]
    n_vblk, tv = v_t.shape[1], v_t.shape[3]
    unroll = min(FLASH_UNROLL, s // tk)
    assert unroll % 2 == 0 and (s // tk) % unroll == 0 and tk % tv == 0
    diff = diff_args is not None
    kern = functools.partial(_flash_kernel, groups_stacked=groups_stacked, dk=dk, tq=tq, tk=tk, tv=tv,
                             seq=s, unroll=unroll, diff=diff)
    in_specs = [pl.BlockSpec((groups_stacked * dk, tq), lambda g, i: (g, i)),
                pl.BlockSpec((s, dk), lambda g, i: (0, g)),
                pl.BlockSpec((1, n_vblk, HEAD_DIM, tv), lambda g, i: (g, 0, 0, 0))]
    args = [q_t, k, v_t]
    if diff:
        bias, scal, gain = diff_args
        in_specs += [pl.BlockSpec((1,) + bias.shape[1:], lambda g, i: (g, 0, 0, 0)),
                     pl.BlockSpec(memory_space=pltpu.SMEM),
                     pl.BlockSpec((1, HEAD_DIM), lambda g, i: (0, 0))]
        args += [bias, scal, gain.reshape(1, HEAD_DIM)]
        out_w = HEAD_DIM
    else:
        out_w = groups_stacked * HEAD_DIM
    return pl.pallas_call(
        kern,
        grid=(groups, s // tq),
        in_specs=in_specs,
        out_specs=pl.BlockSpec((tq, out_w), lambda g, i: (i, g)),
        out_shape=jax.ShapeDtypeStruct((s, groups * out_w), BF16),
        scratch_shapes=[pltpu.VMEM((tk, groups_stacked * tq), F32)] * 2,
        compiler_params=_params(("arbitrary", "arbitrary"), FLASH_FLAGS),
        name=name,
    )(*args)


def _window_kernel(q_ref, kp_ref, kc_ref, kn_ref, vp_ref, vc_ref, vn_ref, bias_ref, sink_ref, o_ref,
                   kbuf, vbuf, *, tw, seq):
    w = WINDOW
    h = pl.program_id(0)
    i = pl.program_id(1)
    kbuf[0:w] = kp_ref[tw - w:tw]
    kbuf[w:w + tw] = kc_ref[...]
    kbuf[w + tw:w + tw + w] = kn_ref[0:w]
    vbuf[0:w] = vp_ref[tw - w:tw]
    vbuf[w:w + tw] = vc_ref[...]
    vbuf[w + tw:w + tw + w] = vn_ref[0:w]
    sink = sink_ref[h] * LOG2E
    bias = bias_ref[0]
    col = lax.broadcasted_iota(jnp.int32, (w, 3 * w), 1)
    for j in range(tw // w):
        qj = q_ref[j * w:(j + 1) * w, :]
        kw = kbuf[j * w:(j + 3) * w]
        vw = vbuf[j * w:(j + 3) * w]
        s = lax.dot_general(qj, kw, (((1,), (1,)), ((), ())), preferred_element_type=F32) + bias
        kpos = i * tw + (j - 1) * w + col
        s = jnp.where((kpos >= 0) & (kpos < seq), s, NEG_BIG)
        m = jnp.maximum(jnp.max(s, axis=-1, keepdims=True), sink)
        p = jnp.exp2(s - m)
        denom = jnp.sum(p, axis=-1, keepdims=True) + jnp.exp2(sink - m)
        o = jnp.dot(p.astype(BF16), vw, preferred_element_type=F32) / denom
        o_ref[j * w:(j + 1) * w, :] = o.astype(o_ref.dtype)


def _window(q, k, v, bias, sink, tw=1024):
    s = q.shape[0]
    hd = HEAD_DIM
    nb = s // tw
    grp = C_HEADS // C_KV_HEADS
    prev = lambda h, i: (jnp.maximum(i - 1, 0), h // grp)
    cur = lambda h, i: (i, h // grp)
    nxt = lambda h, i: (jnp.minimum(i + 1, nb - 1), h // grp)
    kv_specs = [pl.BlockSpec((tw, hd), f) for f in (prev, cur, nxt)]
    return pl.pallas_call(
        functools.partial(_window_kernel, tw=tw, seq=s),
        grid=(C_HEADS, nb),
        in_specs=[pl.BlockSpec((tw, hd), lambda h, i: (i, h))] + kv_specs + kv_specs
                 + [pl.BlockSpec((1, WINDOW, 3 * WINDOW), lambda h, i: (h, 0, 0)),
                    pl.BlockSpec(memory_space=pltpu.SMEM)],
        out_specs=pl.BlockSpec((tw, hd), lambda h, i: (i, h)),
        out_shape=jax.ShapeDtypeStruct((s, C_HEADS * hd), BF16),
        scratch_shapes=[pltpu.VMEM((tw + 2 * WINDOW, hd), BF16),
                        pltpu.VMEM((tw + 2 * WINDOW, hd), BF16)],
        compiler_params=_params(("arbitrary", "arbitrary")),
        name="window_attn",
    )(q, k, k, k, v, v, v, bias, sink)


def _final_norm_kernel(x_ref, g_ref, o_ref):
    o_ref[...] = _rms(x_ref[...], g_ref[...])


def _final_norm(x, gain, tm=512):
    s, d = x.shape
    return pl.pallas_call(
        _final_norm_kernel,
        grid=(s // tm,),
        in_specs=[pl.BlockSpec((tm, d), lambda i: (i, 0)), pl.BlockSpec((1, d), lambda i: (0, 0))],
        out_specs=pl.BlockSpec((tm, d), lambda i: (i, 0)),
        out_shape=jax.ShapeDtypeStruct((s, d), F32),
        compiler_params=_params(("arbitrary",)),
        name="final_norm",
    )(x, gain.reshape(1, d))


def _rope_tables(pos, dim):
    inv = ROPE_THETA ** (-jnp.arange(0, dim, 2, dtype=F32) / dim)
    ang = pos.astype(F32)[:, None] * inv[None, :]
    ang = jnp.concatenate([ang, ang], axis=-1)
    return jnp.cos(ang), jnp.sin(ang)


def _rope_pack(cos, sin, half):
    width = cos.shape[1]
    lane = jnp.arange(width)
    first = (lane % (2 * half)) < half
    tabs = [cos, jnp.where(first, -sin, 0.0), jnp.where(first, 0.0, sin)]
    return jnp.stack([jnp.pad(t, ((0, 0), (0, LANES - width))) for t in tabs])


def _t5_bucket(rel):
    half = N_BUCKETS // 2
    max_exact = half // 2
    n = jnp.abs(rel)
    large = max_exact + (jnp.log(jnp.maximum(n, 1).astype(F32) / max_exact)
                         / math.log(MAX_DISTANCE / max_exact) * (half - max_exact)).astype(jnp.int32)
    large = jnp.minimum(large, half - 1)
    return jnp.where(rel > 0, half, 0) + jnp.where(n < max_exact, n, large)


def _toeplitz(fn, rows, cols, offset):
    n = rows + cols + 1
    vec = fn(jnp.arange(n) - rows + offset)
    flat = jnp.tile(vec, (1, rows))[:, :rows * (n - 1)]
    return flat.reshape(vec.shape[0], rows, n - 1)[:, :, rows:rows + cols]


TQ_A, TQ_B, TQ_D = 256, 512, 256
TK_DENSE = 1024
FLASH_FLAGS = None
FLASH_UNROLL = 4


def kernel(x, c, w_ada, b_ada, norm1, w_in, a_q_norm, a_k_norm, b_q_norm, b_kv_norm, b_w_uq, b_w_ukv,
           c_sink, d_lambda, d_sub_norm, w_out, norm2, w_ff1, w_ff2, rel_bias, final_norm):
    _, s, d = x.shape
    depth = w_ada.shape[0]
    hd = HEAD_DIM
    x2 = x.reshape(s, d)

    t = jnp.arange(s)
    cos_r, sin_r = _rope_tables(t // GRID_W, hd // 2)
    cos_c, sin_c = _rope_tables(t % GRID_W, hd // 2)
    ax_tab = _rope_pack(jnp.concatenate([cos_r, cos_c], -1), jnp.concatenate([sin_r, sin_c], -1), hd // 4)
    sq_tab = _rope_pack(*_rope_tables(t, B_ROPE), B_ROPE // 2)

    bias_fn = lambda heads: (lambda rel: (rel_bias[_t5_bucket(rel)][:, heads] * LOG2E).T)
    c_heads = jnp.arange(C_HEADS)
    d_heads = C_HEADS + jnp.arange(D_HEADS)
    rel_c = jnp.arange(3 * WINDOW)[None, :] - WINDOW - jnp.arange(WINDOW)[:, None]
    bias_c = jnp.where(jnp.abs(rel_c)[None] <= WINDOW, _toeplitz(bias_fn(c_heads), WINDOW, 3 * WINDOW, -WINDOW),
                       NEG_BIG)
    n_off = TK_DENSE // TQ_D + 2
    bias_d = jnp.stack([_toeplitz(bias_fn(d_heads), TQ_D, TK_DENSE, o * TQ_D - TK_DENSE)
                        for o in range(n_off)], axis=1)
    far_d = bias_fn(d_heads)(jnp.array([-MAX_DISTANCE, MAX_DISTANCE]))
    bias_d = jnp.concatenate(
        [bias_d, jnp.broadcast_to(far_d[:, :, None, None], (D_HEADS, 2, TQ_D, TK_DENSE))], axis=1)
    bias_d = jnp.swapaxes(bias_d, 2, 3)

    mod_all = _ada_mod(c, w_ada, b_ada)

    for l in range(depth):
        mod = mod_all[l]
        lam_init = 0.8 - 0.6 * math.exp(-0.3 * l)
        lf = d_lambda[l].astype(F32)
        lam = jnp.exp(jnp.sum(lf[0] * lf[1])) - jnp.exp(jnp.sum(lf[2] * lf[3])) + lam_init
        scal = jnp.stack([lam, jnp.asarray(1.0 - lam_init, F32)])

        w_in_l = w_in[l]
        w_in_p = jnp.concatenate([w_in_l[:, :COL_B_KR + B_ROPE], jnp.zeros((d, B_KR_PAD), F32),
                                  w_in_l[:, COL_B_KR + B_ROPE:]], axis=1).astype(BF16)
        wuq = jnp.pad(b_w_uq[l].reshape(B_Q_RANK, B_HEADS, B_NOPE + B_ROPE),
                      ((0, 0), (0, 0), (0, B_QK_PAD - B_NOPE - B_ROPE))).reshape(B_Q_RANK, -1).astype(BF16)
        wukv = b_w_ukv[l].astype(BF16)

        qkv = _norm_matmul(x2, norm1[l], mod, w_in_p, shift_row=0, scale_row=1, relu2=False,
                           out_dtype=F32, tm=512, tn=IN_COLS_PAD // 2, name="in_proj")
        (qa, ka, va, qb, kb, vb, qc, kc, vc, qd, kd, vd) = _prep(
            qkv, ax_tab, sq_tab, a_q_norm[l], a_k_norm[l], b_q_norm[l], b_kv_norm[l], wuq, wukv)

        oa = _flash(qa, ka, va, groups=A_KV_HEADS, groups_stacked=A_HEADS // A_KV_HEADS, dk=hd,
                    tq=TQ_A, tk=TK_DENSE, name="flash_a")
        ob = _flash(qb, kb, vb, groups=B_HEADS, groups_stacked=1, dk=B_QK_PAD,
                    tq=TQ_B, tk=TK_DENSE, name="flash_b")
        oc = _window(qc, kc, vc, bias_c, c_sink[l])
        od = _flash(qd, kd, vd, groups=D_HEADS, groups_stacked=2, dk=hd, tq=TQ_D, tk=TK_DENSE,
                    name="flash_d", diff_args=(bias_d, scal, d_sub_norm[l]))

        w_out_l = w_out[l].astype(BF16)
        heads = [oa, ob, oc, od]
        w_parts = [w_out_l[k * 4 * hd:(k + 1) * 4 * hd] for k in range(4)]
        x2 = _matmul_res(heads, w_parts, x2, mod, gate_row=2, tm=512, tn=d, name="out_proj")

        hid = _norm_matmul(x2, norm2[l], mod, w_ff1[l].astype(BF16), shift_row=3, scale_row=4, relu2=True,
                           out_dtype=BF16, tm=1024, tn=1024, name="ff1")
        x2 = _matmul_res([hid], [w_ff2[l].astype(BF16)], x2, mod, gate_row=5, tm=512, tn=512, name="ff2")

    return _final_norm(x2, final_norm).reshape(1, s, d)
```

```python
import functools
import math

import jax
import jax.numpy as jnp
from jax import lax
from jax.experimental import pallas as pl
from jax.experimental.pallas import tpu as pltpu

F32 = jnp.float32
BF16 = jnp.bfloat16

HEAD_DIM = 128
GRID_W = 64
ROPE_THETA = 10000.0
EPS = 1e-6
A_HEADS, A_KV_HEADS = 4, 2
B_HEADS, B_NOPE, B_ROPE, B_Q_RANK, B_KV_RANK = 4, 128, 64, 384, 256
C_HEADS, C_KV_HEADS, WINDOW = 4, 2, 128
D_HEADS = 4
D_HALF = HEAD_DIM // 2
N_BUCKETS, MAX_DISTANCE = 32, 128
N_MOD = 6

LOG2E = math.log2(math.e)
NEG_BIG = -1e30

LANES = 128
MXU_DIM = 256
VMEM_LIMIT = 56 * 1024 * 1024

TV = MXU_DIM

B_KR_PAD = LANES - B_ROPE
COL_A_Q = 0
COL_A_K = COL_A_Q + A_HEADS * HEAD_DIM
COL_A_V = COL_A_K + A_KV_HEADS * HEAD_DIM
COL_B_CQ = COL_A_V + A_KV_HEADS * HEAD_DIM
COL_B_CKV = COL_B_CQ + B_Q_RANK
COL_B_KR = COL_B_CKV + B_KV_RANK
COL_C_Q = COL_B_KR + B_ROPE + B_KR_PAD
COL_C_K = COL_C_Q + C_HEADS * HEAD_DIM
COL_C_V = COL_C_K + C_KV_HEADS * HEAD_DIM
COL_D_Q = COL_C_V + C_KV_HEADS * HEAD_DIM
COL_D_K = COL_D_Q + D_HEADS * HEAD_DIM
COL_D_V = COL_D_K + D_HEADS * HEAD_DIM
IN_COLS_PAD = COL_D_V + D_HEADS * HEAD_DIM
B_QK_PAD = 2 * LANES


def _params(sem):
    return pltpu.CompilerParams(dimension_semantics=sem, vmem_limit_bytes=VMEM_LIMIT)


def _ada_kernel(c_ref, w_ref, b_ref, o_ref):
    c = c_ref[...]
    cs = c * (1.0 / (1.0 + jnp.exp(-c)))
    cs8 = jnp.broadcast_to(cs, (8, cs.shape[1])).astype(BF16)
    r = jnp.dot(cs8, w_ref[0].astype(BF16), preferred_element_type=F32)
    o_ref[0] = r[0:1] + b_ref[0]


def _ada_mod(c, w_ada, b_ada, tn=1024):
    depth, d, n = w_ada.shape
    out = pl.pallas_call(
        _ada_kernel,
        grid=(depth, n // tn),
        in_specs=[pl.BlockSpec((1, d), lambda l, j: (0, 0)),
                  pl.BlockSpec((1, d, tn), lambda l, j: (l, 0, j)),
                  pl.BlockSpec((1, 1, tn), lambda l, j: (l, 0, j))],
        out_specs=pl.BlockSpec((1, 1, tn), lambda l, j: (l, 0, j)),
        out_shape=jax.ShapeDtypeStruct((depth, 1, n), F32),
        compiler_params=_params(("arbitrary", "arbitrary")),
        name="ada_mod",
    )(c, w_ada, b_ada.reshape(depth, 1, n))
    return out.reshape(depth, N_MOD, d)


def _norm_mm_kernel(x_ref, g_ref, mod_ref, w_ref, o_ref, h_ref, *, shift_row, scale_row, relu2):
    @pl.when(pl.program_id(1) == 0)
    def _():
        x = x_ref[...]
        y = x * lax.rsqrt(jnp.mean(x * x, axis=-1, keepdims=True) + EPS) * g_ref[...]
        h = y * (1.0 + mod_ref[scale_row:scale_row + 1, :]) + mod_ref[shift_row:shift_row + 1, :]
        h_ref[...] = h.astype(BF16)

    acc = jnp.dot(h_ref[...], w_ref[...], preferred_element_type=F32)
    if relu2:
        acc = jnp.square(jnp.maximum(acc, 0.0))
    o_ref[...] = acc.astype(o_ref.dtype)


def _norm_matmul(x, gain, mod, w, *, shift_row, scale_row, relu2, out_dtype, tm, tn, name):
    s, d = x.shape
    n = w.shape[1]
    return pl.pallas_call(
        functools.partial(_norm_mm_kernel, shift_row=shift_row, scale_row=scale_row, relu2=relu2),
        grid=(s // tm, n // tn),
        in_specs=[pl.BlockSpec((tm, d), lambda i, j: (i, 0)),
                  pl.BlockSpec((1, d), lambda i, j: (0, 0)),
                  pl.BlockSpec((N_MOD, d), lambda i, j: (0, 0)),
                  pl.BlockSpec((d, tn), lambda i, j: (0, j))],
        out_specs=pl.BlockSpec((tm, tn), lambda i, j: (i, j)),
        out_shape=jax.ShapeDtypeStruct((s, n), out_dtype),
        scratch_shapes=[pltpu.VMEM((tm, d), BF16)],
        compiler_params=_params(("arbitrary", "arbitrary")),
        name=name,
    )(x, gain.reshape(1, d), mod, w)


def _mm_res_kernel(*refs, n_in, gate_row):
    a_refs, w_refs = refs[:n_in], refs[n_in:2 * n_in]
    x_ref, mod_ref, o_ref = refs[2 * n_in:]
    acc = jnp.dot(a_refs[0][...], w_refs[0][...], preferred_element_type=F32)
    for a_ref, w_ref in zip(a_refs[1:], w_refs[1:]):
        acc = acc + jnp.dot(a_ref[...], w_ref[...], preferred_element_type=F32)
    o_ref[...] = x_ref[...] + mod_ref[gate_row:gate_row + 1, :] * acc


def _matmul_res(a_list, w_list, x, mod, *, gate_row, tm, tn, name):
    s, d = x.shape
    n_in = len(a_list)
    in_specs = ([pl.BlockSpec((tm, a.shape[1]), lambda i, j: (i, 0)) for a in a_list]
                + [pl.BlockSpec((w.shape[0], tn), lambda i, j: (0, j)) for w in w_list]
                + [pl.BlockSpec((tm, tn), lambda i, j: (i, j)),
                   pl.BlockSpec((N_MOD, tn), lambda i, j: (0, j))])
    return pl.pallas_call(
        functools.partial(_mm_res_kernel, n_in=n_in, gate_row=gate_row),
        grid=(s // tm, d // tn),
        in_specs=in_specs,
        out_specs=pl.BlockSpec((tm, tn), lambda i, j: (i, j)),
        out_shape=jax.ShapeDtypeStruct((s, d), F32),
        compiler_params=_params(("arbitrary", "arbitrary")),
        name=name,
    )(*a_list, *w_list, x, mod)


def _rope(x, cos, sin_lo, sin_hi):
    return (x * cos + pltpu.roll(x, LANES - 32, 1) * sin_lo + pltpu.roll(x, 32, 1) * sin_hi)


def _rms(x, gain):
    return x * lax.rsqrt(jnp.mean(x * x, axis=-1, keepdims=True) + EPS) * gain


def _store_val_t(ref, h, v):
    v_t = v.T.astype(BF16)
    n_blk, tv = ref.shape[1], ref.shape[3]
    for j in range(n_blk):
        ref[h, j] = v_t[:, j * tv:(j + 1) * tv]


def _prep_kernel(qkv_ref, ax_ref, sq_ref, aqn_ref, akn_ref, bqn_ref, bkvn_ref, wuq_ref, wukv_ref,
                 qa_ref, ka_ref, va_ref, qb_ref, kb_ref, vb_ref, qc_ref, kc_ref, vc_ref,
                 qd_ref, kd_ref, vd_ref):
    hd = HEAD_DIM
    ax_cos, ax_lo, ax_hi = ax_ref[0], ax_ref[1], ax_ref[2]
    sq_cos, sq_lo, sq_hi = sq_ref[0], sq_ref[1], sq_ref[2]

    def col(start, width=hd):
        return qkv_ref[:, start:start + width]

    sc_a = HEAD_DIM ** -0.5 * LOG2E
    for h in range(A_HEADS):
        q = _rope(_rms(col(COL_A_Q + h * hd), aqn_ref[...]), ax_cos, ax_lo, ax_hi)
        qa_ref[h * hd:(h + 1) * hd, :] = (q * sc_a).T.astype(BF16)
    for h in range(A_KV_HEADS):
        k = _rope(_rms(col(COL_A_K + h * hd), akn_ref[...]), ax_cos, ax_lo, ax_hi)
        ka_ref[:, h * hd:(h + 1) * hd] = k.astype(BF16)
        _store_val_t(va_ref, h, col(COL_A_V + h * hd))

    sc_b = (B_NOPE + B_ROPE) ** -0.5 * LOG2E
    cq = _rms(col(COL_B_CQ, B_Q_RANK), bqn_ref[...]).astype(BF16)
    qb = jnp.dot(cq, wuq_ref[...], preferred_element_type=F32)
    ckv = _rms(col(COL_B_CKV, B_KV_RANK), bkvn_ref[...]).astype(BF16)
    kvb = jnp.dot(ckv, wukv_ref[...], preferred_element_type=F32)
    kr = _rope(col(COL_B_KR, LANES), sq_cos, sq_lo, sq_hi).astype(BF16)
    for h in range(B_HEADS):
        base = h * B_QK_PAD
        qb_ref[base:base + hd, :] = (qb[:, base:base + hd] * sc_b).T.astype(BF16)
        q_r = _rope(qb[:, base + hd:base + 2 * hd], sq_cos, sq_lo, sq_hi)
        qb_ref[base + hd:base + 2 * hd, :] = (q_r * sc_b).T.astype(BF16)
        kb_ref[:, base:base + hd] = kvb[:, base:base + hd].astype(BF16)
        kb_ref[:, base + hd:base + 2 * hd] = kr
        _store_val_t(vb_ref, h, kvb[:, base + hd:base + 2 * hd])

    sc_c = HEAD_DIM ** -0.5 * LOG2E
    qc_ref[...] = (col(COL_C_Q, C_HEADS * hd) * sc_c).astype(BF16)
    kc_ref[...] = col(COL_C_K, C_KV_HEADS * hd).astype(BF16)
    vc_ref[...] = col(COL_C_V, C_KV_HEADS * hd).astype(BF16)

    sc_d = D_HALF ** -0.5 * LOG2E
    lane = lax.broadcasted_iota(jnp.int32, (qkv_ref.shape[0], hd), 1)
    lo = lane < D_HALF
    for h in range(D_HEADS):
        q = col(COL_D_Q + h * hd) * sc_d
        qd_ref[2 * h * hd:(2 * h + 1) * hd, :] = jnp.where(lo, q, 0.0).T.astype(BF16)
        qd_ref[(2 * h + 1) * hd:(2 * h + 2) * hd, :] = jnp.where(lo, 0.0, q).T.astype(BF16)
        _store_val_t(vd_ref, h, col(COL_D_V + h * hd))
    kd_ref[...] = col(COL_D_K, D_HEADS * hd).astype(BF16)


def _prep(qkv, ax_tab, sq_tab, aqn, akn, bqn, bkvn, wuq, wukv, tm=512):
    s = qkv.shape[0]
    hd = HEAD_DIM
    row = lambda i: (i, 0)
    full = lambda i: (0, 0)

    def row_major(w):
        return pl.BlockSpec((tm, w), row), jax.ShapeDtypeStruct((s, w), BF16)

    def feat_major(w):
        return pl.BlockSpec((w, tm), lambda i: (0, i)), jax.ShapeDtypeStruct((w, s), BF16)

    def val_t(h):
        return (pl.BlockSpec((h, tm // TV, hd, TV), lambda i: (0, i, 0, 0)),
                jax.ShapeDtypeStruct((h, s // TV, hd, TV), BF16))

    outs = [feat_major(A_HEADS * hd), row_major(A_KV_HEADS * hd), val_t(A_KV_HEADS),
            feat_major(B_HEADS * B_QK_PAD), row_major(B_HEADS * B_QK_PAD), val_t(B_HEADS),
            row_major(C_HEADS * hd), row_major(C_KV_HEADS * hd), row_major(C_KV_HEADS * hd),
            feat_major(2 * D_HEADS * hd), row_major(D_HEADS * hd), val_t(D_HEADS)]
    return pl.pallas_call(
        _prep_kernel,
        grid=(s // tm,),
        in_specs=[pl.BlockSpec((tm, IN_COLS_PAD), row),
                  pl.BlockSpec((3, tm, LANES), lambda i: (0, i, 0)),
                  pl.BlockSpec((3, tm, LANES), lambda i: (0, i, 0)),
                  pl.BlockSpec((1, hd), full), pl.BlockSpec((1, hd), full),
                  pl.BlockSpec((1, B_Q_RANK), full), pl.BlockSpec((1, B_KV_RANK), full),
                  pl.BlockSpec(wuq.shape, full), pl.BlockSpec(wukv.shape, full)],
        out_specs=[spec for spec, _ in outs],
        out_shape=[shape for _, shape in outs],
        compiler_params=_params(("arbitrary",)),
        name="prep",
    )(qkv, ax_tab, sq_tab, aqn.reshape(1, hd), akn.reshape(1, hd),
      bqn.reshape(1, B_Q_RANK), bkvn.reshape(1, B_KV_RANK), wuq, wukv)


def _flash_kernel(*refs, groups_stacked, dk, tq, tk, tv, seq, unroll, diff):
    if diff:
        q_ref, k_ref, v_ref, bias_ref, scal_ref, gain_ref, o_ref = refs[:7]
    else:
        q_ref, k_ref, v_ref, o_ref = refs[:4]
    s_scr = refs[-2:]
    r_stack = groups_stacked
    cols = r_stack * tq
    i = pl.program_id(1)
    n_chunks = seq // tk
    n_sub = tk // tv
    q0 = i * tq
    q_t = jnp.concatenate([q_ref[r * dk:(r + 1) * dk, :] for r in range(r_stack)], axis=1)

    def scores(c, slot):
        start = c * tk
        if not isinstance(start, int):
            start = pl.multiple_of(start, tk)
        s = jnp.dot(k_ref[pl.ds(start, tk), :], q_t, preferred_element_type=F32)
        if not diff:
            s_scr[slot][...] = s
            return jnp.max(s, axis=0, keepdims=True)
        n_near = tk // tq + 2
        off = c * tk - q0
        idx = jnp.where(off < -tk, n_near, jnp.where(off > tq, n_near + 1, (off + tk) // tq))
        tile = bias_ref[0, idx]
        m_parts = []
        for r in range(r_stack):
            s_r = s[:, r * tq:(r + 1) * tq] + tile
            s_scr[slot][:, r * tq:(r + 1) * tq] = s_r
            m_parts.append(jnp.max(s_r, axis=0, keepdims=True))
        return jnp.concatenate(m_parts, axis=1)

    def step(c, par, m_chunk, state, do_scores):
        m, l, acc = state
        m_new = jnp.maximum(m, m_chunk)
        alpha = jnp.exp2(m - m_new)
        l = alpha * l
        acc = alpha * acc
        m_next = scores(c + 1, 1 - par) if do_scores else None
        for j in range(n_sub):
            p = jnp.exp2(s_scr[par][j * tv:(j + 1) * tv, :] - m_new)
            l = l + jnp.sum(p, axis=0, keepdims=True)
            acc = acc + jnp.dot(v_ref[0, c * n_sub + j], p.astype(BF16), preferred_element_type=F32)
        return m_next, (m_new, l, acc)

    def group(base, carry, last):
        m_chunk, state = carry
        for u in range(unroll):
            m_chunk, state = step(base + u, u % 2, m_chunk, state, not (last and u == unroll - 1))
        return m_chunk, state

    carry = (scores(0, 0), (jnp.full((1, cols), NEG_BIG, F32), jnp.zeros((1, cols), F32),
                       jnp.zeros((HEAD_DIM, cols), F32)))
    n_groups = n_chunks // unroll
    if n_groups > 1:
        carry = lax.fori_loop(0, n_groups - 1, lambda t, cr: group(t * unroll, cr, False), carry)
    _, (m, l, acc) = group((n_groups - 1) * unroll, carry, True)

    o_t = acc / l
    o = [o_t[:, r * tq:(r + 1) * tq].T for r in range(r_stack)]
    if not diff:
        for r in range(r_stack):
            o_ref[:, r * HEAD_DIM:(r + 1) * HEAD_DIM] = o[r].astype(o_ref.dtype)
        return
    lam = scal_ref[0]
    od = o[0] - lam * o[1]
    y = od * lax.rsqrt(jnp.mean(od * od, axis=-1, keepdims=True) + EPS) * gain_ref[...]
    o_ref[...] = (y * scal_ref[1]).astype(o_ref.dtype)


def _flash(q_t, k, v_t, *, groups, groups_stacked, dk, tq, tk, name, diff_args=None):
    s = k.shape[0]
    n_vblk, tv = v_t.shape[1], v_t.shape[3]
    unroll = min(FLASH_KEYS_PER_TRIP, s) // tk
    assert unroll % 2 == 0 and (s // tk) % unroll == 0 and tk % tv == 0
    diff = diff_args is not None
    kern = functools.partial(_flash_kernel, groups_stacked=groups_stacked, dk=dk, tq=tq, tk=tk, tv=tv,
                             seq=s, unroll=unroll, diff=diff)
    in_specs = [pl.BlockSpec((groups_stacked * dk, tq), lambda g, i: (g, i)),
                pl.BlockSpec((s, dk), lambda g, i: (0, g)),
                pl.BlockSpec((1, n_vblk, HEAD_DIM, tv), lambda g, i: (g, 0, 0, 0))]
    args = [q_t, k, v_t]
    if diff:
        bias, scal, gain = diff_args
        in_specs += [pl.BlockSpec((1,) + bias.shape[1:], lambda g, i: (g, 0, 0, 0)),
                     pl.BlockSpec(memory_space=pltpu.SMEM),
                     pl.BlockSpec((1, HEAD_DIM), lambda g, i: (0, 0))]
        args += [bias, scal, gain.reshape(1, HEAD_DIM)]
        out_w = HEAD_DIM
    else:
        out_w = groups_stacked * HEAD_DIM
    return pl.pallas_call(
        kern,
        grid=(groups, s // tq),
        in_specs=in_specs,
        out_specs=pl.BlockSpec((tq, out_w), lambda g, i: (i, g)),
        out_shape=jax.ShapeDtypeStruct((s, groups * out_w), BF16),
        scratch_shapes=[pltpu.VMEM((tk, groups_stacked * tq), F32)] * 2,
        compiler_params=_params(("arbitrary", "arbitrary")),
        name=name,
    )(*args)


def _window_kernel(q_ref, kp_ref, kc_ref, kn_ref, vp_ref, vc_ref, vn_ref, bias_ref, sink_ref, o_ref,
                   kbuf, vbuf, *, tw, seq):
    w = WINDOW
    h = pl.program_id(0)
    i = pl.program_id(1)
    kbuf[0:w] = kp_ref[tw - w:tw]
    kbuf[w:w + tw] = kc_ref[...]
    kbuf[w + tw:w + tw + w] = kn_ref[0:w]
    vbuf[0:w] = vp_ref[tw - w:tw]
    vbuf[w:w + tw] = vc_ref[...]
    vbuf[w + tw:w + tw + w] = vn_ref[0:w]
    sink = sink_ref[h] * LOG2E
    nq = tw // w
    q3 = q_ref[...].reshape(nq, w, HEAD_DIM)
    k3 = jnp.stack([kbuf[j * w:(j + 3) * w] for j in range(nq)])
    v3 = jnp.stack([vbuf[j * w:(j + 3) * w] for j in range(nq)])
    s = jnp.einsum("jqd,jkd->jqk", q3, k3, preferred_element_type=F32) + bias_ref[...]
    kpos = (i * tw - w + w * lax.broadcasted_iota(jnp.int32, (nq, w, 3 * w), 0)
            + lax.broadcasted_iota(jnp.int32, (nq, w, 3 * w), 2))
    s = jnp.where((kpos >= 0) & (kpos < seq), s, NEG_BIG)
    m = jnp.maximum(jnp.max(s, axis=-1, keepdims=True), sink)
    p = jnp.exp2(s - m)
    denom = jnp.sum(p, axis=-1, keepdims=True) + jnp.exp2(sink - m)
    o = jnp.einsum("jqk,jkd->jqd", p.astype(BF16), v3, preferred_element_type=F32) / denom
    o_ref[...] = o.reshape(tw, HEAD_DIM).astype(o_ref.dtype)


def _window(q, k, v, bias, sink, tw=1024):
    s = q.shape[0]
    hd = HEAD_DIM
    nb = s // tw
    grp = C_HEADS // C_KV_HEADS
    prev = lambda h, i: (jnp.maximum(i - 1, 0), h // grp)
    cur = lambda h, i: (i, h // grp)
    nxt = lambda h, i: (jnp.minimum(i + 1, nb - 1), h // grp)
    kv_specs = [pl.BlockSpec((tw, hd), f) for f in (prev, cur, nxt)]
    return pl.pallas_call(
        functools.partial(_window_kernel, tw=tw, seq=s),
        grid=(C_HEADS, nb),
        in_specs=[pl.BlockSpec((tw, hd), lambda h, i: (i, h))] + kv_specs + kv_specs
                 + [pl.BlockSpec((1, WINDOW, 3 * WINDOW), lambda h, i: (h, 0, 0)),
                    pl.BlockSpec(memory_space=pltpu.SMEM)],
        out_specs=pl.BlockSpec((tw, hd), lambda h, i: (i, h)),
        out_shape=jax.ShapeDtypeStruct((s, C_HEADS * hd), BF16),
        scratch_shapes=[pltpu.VMEM((tw + 2 * WINDOW, hd), BF16),
                        pltpu.VMEM((tw + 2 * WINDOW, hd), BF16)],
        compiler_params=_params(("arbitrary", "arbitrary")),
        name="window_attn",
    )(q, k, k, k, v, v, v, bias, sink)


def _final_norm_kernel(x_ref, g_ref, o_ref):
    o_ref[...] = _rms(x_ref[...], g_ref[...])


def _final_norm(x, gain, tm=512):
    s, d = x.shape
    return pl.pallas_call(
        _final_norm_kernel,
        grid=(s // tm,),
        in_specs=[pl.BlockSpec((tm, d), lambda i: (i, 0)), pl.BlockSpec((1, d), lambda i: (0, 0))],
        out_specs=pl.BlockSpec((tm, d), lambda i: (i, 0)),
        out_shape=jax.ShapeDtypeStruct((s, d), F32),
        compiler_params=_params(("arbitrary",)),
        name="final_norm",
    )(x, gain.reshape(1, d))


def _rope_tables(pos, dim):
    inv = ROPE_THETA ** (-jnp.arange(0, dim, 2, dtype=F32) / dim)
    ang = pos.astype(F32)[:, None] * inv[None, :]
    ang = jnp.concatenate([ang, ang], axis=-1)
    return jnp.cos(ang), jnp.sin(ang)


def _rope_pack(cos, sin, half):
    width = cos.shape[1]
    lane = jnp.arange(width)
    first = (lane % (2 * half)) < half
    tabs = [cos, jnp.where(first, -sin, 0.0), jnp.where(first, 0.0, sin)]
    return jnp.stack([jnp.pad(t, ((0, 0), (0, LANES - width))) for t in tabs])


def _t5_bucket(rel):
    half = N_BUCKETS // 2
    max_exact = half // 2
    n = jnp.abs(rel)
    large = max_exact + (jnp.log(jnp.maximum(n, 1).astype(F32) / max_exact)
                         / math.log(MAX_DISTANCE / max_exact) * (half - max_exact)).astype(jnp.int32)
    large = jnp.minimum(large, half - 1)
    return jnp.where(rel > 0, half, 0) + jnp.where(n < max_exact, n, large)


def _toeplitz(fn, rows, cols, offset):
    n = rows + cols + 1
    vec = fn(jnp.arange(n) - rows + offset)
    flat = jnp.tile(vec, (1, rows))[:, :rows * (n - 1)]
    return flat.reshape(vec.shape[0], rows, n - 1)[:, :, rows:rows + cols]


TQ_A, TQ_B, TQ_D = 256, 512, 256
TK_DENSE = 1024
FLASH_KEYS_PER_TRIP = 4096


def kernel(x, c, w_ada, b_ada, norm1, w_in, a_q_norm, a_k_norm, b_q_norm, b_kv_norm, b_w_uq, b_w_ukv,
           c_sink, d_lambda, d_sub_norm, w_out, norm2, w_ff1, w_ff2, rel_bias, final_norm):
    _, s, d = x.shape
    depth = w_ada.shape[0]
    hd = HEAD_DIM
    x2 = x.reshape(s, d)

    t = jnp.arange(s)
    grid_h = s // GRID_W
    cos_r, sin_r = (jnp.repeat(tab, GRID_W, axis=0) for tab in _rope_tables(jnp.arange(grid_h), hd // 2))
    cos_c, sin_c = (jnp.tile(tab, (grid_h, 1)) for tab in _rope_tables(jnp.arange(GRID_W), hd // 2))
    ax_tab = _rope_pack(jnp.concatenate([cos_r, cos_c], -1), jnp.concatenate([sin_r, sin_c], -1), hd // 4)
    sq_tab = _rope_pack(*_rope_tables(t, B_ROPE), B_ROPE // 2)

    bias_fn = lambda heads: (lambda rel: (rel_bias[_t5_bucket(rel)][:, heads] * LOG2E).T)
    c_heads = jnp.arange(C_HEADS)
    d_heads = C_HEADS + jnp.arange(D_HEADS)
    rel_c = jnp.arange(3 * WINDOW)[None, :] - WINDOW - jnp.arange(WINDOW)[:, None]
    bias_c = jnp.where(jnp.abs(rel_c)[None] <= WINDOW, _toeplitz(bias_fn(c_heads), WINDOW, 3 * WINDOW, -WINDOW),
                       NEG_BIG)
    n_off = TK_DENSE // TQ_D + 2
    bias_d = jnp.stack([_toeplitz(bias_fn(d_heads), TQ_D, TK_DENSE, o * TQ_D - TK_DENSE)
                        for o in range(n_off)], axis=1)
    far_d = bias_fn(d_heads)(jnp.array([-MAX_DISTANCE, MAX_DISTANCE]))
    bias_d = jnp.concatenate(
        [bias_d, jnp.broadcast_to(far_d[:, :, None, None], (D_HEADS, 2, TQ_D, TK_DENSE))], axis=1)
    bias_d = jnp.swapaxes(bias_d, 2, 3)

    mod_all = _ada_mod(c, w_ada, b_ada)

    for l in range(depth):
        mod = mod_all[l]
        lam_init = 0.8 - 0.6 * math.exp(-0.3 * l)
        lf = d_lambda[l].astype(F32)
        lam = jnp.exp(jnp.sum(lf[0] * lf[1])) - jnp.exp(jnp.sum(lf[2] * lf[3])) + lam_init
        scal = jnp.stack([lam, jnp.asarray(1.0 - lam_init, F32)])

        w_in_l = w_in[l]
        w_in_p = jnp.concatenate([w_in_l[:, :COL_B_KR + B_ROPE], jnp.zeros((d, B_KR_PAD), F32),
                                  w_in_l[:, COL_B_KR + B_ROPE:]], axis=1).astype(BF16)
        wuq = jnp.pad(b_w_uq[l].reshape(B_Q_RANK, B_HEADS, B_NOPE + B_ROPE),
                      ((0, 0), (0, 0), (0, B_QK_PAD - B_NOPE - B_ROPE))).reshape(B_Q_RANK, -1).astype(BF16)
        wukv = b_w_ukv[l].astype(BF16)

        qkv = _norm_matmul(x2, norm1[l], mod, w_in_p, shift_row=0, scale_row=1, relu2=False,
                           out_dtype=F32, tm=512, tn=IN_COLS_PAD // 2, name="in_proj")
        (qa, ka, va, qb, kb, vb, qc, kc, vc, qd, kd, vd) = _prep(
            qkv, ax_tab, sq_tab, a_q_norm[l], a_k_norm[l], b_q_norm[l], b_kv_norm[l], wuq, wukv)

        oa = _flash(qa, ka, va, groups=A_KV_HEADS, groups_stacked=A_HEADS // A_KV_HEADS, dk=hd,
                    tq=TQ_A, tk=TK_DENSE, name="flash_a")
        ob = _flash(qb, kb, vb, groups=B_HEADS, groups_stacked=1, dk=B_QK_PAD,
                    tq=TQ_B, tk=TK_DENSE, name="flash_b")
        oc = _window(qc, kc, vc, bias_c, c_sink[l])
        od = _flash(qd, kd, vd, groups=D_HEADS, groups_stacked=2, dk=hd, tq=TQ_D, tk=TK_DENSE,
                    name="flash_d", diff_args=(bias_d, scal, d_sub_norm[l]))

        w_out_l = w_out[l].astype(BF16)
        heads = [oa, ob, oc, od]
        w_parts = [w_out_l[k * 4 * hd:(k + 1) * 4 * hd] for k in range(4)]
        x2 = _matmul_res(heads, w_parts, x2, mod, gate_row=2, tm=512, tn=d, name="out_proj")

        hid = _norm_matmul(x2, norm2[l], mod, w_ff1[l].astype(BF16), shift_row=3, scale_row=4, relu2=True,
                           out_dtype=BF16, tm=1024, tn=1024, name="ff1")
        x2 = _matmul_res([hid], [w_ff2[l].astype(BF16)], x2, mod, gate_row=5, tm=512, tn=512, name="ff2")

    return _final_norm(x2, final_norm).reshape(1, s, d)
```

```python
import functools
import math

import jax
import jax.numpy as jnp
from jax import lax
from jax.experimental import pallas as pl
from jax.experimental.pallas import tpu as pltpu

F32 = jnp.float32
BF16 = jnp.bfloat16

HEAD_DIM = 128
GRID_W = 64
ROPE_THETA = 10000.0
EPS = 1e-6
A_HEADS, A_KV_HEADS = 4, 2
B_HEADS, B_NOPE, B_ROPE, B_Q_RANK, B_KV_RANK = 4, 128, 64, 384, 256
C_HEADS, C_KV_HEADS, WINDOW = 4, 2, 128
D_HEADS = 4
D_HALF = HEAD_DIM // 2
N_BUCKETS, MAX_DISTANCE = 32, 128
N_MOD = 6

LOG2E = math.log2(math.e)
NEG_BIG = -1e30
LAZY_MAX_JUMP = 60.0

LANES = 128
MXU_DIM = 256
VMEM_LIMIT = 56 * 1024 * 1024

TV = MXU_DIM

B_KR_PAD = LANES - B_ROPE
COL_A_Q = 0
COL_A_K = COL_A_Q + A_HEADS * HEAD_DIM
COL_A_V = COL_A_K + A_KV_HEADS * HEAD_DIM
COL_B_CQ = COL_A_V + A_KV_HEADS * HEAD_DIM
COL_B_CKV = COL_B_CQ + B_Q_RANK
COL_B_KR = COL_B_CKV + B_KV_RANK
COL_C_Q = COL_B_KR + B_ROPE + B_KR_PAD
COL_C_K = COL_C_Q + C_HEADS * HEAD_DIM
COL_C_V = COL_C_K + C_KV_HEADS * HEAD_DIM
COL_D_Q = COL_C_V + C_KV_HEADS * HEAD_DIM
COL_D_K = COL_D_Q + D_HEADS * HEAD_DIM
COL_D_V = COL_D_K + D_HEADS * HEAD_DIM
IN_COLS_PAD = COL_D_V + D_HEADS * HEAD_DIM
B_QK_PAD = 2 * LANES


def _params(sem):
    return pltpu.CompilerParams(dimension_semantics=sem, vmem_limit_bytes=VMEM_LIMIT)


def _ada_kernel(c_ref, w_ref, b_ref, o_ref):
    c = c_ref[...]
    cs = c * (1.0 / (1.0 + jnp.exp(-c)))
    cs8 = jnp.broadcast_to(cs, (8, cs.shape[1])).astype(BF16)
    r = jnp.dot(cs8, w_ref[0].astype(BF16), preferred_element_type=F32)
    o_ref[0] = r[0:1] + b_ref[0]


def _ada_mod(c, w_ada, b_ada, tn=1024):
    depth, d, n = w_ada.shape
    out = pl.pallas_call(
        _ada_kernel,
        grid=(depth, n // tn),
        in_specs=[pl.BlockSpec((1, d), lambda l, j: (0, 0)),
                  pl.BlockSpec((1, d, tn), lambda l, j: (l, 0, j)),
                  pl.BlockSpec((1, 1, tn), lambda l, j: (l, 0, j))],
        out_specs=pl.BlockSpec((1, 1, tn), lambda l, j: (l, 0, j)),
        out_shape=jax.ShapeDtypeStruct((depth, 1, n), F32),
        compiler_params=_params(("arbitrary", "arbitrary")),
        name="ada_mod",
    )(c, w_ada, b_ada.reshape(depth, 1, n))
    return out.reshape(depth, N_MOD, d)


def _norm_mm_kernel(x_ref, g_ref, mod_ref, w_ref, o_ref, h_ref, *, shift_row, scale_row, relu2):
    @pl.when(pl.program_id(1) == 0)
    def _():
        x = x_ref[...]
        y = x * lax.rsqrt(jnp.mean(x * x, axis=-1, keepdims=True) + EPS) * g_ref[...]
        h = y * (1.0 + mod_ref[scale_row:scale_row + 1, :]) + mod_ref[shift_row:shift_row + 1, :]
        h_ref[...] = h.astype(BF16)

    acc = jnp.dot(h_ref[...], w_ref[...], preferred_element_type=F32)
    if relu2:
        acc = jnp.square(jnp.maximum(acc, 0.0))
    o_ref[...] = acc.astype(o_ref.dtype)


def _norm_matmul(x, gain, mod, w, *, shift_row, scale_row, relu2, out_dtype, tm, tn, name):
    s, d = x.shape
    n = w.shape[1]
    return pl.pallas_call(
        functools.partial(_norm_mm_kernel, shift_row=shift_row, scale_row=scale_row, relu2=relu2),
        grid=(s // tm, n // tn),
        in_specs=[pl.BlockSpec((tm, d), lambda i, j: (i, 0)),
                  pl.BlockSpec((1, d), lambda i, j: (0, 0)),
                  pl.BlockSpec((N_MOD, d), lambda i, j: (0, 0)),
                  pl.BlockSpec((d, tn), lambda i, j: (0, j))],
        out_specs=pl.BlockSpec((tm, tn), lambda i, j: (i, j)),
        out_shape=jax.ShapeDtypeStruct((s, n), out_dtype),
        scratch_shapes=[pltpu.VMEM((tm, d), BF16)],
        compiler_params=_params(("arbitrary", "arbitrary")),
        name=name,
    )(x, gain.reshape(1, d), mod, w)


def _mm_res_kernel(*refs, n_in, gate_row):
    a_refs, w_refs = refs[:n_in], refs[n_in:2 * n_in]
    x_ref, mod_ref, o_ref = refs[2 * n_in:]
    acc = jnp.dot(a_refs[0][...], w_refs[0][...], preferred_element_type=F32)
    for a_ref, w_ref in zip(a_refs[1:], w_refs[1:]):
        acc = acc + jnp.dot(a_ref[...], w_ref[...], preferred_element_type=F32)
    o_ref[...] = x_ref[...] + mod_ref[gate_row:gate_row + 1, :] * acc


def _matmul_res(a_list, w_list, x, mod, *, gate_row, tm, tn, name):
    s, d = x.shape
    n_in = len(a_list)
    in_specs = ([pl.BlockSpec((tm, a.shape[1]), lambda i, j: (i, 0)) for a in a_list]
                + [pl.BlockSpec((w.shape[0], tn), lambda i, j: (0, j)) for w in w_list]
                + [pl.BlockSpec((tm, tn), lambda i, j: (i, j)),
                   pl.BlockSpec((N_MOD, tn), lambda i, j: (0, j))])
    return pl.pallas_call(
        functools.partial(_mm_res_kernel, n_in=n_in, gate_row=gate_row),
        grid=(s // tm, d // tn),
        in_specs=in_specs,
        out_specs=pl.BlockSpec((tm, tn), lambda i, j: (i, j)),
        out_shape=jax.ShapeDtypeStruct((s, d), F32),
        compiler_params=_params(("arbitrary", "arbitrary")),
        name=name,
    )(*a_list, *w_list, x, mod)


def _rope(x, cos, sin_lo, sin_hi):
    return (x * cos + pltpu.roll(x, LANES - 32, 1) * sin_lo + pltpu.roll(x, 32, 1) * sin_hi)


def _rms(x, gain):
    return x * lax.rsqrt(jnp.mean(x * x, axis=-1, keepdims=True) + EPS) * gain


def _store_val_t(ref, h, v):
    v_t = v.T.astype(BF16)
    n_blk, tv = ref.shape[1], ref.shape[3]
    for j in range(n_blk):
        ref[h, j] = v_t[:, j * tv:(j + 1) * tv]


def _prep_kernel(qkv_ref, ax_ref, sq_ref, aqn_ref, akn_ref, bqn_ref, bkvn_ref, wuq_ref, wukv_ref,
                 qa_ref, ka_ref, va_ref, qb_ref, kb_ref, vb_ref, qc_ref, kc_ref, vc_ref,
                 qd_ref, kd_ref, vd_ref):
    hd = HEAD_DIM
    ax_cos, ax_lo, ax_hi = ax_ref[0], ax_ref[1], ax_ref[2]
    sq_cos, sq_lo, sq_hi = sq_ref[0], sq_ref[1], sq_ref[2]

    def col(start, width=hd):
        return qkv_ref[:, start:start + width]

    sc_a = HEAD_DIM ** -0.5 * LOG2E
    for h in range(A_HEADS):
        q = _rope(_rms(col(COL_A_Q + h * hd), aqn_ref[...]), ax_cos, ax_lo, ax_hi)
        qa_ref[h * hd:(h + 1) * hd, :] = (q * sc_a).T.astype(BF16)
    for h in range(A_KV_HEADS):
        k = _rope(_rms(col(COL_A_K + h * hd), akn_ref[...]), ax_cos, ax_lo, ax_hi)
        ka_ref[:, h * hd:(h + 1) * hd] = k.astype(BF16)
        _store_val_t(va_ref, h, col(COL_A_V + h * hd))

    sc_b = (B_NOPE + B_ROPE) ** -0.5 * LOG2E
    cq = _rms(col(COL_B_CQ, B_Q_RANK), bqn_ref[...]).astype(BF16)
    qb = jnp.dot(cq, wuq_ref[...], preferred_element_type=F32)
    ckv = _rms(col(COL_B_CKV, B_KV_RANK), bkvn_ref[...]).astype(BF16)
    kvb = jnp.dot(ckv, wukv_ref[...], preferred_element_type=F32)
    kr = _rope(col(COL_B_KR, LANES), sq_cos, sq_lo, sq_hi).astype(BF16)
    for h in range(B_HEADS):
        base = h * B_QK_PAD
        qb_ref[base:base + hd, :] = (qb[:, base:base + hd] * sc_b).T.astype(BF16)
        q_r = _rope(qb[:, base + hd:base + 2 * hd], sq_cos, sq_lo, sq_hi)
        qb_ref[base + hd:base + 2 * hd, :] = (q_r * sc_b).T.astype(BF16)
        kb_ref[:, base:base + hd] = kvb[:, base:base + hd].astype(BF16)
        kb_ref[:, base + hd:base + 2 * hd] = kr
        _store_val_t(vb_ref, h, kvb[:, base + hd:base + 2 * hd])

    sc_c = HEAD_DIM ** -0.5 * LOG2E
    qc_ref[...] = (col(COL_C_Q, C_HEADS * hd) * sc_c).astype(BF16)
    kc_ref[...] = col(COL_C_K, C_KV_HEADS * hd).astype(BF16)
    vc_ref[...] = col(COL_C_V, C_KV_HEADS * hd).astype(BF16)

    sc_d = D_HALF ** -0.5 * LOG2E
    lane = lax.broadcasted_iota(jnp.int32, (qkv_ref.shape[0], hd), 1)
    lo = lane < D_HALF
    for h in range(D_HEADS):
        q = col(COL_D_Q + h * hd) * sc_d
        qd_ref[2 * h * hd:(2 * h + 1) * hd, :] = jnp.where(lo, q, 0.0).T.astype(BF16)
        qd_ref[(2 * h + 1) * hd:(2 * h + 2) * hd, :] = jnp.where(lo, 0.0, q).T.astype(BF16)
        _store_val_t(vd_ref, h, col(COL_D_V + h * hd))
    kd_ref[...] = col(COL_D_K, D_HEADS * hd).astype(BF16)


def _prep(qkv, ax_tab, sq_tab, aqn, akn, bqn, bkvn, wuq, wukv, tm=512):
    s = qkv.shape[0]
    hd = HEAD_DIM
    row = lambda i: (i, 0)
    full = lambda i: (0, 0)

    def row_major(w):
        return pl.BlockSpec((tm, w), row), jax.ShapeDtypeStruct((s, w), BF16)

    def feat_major(w):
        return pl.BlockSpec((w, tm), lambda i: (0, i)), jax.ShapeDtypeStruct((w, s), BF16)

    def val_t(h):
        return (pl.BlockSpec((h, tm // TV, hd, TV), lambda i: (0, i, 0, 0)),
                jax.ShapeDtypeStruct((h, s // TV, hd, TV), BF16))

    outs = [feat_major(A_HEADS * hd), row_major(A_KV_HEADS * hd), val_t(A_KV_HEADS),
            feat_major(B_HEADS * B_QK_PAD), row_major(B_HEADS * B_QK_PAD), val_t(B_HEADS),
            row_major(C_HEADS * hd), row_major(C_KV_HEADS * hd), row_major(C_KV_HEADS * hd),
            feat_major(2 * D_HEADS * hd), row_major(D_HEADS * hd), val_t(D_HEADS)]
    return pl.pallas_call(
        _prep_kernel,
        grid=(s // tm,),
        in_specs=[pl.BlockSpec((tm, IN_COLS_PAD), row),
                  pl.BlockSpec((3, tm, LANES), lambda i: (0, i, 0)),
                  pl.BlockSpec((3, tm, LANES), lambda i: (0, i, 0)),
                  pl.BlockSpec((1, hd), full), pl.BlockSpec((1, hd), full),
                  pl.BlockSpec((1, B_Q_RANK), full), pl.BlockSpec((1, B_KV_RANK), full),
                  pl.BlockSpec(wuq.shape, full), pl.BlockSpec(wukv.shape, full)],
        out_specs=[spec for spec, _ in outs],
        out_shape=[shape for _, shape in outs],
        compiler_params=_params(("arbitrary",)),
        name="prep",
    )(qkv, ax_tab, sq_tab, aqn.reshape(1, hd), akn.reshape(1, hd),
      bqn.reshape(1, B_Q_RANK), bkvn.reshape(1, B_KV_RANK), wuq, wukv)


def _flash_kernel(*refs, groups_stacked, dk, tq, tk, tv, seq, unroll, diff):
    if diff:
        q_ref, k_ref, v_ref, bias_ref, scal_ref, gain_ref, o_ref = refs[:7]
    else:
        q_ref, k_ref, v_ref, o_ref = refs[:4]
    s_scr, acc_scr, p_scr = refs[-5:-3], refs[-3], refs[-2:]
    r_stack = groups_stacked
    cols = r_stack * tq
    i = pl.program_id(1)
    n_chunks = seq // tk
    n_sub = tk // tv
    q0 = i * tq
    q_t = jnp.concatenate([q_ref[r * dk:(r + 1) * dk, :] for r in range(r_stack)], axis=1)

    def raw_scores(c):
        start = c * tk
        if not isinstance(start, int):
            start = pl.multiple_of(start, tk)
        s = jnp.dot(k_ref[pl.ds(start, tk), :], q_t, preferred_element_type=F32)
        if not diff:
            return [(slice(0, cols), s)]
        n_near = tk // tq + 2
        off = c * tk - q0
        idx = jnp.where(off < -tk, n_near, jnp.where(off > tq, n_near + 1, (off + tk) // tq))
        tile = bias_ref[0, idx]
        return [(slice(r * tq, (r + 1) * tq), s[:, r * tq:(r + 1) * tq] + tile) for r in range(r_stack)]

    def finalize(acc_t, l):
        o_t = acc_t / l
        o = [o_t[:, r * tq:(r + 1) * tq].T for r in range(r_stack)]
        if not diff:
            for r in range(r_stack):
                o_ref[:, r * HEAD_DIM:(r + 1) * HEAD_DIM] = o[r].astype(o_ref.dtype)
            return
        lam = scal_ref[0]
        od = o[0] - lam * o[1]
        y = od * lax.rsqrt(jnp.mean(od * od, axis=-1, keepdims=True) + EPS) * gain_ref[...]
        o_ref[...] = (y * scal_ref[1]).astype(o_ref.dtype)

    def values(c, p_ref, acc):
        for j in range(n_sub):
            acc = acc + jnp.dot(v_ref[0, c * n_sub + j], p_ref[j * tv:(j + 1) * tv, :], preferred_element_type=F32)
        return acc

    n_groups = n_chunks // unroll

    def scores_probs(c, slot, m_ref_row):
        m_parts, l_parts = [], []
        for cs, s_piece in raw_scores(c):
            e = jnp.exp2(s_piece - m_ref_row[:, cs])
            p_scr[slot][:, cs] = e.astype(BF16)
            m_parts.append(jnp.max(s_piece, axis=0, keepdims=True))
            l_parts.append(jnp.sum(e, axis=0, keepdims=True))
        return jnp.concatenate(m_parts, axis=1), jnp.concatenate(l_parts, axis=1)

    def fast_step(c, par, state, do_scores):
        m_run, beta, l, jump = state
        if do_scores:
            m_chunk, l_chunk = scores_probs(c + 1, 1 - par, m_run)
        acc_scr[...] = beta * values(c, p_scr[par], acc_scr[...])
        if not do_scores:
            return state
        m_new = jnp.maximum(m_run, m_chunk)
        beta_new = jnp.exp2(m_run - m_new)
        return m_new, beta_new, beta_new * (l + l_chunk), jnp.maximum(jump, m_chunk - m_run)

    def fast_group(base, state, last):
        for u in range(unroll):
            state = fast_step(base + u, u % 2, state, not (last and u == unroll - 1))
        return state

    pieces = raw_scores(0)
    m_first = jnp.concatenate([jnp.max(sp, axis=0, keepdims=True) for _, sp in pieces], axis=1)
    for cs, s_piece in pieces:
        s_scr[0][:, cs] = s_piece
    e_first = jnp.exp2(s_scr[0][...] - m_first)
    p_scr[0][...] = e_first.astype(BF16)
    acc_scr[...] = jnp.zeros((HEAD_DIM, cols), F32)
    state = (m_first, jnp.ones((1, cols), F32), jnp.sum(e_first, axis=0, keepdims=True), jnp.zeros((1, cols), F32))
    if n_groups > 1:
        state = lax.fori_loop(0, n_groups - 1, lambda t, st: fast_group(t * unroll, st, False), state)
    _, _, l_fast, jump = fast_group((n_groups - 1) * unroll, state, True)
    finalize(acc_scr[...], l_fast)

    def scores(c, slot):
        m_parts = []
        for cs, s_piece in raw_scores(c):
            s_scr[slot][:, cs] = s_piece
            m_parts.append(jnp.max(s_piece, axis=0, keepdims=True))
        return jnp.concatenate(m_parts, axis=1)

    def step(c, par, m_chunk, state, do_scores):
        m, l = state
        m_new = jnp.maximum(m, m_chunk)
        alpha = jnp.exp2(m - m_new)
        l = alpha * l
        m_next = scores(c + 1, 1 - par) if do_scores else None
        acc = alpha * acc_scr[...]
        for j in range(n_sub):
            p = jnp.exp2(s_scr[par][j * tv:(j + 1) * tv, :] - m_new)
            l = l + jnp.sum(p, axis=0, keepdims=True)
            acc = acc + jnp.dot(v_ref[0, c * n_sub + j], p.astype(BF16), preferred_element_type=F32)
        acc_scr[...] = acc
        return m_next, (m_new, l)

    def group(base, carry, last):
        m_chunk, state = carry
        for u in range(unroll):
            m_chunk, state = step(base + u, u % 2, m_chunk, state, not (last and u == unroll - 1))
        return m_chunk, state

    @pl.when(jnp.max(jump) > LAZY_MAX_JUMP)
    def _():
        acc_scr[...] = jnp.zeros((HEAD_DIM, cols), F32)
        carry = (scores(0, 0), (jnp.full((1, cols), NEG_BIG, F32), jnp.zeros((1, cols), F32)))
        if n_groups > 1:
            carry = lax.fori_loop(0, n_groups - 1, lambda t, cr: group(t * unroll, cr, False), carry)
        _, (_, l_exact) = group((n_groups - 1) * unroll, carry, True)
        finalize(acc_scr[...], l_exact)


def _flash(q_t, k, v_t, *, groups, groups_stacked, dk, tq, tk, name, diff_args=None):
    s = k.shape[0]
    n_vblk, tv = v_t.shape[1], v_t.shape[3]
    unroll = min(FLASH_KEYS_PER_TRIP, s) // tk
    assert unroll % 2 == 0 and (s // tk) % unroll == 0 and tk % tv == 0
    diff = diff_args is not None
    kern = functools.partial(_flash_kernel, groups_stacked=groups_stacked, dk=dk, tq=tq, tk=tk, tv=tv,
                             seq=s, unroll=unroll, diff=diff)
    in_specs = [pl.BlockSpec((groups_stacked * dk, tq), lambda g, i: (g, i)),
                pl.BlockSpec((s, dk), lambda g, i: (0, g)),
                pl.BlockSpec((1, n_vblk, HEAD_DIM, tv), lambda g, i: (g, 0, 0, 0))]
    args = [q_t, k, v_t]
    if diff:
        bias, scal, gain = diff_args
        in_specs += [pl.BlockSpec((1,) + bias.shape[1:], lambda g, i: (g, 0, 0, 0)),
                     pl.BlockSpec(memory_space=pltpu.SMEM),
                     pl.BlockSpec((1, HEAD_DIM), lambda g, i: (0, 0))]
        args += [bias, scal, gain.reshape(1, HEAD_DIM)]
        out_w = HEAD_DIM
    else:
        out_w = groups_stacked * HEAD_DIM
    return pl.pallas_call(
        kern,
        grid=(groups, s // tq),
        in_specs=in_specs,
        out_specs=pl.BlockSpec((tq, out_w), lambda g, i: (i, g)),
        out_shape=jax.ShapeDtypeStruct((s, groups * out_w), BF16),
        scratch_shapes=([pltpu.VMEM((tk, groups_stacked * tq), F32)] * 2
                        + [pltpu.VMEM((HEAD_DIM, groups_stacked * tq), F32)]
                        + [pltpu.VMEM((tk, groups_stacked * tq), BF16)] * 2),
        compiler_params=_params(("arbitrary", "arbitrary")),
        name=name,
    )(*args)


def _window_kernel(q_ref, kp_ref, kc_ref, kn_ref, vp_ref, vc_ref, vn_ref, bias_ref, sink_ref, o_ref,
                   kbuf, vbuf, *, tw, seq):
    w = WINDOW
    h = pl.program_id(0)
    i = pl.program_id(1)
    kbuf[0:w] = kp_ref[tw - w:tw]
    kbuf[w:w + tw] = kc_ref[...]
    kbuf[w + tw:w + tw + w] = kn_ref[0:w]
    vbuf[0:w] = vp_ref[tw - w:tw]
    vbuf[w:w + tw] = vc_ref[...]
    vbuf[w + tw:w + tw + w] = vn_ref[0:w]
    sink = sink_ref[h] * LOG2E
    nq = tw // w
    q3 = q_ref[...].reshape(nq, w, HEAD_DIM)
    k3 = jnp.stack([kbuf[j * w:(j + 3) * w] for j in range(nq)])
    v3 = jnp.stack([vbuf[j * w:(j + 3) * w] for j in range(nq)])
    s = jnp.einsum("jqd,jkd->jqk", q3, k3, preferred_element_type=F32) + bias_ref[...]
    kpos = (i * tw - w + w * lax.broadcasted_iota(jnp.int32, (nq, w, 3 * w), 0)
            + lax.broadcasted_iota(jnp.int32, (nq, w, 3 * w), 2))
    s = jnp.where((kpos >= 0) & (kpos < seq), s, NEG_BIG)
    m = jnp.maximum(jnp.max(s, axis=-1, keepdims=True), sink)
    p = jnp.exp2(s - m)
    denom = jnp.sum(p, axis=-1, keepdims=True) + jnp.exp2(sink - m)
    o = jnp.einsum("jqk,jkd->jqd", p.astype(BF16), v3, preferred_element_type=F32) / denom
    o_ref[...] = o.reshape(tw, HEAD_DIM).astype(o_ref.dtype)


def _window(q, k, v, bias, sink, tw=1024):
    s = q.shape[0]
    hd = HEAD_DIM
    nb = s // tw
    grp = C_HEADS // C_KV_HEADS
    prev = lambda h, i: (jnp.maximum(i - 1, 0), h // grp)
    cur = lambda h, i: (i, h // grp)
    nxt = lambda h, i: (jnp.minimum(i + 1, nb - 1), h // grp)
    kv_specs = [pl.BlockSpec((tw, hd), f) for f in (prev, cur, nxt)]
    return pl.pallas_call(
        functools.partial(_window_kernel, tw=tw, seq=s),
        grid=(C_HEADS, nb),
        in_specs=[pl.BlockSpec((tw, hd), lambda h, i: (i, h))] + kv_specs + kv_specs
                 + [pl.BlockSpec((1, WINDOW, 3 * WINDOW), lambda h, i: (h, 0, 0)),
                    pl.BlockSpec(memory_space=pltpu.SMEM)],
        out_specs=pl.BlockSpec((tw, hd), lambda h, i: (i, h)),
        out_shape=jax.ShapeDtypeStruct((s, C_HEADS * hd), BF16),
        scratch_shapes=[pltpu.VMEM((tw + 2 * WINDOW, hd), BF16),
                        pltpu.VMEM((tw + 2 * WINDOW, hd), BF16)],
        compiler_params=_params(("arbitrary", "arbitrary")),
        name="window_attn",
    )(q, k, k, k, v, v, v, bias, sink)


def _final_norm_kernel(x_ref, g_ref, o_ref):
    o_ref[...] = _rms(x_ref[...], g_ref[...])


def _final_norm(x, gain, tm=512):
    s, d = x.shape
    return pl.pallas_call(
        _final_norm_kernel,
        grid=(s // tm,),
        in_specs=[pl.BlockSpec((tm, d), lambda i: (i, 0)), pl.BlockSpec((1, d), lambda i: (0, 0))],
        out_specs=pl.BlockSpec((tm, d), lambda i: (i, 0)),
        out_shape=jax.ShapeDtypeStruct((s, d), F32),
        compiler_params=_params(("arbitrary",)),
        name="final_norm",
    )(x, gain.reshape(1, d))


def _rope_tables(pos, dim):
    inv = ROPE_THETA ** (-jnp.arange(0, dim, 2, dtype=F32) / dim)
    ang = pos.astype(F32)[:, None] * inv[None, :]
    ang = jnp.concatenate([ang, ang], axis=-1)
    return jnp.cos(ang), jnp.sin(ang)


def _rope_pack(cos, sin, half):
    width = cos.shape[1]
    lane = jnp.arange(width)
    first = (lane % (2 * half)) < half
    tabs = [cos, jnp.where(first, -sin, 0.0), jnp.where(first, 0.0, sin)]
    return jnp.stack([jnp.pad(t, ((0, 0), (0, LANES - width))) for t in tabs])


def _t5_bucket(rel):
    half = N_BUCKETS // 2
    max_exact = half // 2
    n = jnp.abs(rel)
    large = max_exact + (jnp.log(jnp.maximum(n, 1).astype(F32) / max_exact)
                         / math.log(MAX_DISTANCE / max_exact) * (half - max_exact)).astype(jnp.int32)
    large = jnp.minimum(large, half - 1)
    return jnp.where(rel > 0, half, 0) + jnp.where(n < max_exact, n, large)


def _toeplitz(fn, rows, cols, offset):
    n = rows + cols + 1
    vec = fn(jnp.arange(n) - rows + offset)
    flat = jnp.tile(vec, (1, rows))[:, :rows * (n - 1)]
    return flat.reshape(vec.shape[0], rows, n - 1)[:, :, rows:rows + cols]


TQ_A, TQ_B, TQ_D = 256, 512, 256
TK_DENSE = 1024
FLASH_KEYS_PER_TRIP = 4096


def kernel(x, c, w_ada, b_ada, norm1, w_in, a_q_norm, a_k_norm, b_q_norm, b_kv_norm, b_w_uq, b_w_ukv,
           c_sink, d_lambda, d_sub_norm, w_out, norm2, w_ff1, w_ff2, rel_bias, final_norm):
    _, s, d = x.shape
    depth = w_ada.shape[0]
    hd = HEAD_DIM
    x2 = x.reshape(s, d)

    t = jnp.arange(s)
    grid_h = s // GRID_W
    cos_r, sin_r = (jnp.repeat(tab, GRID_W, axis=0) for tab in _rope_tables(jnp.arange(grid_h), hd // 2))
    cos_c, sin_c = (jnp.tile(tab, (grid_h, 1)) for tab in _rope_tables(jnp.arange(GRID_W), hd // 2))
    ax_tab = _rope_pack(jnp.concatenate([cos_r, cos_c], -1), jnp.concatenate([sin_r, sin_c], -1), hd // 4)
    sq_tab = _rope_pack(*_rope_tables(t, B_ROPE), B_ROPE // 2)

    bias_fn = lambda heads: (lambda rel: (rel_bias[_t5_bucket(rel)][:, heads] * LOG2E).T)
    c_heads = jnp.arange(C_HEADS)
    d_heads = C_HEADS + jnp.arange(D_HEADS)
    rel_c = jnp.arange(3 * WINDOW)[None, :] - WINDOW - jnp.arange(WINDOW)[:, None]
    bias_c = jnp.where(jnp.abs(rel_c)[None] <= WINDOW, _toeplitz(bias_fn(c_heads), WINDOW, 3 * WINDOW, -WINDOW),
                       NEG_BIG)
    n_off = TK_DENSE // TQ_D + 2
    bias_d = jnp.stack([_toeplitz(bias_fn(d_heads), TQ_D, TK_DENSE, o * TQ_D - TK_DENSE)
                        for o in range(n_off)], axis=1)
    far_d = bias_fn(d_heads)(jnp.array([-MAX_DISTANCE, MAX_DISTANCE]))
    bias_d = jnp.concatenate(
        [bias_d, jnp.broadcast_to(far_d[:, :, None, None], (D_HEADS, 2, TQ_D, TK_DENSE))], axis=1)
    bias_d = jnp.swapaxes(bias_d, 2, 3)

    mod_all = _ada_mod(c, w_ada, b_ada)

    for l in range(depth):
        mod = mod_all[l]
        lam_init = 0.8 - 0.6 * math.exp(-0.3 * l)
        lf = d_lambda[l].astype(F32)
        lam = jnp.exp(jnp.sum(lf[0] * lf[1])) - jnp.exp(jnp.sum(lf[2] * lf[3])) + lam_init
        scal = jnp.stack([lam, jnp.asarray(1.0 - lam_init, F32)])

        w_in_l = w_in[l]
        w_in_p = jnp.concatenate([w_in_l[:, :COL_B_KR + B_ROPE], jnp.zeros((d, B_KR_PAD), F32),
                                  w_in_l[:, COL_B_KR + B_ROPE:]], axis=1).astype(BF16)
        wuq = jnp.pad(b_w_uq[l].reshape(B_Q_RANK, B_HEADS, B_NOPE + B_ROPE),
                      ((0, 0), (0, 0), (0, B_QK_PAD - B_NOPE - B_ROPE))).reshape(B_Q_RANK, -1).astype(BF16)
        wukv = b_w_ukv[l].astype(BF16)

        qkv = _norm_matmul(x2, norm1[l], mod, w_in_p, shift_row=0, scale_row=1, relu2=False,
                           out_dtype=F32, tm=512, tn=IN_COLS_PAD // 2, name="in_proj")
        (qa, ka, va, qb, kb, vb, qc, kc, vc, qd, kd, vd) = _prep(
            qkv, ax_tab, sq_tab, a_q_norm[l], a_k_norm[l], b_q_norm[l], b_kv_norm[l], wuq, wukv)

        oa = _flash(qa, ka, va, groups=A_KV_HEADS, groups_stacked=A_HEADS // A_KV_HEADS, dk=hd,
                    tq=TQ_A, tk=TK_DENSE, name="flash_a")
        ob = _flash(qb, kb, vb, groups=B_HEADS, groups_stacked=1, dk=B_QK_PAD,
                    tq=TQ_B, tk=TK_DENSE, name="flash_b")
        oc = _window(qc, kc, vc, bias_c, c_sink[l])
        od = _flash(qd, kd, vd, groups=D_HEADS, groups_stacked=2, dk=hd, tq=TQ_D, tk=TK_DENSE,
                    name="flash_d", diff_args=(bias_d, scal, d_sub_norm[l]))

        w_out_l = w_out[l].astype(BF16)
        heads = [oa, ob, oc, od]
        w_parts = [w_out_l[k * 4 * hd:(k + 1) * 4 * hd] for k in range(4)]
        x2 = _matmul_res(heads, w_parts, x2, mod, gate_row=2, tm=512, tn=d, name="out_proj")

        hid = _norm_matmul(x2, norm2[l], mod, w_ff1[l].astype(BF16), shift_row=3, scale_row=4, relu2=True,
                           out_dtype=BF16, tm=1024, tn=1024, name="ff1")
        x2 = _matmul_res([hid], [w_ff2[l].astype(BF16)], x2, mod, gate_row=5, tm=512, tn=512, name="ff2")

    return _final_norm(x2, final_norm).reshape(1, s, d)
```

```python
import functools
import math

import jax
import jax.numpy as jnp
from jax import lax
from jax.experimental import pallas as pl
from jax.experimental.pallas import tpu as pltpu

F32 = jnp.float32
BF16 = jnp.bfloat16

HEAD_DIM = 128
GRID_W = 64
ROPE_THETA = 10000.0
EPS = 1e-6
A_HEADS, A_KV_HEADS = 4, 2
B_HEADS, B_NOPE, B_ROPE, B_Q_RANK, B_KV_RANK = 4, 128, 64, 384, 256
C_HEADS, C_KV_HEADS, WINDOW = 4, 2, 128
D_HEADS = 4
D_HALF = HEAD_DIM // 2
N_BUCKETS, MAX_DISTANCE = 32, 128
N_MOD = 6

LOG2E = math.log2(math.e)
NEG_BIG = -1e30
LAZY_MAX_JUMP = 60.0
N_TILED = 3

LANES = 128
MXU_DIM = 256
VMEM_LIMIT = 56 * 1024 * 1024

TV = MXU_DIM

B_KR_PAD = LANES - B_ROPE
COL_A_Q = 0
COL_A_K = COL_A_Q + A_HEADS * HEAD_DIM
COL_A_V = COL_A_K + A_KV_HEADS * HEAD_DIM
COL_B_CQ = COL_A_V + A_KV_HEADS * HEAD_DIM
COL_B_CKV = COL_B_CQ + B_Q_RANK
COL_B_KR = COL_B_CKV + B_KV_RANK
COL_C_Q = COL_B_KR + B_ROPE + B_KR_PAD
COL_C_K = COL_C_Q + C_HEADS * HEAD_DIM
COL_C_V = COL_C_K + C_KV_HEADS * HEAD_DIM
COL_D_Q = COL_C_V + C_KV_HEADS * HEAD_DIM
COL_D_K = COL_D_Q + D_HEADS * HEAD_DIM
COL_D_V = COL_D_K + D_HEADS * HEAD_DIM
IN_COLS_PAD = COL_D_V + D_HEADS * HEAD_DIM
B_QK_PAD = 2 * LANES


def _params(sem):
    return pltpu.CompilerParams(dimension_semantics=sem, vmem_limit_bytes=VMEM_LIMIT)


def _ada_kernel(c_ref, w_ref, b_ref, o_ref):
    c = c_ref[...]
    cs = c * (1.0 / (1.0 + jnp.exp(-c)))
    cs8 = jnp.broadcast_to(cs, (8, cs.shape[1])).astype(BF16)
    r = jnp.dot(cs8, w_ref[0].astype(BF16), preferred_element_type=F32)
    o_ref[0] = r[0:1] + b_ref[0]


def _ada_mod(c, w_ada, b_ada, tn=1024):
    depth, d, n = w_ada.shape
    out = pl.pallas_call(
        _ada_kernel,
        grid=(depth, n // tn),
        in_specs=[pl.BlockSpec((1, d), lambda l, j: (0, 0)),
                  pl.BlockSpec((1, d, tn), lambda l, j: (l, 0, j)),
                  pl.BlockSpec((1, 1, tn), lambda l, j: (l, 0, j))],
        out_specs=pl.BlockSpec((1, 1, tn), lambda l, j: (l, 0, j)),
        out_shape=jax.ShapeDtypeStruct((depth, 1, n), F32),
        compiler_params=_params(("arbitrary", "arbitrary")),
        name="ada_mod",
    )(c, w_ada, b_ada.reshape(depth, 1, n))
    return out.reshape(depth, N_MOD, d)


def _norm_mm_kernel(x_ref, g_ref, mod_ref, w_ref, o_ref, h_ref, *, shift_row, scale_row, relu2):
    @pl.when(pl.program_id(1) == 0)
    def _():
        x = x_ref[...]
        y = x * lax.rsqrt(jnp.mean(x * x, axis=-1, keepdims=True) + EPS) * g_ref[...]
        h = y * (1.0 + mod_ref[scale_row:scale_row + 1, :]) + mod_ref[shift_row:shift_row + 1, :]
        h_ref[...] = h.astype(BF16)

    acc = jnp.dot(h_ref[...], w_ref[...], preferred_element_type=F32)
    if relu2:
        acc = jnp.square(jnp.maximum(acc, 0.0))
    o_ref[...] = acc.astype(o_ref.dtype)


def _norm_matmul(x, gain, mod, w, *, shift_row, scale_row, relu2, out_dtype, tm, tn, name):
    s, d = x.shape
    n = w.shape[1]
    return pl.pallas_call(
        functools.partial(_norm_mm_kernel, shift_row=shift_row, scale_row=scale_row, relu2=relu2),
        grid=(s // tm, n // tn),
        in_specs=[pl.BlockSpec((tm, d), lambda i, j: (i, 0)),
                  pl.BlockSpec((1, d), lambda i, j: (0, 0)),
                  pl.BlockSpec((N_MOD, d), lambda i, j: (0, 0)),
                  pl.BlockSpec((d, tn), lambda i, j: (0, j))],
        out_specs=pl.BlockSpec((tm, tn), lambda i, j: (i, j)),
        out_shape=jax.ShapeDtypeStruct((s, n), out_dtype),
        scratch_shapes=[pltpu.VMEM((tm, d), BF16)],
        compiler_params=_params(("arbitrary", "arbitrary")),
        name=name,
    )(x, gain.reshape(1, d), mod, w)


def _mm_res_kernel(*refs, n_in, gate_row):
    a_refs, w_refs = refs[:n_in], refs[n_in:2 * n_in]
    x_ref, mod_ref, o_ref = refs[2 * n_in:]
    acc = jnp.dot(a_refs[0][...], w_refs[0][...], preferred_element_type=F32)
    for a_ref, w_ref in zip(a_refs[1:], w_refs[1:]):
        acc = acc + jnp.dot(a_ref[...], w_ref[...], preferred_element_type=F32)
    o_ref[...] = x_ref[...] + mod_ref[gate_row:gate_row + 1, :] * acc


def _matmul_res(a_list, w_list, x, mod, *, gate_row, tm, tn, name):
    s, d = x.shape
    n_in = len(a_list)
    in_specs = ([pl.BlockSpec((tm, a.shape[1]), lambda i, j: (i, 0)) for a in a_list]
                + [pl.BlockSpec((w.shape[0], tn), lambda i, j: (0, j)) for w in w_list]
                + [pl.BlockSpec((tm, tn), lambda i, j: (i, j)),
                   pl.BlockSpec((N_MOD, tn), lambda i, j: (0, j))])
    return pl.pallas_call(
        functools.partial(_mm_res_kernel, n_in=n_in, gate_row=gate_row),
        grid=(s // tm, d // tn),
        in_specs=in_specs,
        out_specs=pl.BlockSpec((tm, tn), lambda i, j: (i, j)),
        out_shape=jax.ShapeDtypeStruct((s, d), F32),
        compiler_params=_params(("arbitrary", "arbitrary")),
        name=name,
    )(*a_list, *w_list, x, mod)


def _rope(x, cos, sin_lo, sin_hi):
    return (x * cos + pltpu.roll(x, LANES - 32, 1) * sin_lo + pltpu.roll(x, 32, 1) * sin_hi)


def _rms(x, gain):
    return x * lax.rsqrt(jnp.mean(x * x, axis=-1, keepdims=True) + EPS) * gain


def _store_val_t(ref, h, v):
    v_t = v.T.astype(BF16)
    n_blk, tv = ref.shape[1], ref.shape[3]
    for j in range(n_blk):
        ref[h, j] = v_t[:, j * tv:(j + 1) * tv]


def _prep_kernel(qkv_ref, ax_ref, sq_ref, aqn_ref, akn_ref, bqn_ref, bkvn_ref, wuq_ref, wukv_ref,
                 qa_ref, ka_ref, va_ref, qb_ref, kb_ref, vb_ref, qc_ref, kc_ref, vc_ref,
                 qd_ref, kd_ref, vd_ref):
    hd = HEAD_DIM
    ax_cos, ax_lo, ax_hi = ax_ref[0], ax_ref[1], ax_ref[2]
    sq_cos, sq_lo, sq_hi = sq_ref[0], sq_ref[1], sq_ref[2]

    def col(start, width=hd):
        return qkv_ref[:, start:start + width]

    sc_a = HEAD_DIM ** -0.5 * LOG2E
    for h in range(A_HEADS):
        q = _rope(_rms(col(COL_A_Q + h * hd), aqn_ref[...]), ax_cos, ax_lo, ax_hi)
        qa_ref[h * hd:(h + 1) * hd, :] = (q * sc_a).T.astype(BF16)
    for h in range(A_KV_HEADS):
        k = _rope(_rms(col(COL_A_K + h * hd), akn_ref[...]), ax_cos, ax_lo, ax_hi)
        ka_ref[:, h * hd:(h + 1) * hd] = k.astype(BF16)
        _store_val_t(va_ref, h, col(COL_A_V + h * hd))

    sc_b = (B_NOPE + B_ROPE) ** -0.5 * LOG2E
    cq = _rms(col(COL_B_CQ, B_Q_RANK), bqn_ref[...]).astype(BF16)
    qb = jnp.dot(cq, wuq_ref[...], preferred_element_type=F32)
    ckv = _rms(col(COL_B_CKV, B_KV_RANK), bkvn_ref[...]).astype(BF16)
    kvb = jnp.dot(ckv, wukv_ref[...], preferred_element_type=F32)
    kr = _rope(col(COL_B_KR, LANES), sq_cos, sq_lo, sq_hi).astype(BF16)
    for h in range(B_HEADS):
        base = h * B_QK_PAD
        qb_ref[base:base + hd, :] = (qb[:, base:base + hd] * sc_b).T.astype(BF16)
        q_r = _rope(qb[:, base + hd:base + 2 * hd], sq_cos, sq_lo, sq_hi)
        qb_ref[base + hd:base + 2 * hd, :] = (q_r * sc_b).T.astype(BF16)
        kb_ref[:, base:base + hd] = kvb[:, base:base + hd].astype(BF16)
        kb_ref[:, base + hd:base + 2 * hd] = kr
        _store_val_t(vb_ref, h, kvb[:, base + hd:base + 2 * hd])

    sc_c = HEAD_DIM ** -0.5 * LOG2E
    qc_ref[...] = (col(COL_C_Q, C_HEADS * hd) * sc_c).astype(BF16)
    kc_ref[...] = col(COL_C_K, C_KV_HEADS * hd).astype(BF16)
    vc_ref[...] = col(COL_C_V, C_KV_HEADS * hd).astype(BF16)

    sc_d = D_HALF ** -0.5 * LOG2E
    lane = lax.broadcasted_iota(jnp.int32, (qkv_ref.shape[0], hd), 1)
    lo = lane < D_HALF
    for h in range(D_HEADS):
        q = col(COL_D_Q + h * hd) * sc_d
        qd_ref[2 * h * hd:(2 * h + 1) * hd, :] = jnp.where(lo, q, 0.0).T.astype(BF16)
        qd_ref[(2 * h + 1) * hd:(2 * h + 2) * hd, :] = jnp.where(lo, 0.0, q).T.astype(BF16)
        _store_val_t(vd_ref, h, col(COL_D_V + h * hd))
    kd_ref[...] = col(COL_D_K, D_HEADS * hd).astype(BF16)


def _prep(qkv, ax_tab, sq_tab, aqn, akn, bqn, bkvn, wuq, wukv, tm=512):
    s = qkv.shape[0]
    hd = HEAD_DIM
    row = lambda i: (i, 0)
    full = lambda i: (0, 0)

    def row_major(w):
        return pl.BlockSpec((tm, w), row), jax.ShapeDtypeStruct((s, w), BF16)

    def feat_major(w):
        return pl.BlockSpec((w, tm), lambda i: (0, i)), jax.ShapeDtypeStruct((w, s), BF16)

    def val_t(h):
        return (pl.BlockSpec((h, tm // TV, hd, TV), lambda i: (0, i, 0, 0)),
                jax.ShapeDtypeStruct((h, s // TV, hd, TV), BF16))

    outs = [feat_major(A_HEADS * hd), row_major(A_KV_HEADS * hd), val_t(A_KV_HEADS),
            feat_major(B_HEADS * B_QK_PAD), row_major(B_HEADS * B_QK_PAD), val_t(B_HEADS),
            row_major(C_HEADS * hd), row_major(C_KV_HEADS * hd), row_major(C_KV_HEADS * hd),
            feat_major(2 * D_HEADS * hd), row_major(D_HEADS * hd), val_t(D_HEADS)]
    return pl.pallas_call(
        _prep_kernel,
        grid=(s // tm,),
        in_specs=[pl.BlockSpec((tm, IN_COLS_PAD), row),
                  pl.BlockSpec((3, tm, LANES), lambda i: (0, i, 0)),
                  pl.BlockSpec((3, tm, LANES), lambda i: (0, i, 0)),
                  pl.BlockSpec((1, hd), full), pl.BlockSpec((1, hd), full),
                  pl.BlockSpec((1, B_Q_RANK), full), pl.BlockSpec((1, B_KV_RANK), full),
                  pl.BlockSpec(wuq.shape, full), pl.BlockSpec(wukv.shape, full)],
        out_specs=[spec for spec, _ in outs],
        out_shape=[shape for _, shape in outs],
        compiler_params=_params(("arbitrary",)),
        name="prep",
    )(qkv, ax_tab, sq_tab, aqn.reshape(1, hd), akn.reshape(1, hd),
      bqn.reshape(1, B_Q_RANK), bkvn.reshape(1, B_KV_RANK), wuq, wukv)


def _flash_kernel(*refs, groups_stacked, dk, tq, tk, tv, seq, unroll, diff):
    if diff:
        q_ref, k_ref, v_ref, bias_ref, far_ref, scal_ref, gain_ref, o_ref = refs[:8]
    else:
        q_ref, k_ref, v_ref, o_ref = refs[:4]
    s_scr, acc_scr, p_scr = refs[-5:-3], refs[-3], refs[-2:]
    r_stack = groups_stacked
    cols = r_stack * tq
    i = pl.program_id(1)
    n_chunks = seq // tk
    n_sub = tk // tv
    q0 = i * tq
    q_t = jnp.concatenate([q_ref[r * dk:(r + 1) * dk, :] for r in range(r_stack)], axis=1)

    def raw_scores(c, far=False):
        start = c * tk
        if not isinstance(start, int):
            start = pl.multiple_of(start, tk)
        s = jnp.dot(k_ref[pl.ds(start, tk), :], q_t, preferred_element_type=F32)
        if not diff:
            return [(slice(0, cols), s)], 0.0
        if far:
            g = pl.program_id(0)
            return [(slice(0, cols), s)], jnp.where((c + 1) * tk <= q0, far_ref[g, 0], far_ref[g, 1])
        n_near = tk // tq + 2
        off = c * tk - q0
        idx = jnp.where(off < -tk, n_near, jnp.where(off > tq, n_near + 1, (off + tk) // tq))
        tile = bias_ref[0, idx]
        return [(slice(r * tq, (r + 1) * tq), s[:, r * tq:(r + 1) * tq] + tile) for r in range(r_stack)], 0.0

    def finalize(acc_t, l):
        o_t = acc_t / l
        o = [o_t[:, r * tq:(r + 1) * tq].T for r in range(r_stack)]
        if not diff:
            for r in range(r_stack):
                o_ref[:, r * HEAD_DIM:(r + 1) * HEAD_DIM] = o[r].astype(o_ref.dtype)
            return
        lam = scal_ref[0]
        od = o[0] - lam * o[1]
        y = od * lax.rsqrt(jnp.mean(od * od, axis=-1, keepdims=True) + EPS) * gain_ref[...]
        o_ref[...] = (y * scal_ref[1]).astype(o_ref.dtype)

    def values(c, p_ref, acc):
        for j in range(n_sub):
            acc = acc + jnp.dot(v_ref[0, c * n_sub + j], p_ref[j * tv:(j + 1) * tv, :], preferred_element_type=F32)
        return acc

    n_groups = n_chunks // unroll

    def scores_probs(c, slot, m_ref_row, far):
        pieces, offset = raw_scores(c, far)
        ref_row = m_ref_row - offset
        m_parts, l_parts = [], []
        for cs, s_piece in pieces:
            e = jnp.exp2(s_piece - ref_row[:, cs])
            p_scr[slot][:, cs] = e.astype(BF16)
            m_parts.append(jnp.max(s_piece, axis=0, keepdims=True))
            l_parts.append(jnp.sum(e, axis=0, keepdims=True))
        return jnp.concatenate(m_parts, axis=1) + offset, jnp.concatenate(l_parts, axis=1)

    first_tiled = jnp.clip(q0 // tk - 1, 0, n_chunks - N_TILED)

    def is_far(pos):
        return diff and not (isinstance(pos, int) and pos < N_TILED)

    def chunk_at(pos):
        if not diff:
            return pos
        if not is_far(pos):
            return first_tiled + pos
        j = pos - N_TILED
        return j + jnp.where(j >= first_tiled, N_TILED, 0)

    def fast_step(pos, par, state, do_scores):
        m_run, beta, l, jump = state
        c = chunk_at(pos)
        if do_scores:
            m_chunk, l_chunk = scores_probs(chunk_at(pos + 1), 1 - par, m_run, is_far(pos + 1))
        acc_scr[...] = beta * values(c, p_scr[par], acc_scr[...])
        if not do_scores:
            return state
        m_new = jnp.maximum(m_run, m_chunk)
        beta_new = jnp.exp2(m_run - m_new)
        return m_new, beta_new, beta_new * (l + l_chunk), jnp.maximum(jump, m_chunk - m_run)

    def fast_group(base, state, last):
        for u in range(unroll):
            state = fast_step(base + u, u % 2, state, not (last and u == unroll - 1))
        return state

    pieces, _ = raw_scores(chunk_at(0))
    m_first = jnp.concatenate([jnp.max(sp, axis=0, keepdims=True) for _, sp in pieces], axis=1)
    for cs, s_piece in pieces:
        s_scr[0][:, cs] = s_piece
    e_first = jnp.exp2(s_scr[0][...] - m_first)
    p_scr[0][...] = e_first.astype(BF16)
    acc_scr[...] = jnp.zeros((HEAD_DIM, cols), F32)
    state = (m_first, jnp.ones((1, cols), F32), jnp.sum(e_first, axis=0, keepdims=True), jnp.zeros((1, cols), F32))
    first_loop_group = 0
    if diff and n_groups > 1:
        state = fast_group(0, state, False)
        first_loop_group = 1
    if n_groups - 1 > first_loop_group:
        state = lax.fori_loop(first_loop_group, n_groups - 1, lambda t, st: fast_group(t * unroll, st, False), state)
    _, _, l_fast, jump = fast_group((n_groups - 1) * unroll, state, True)
    finalize(acc_scr[...], l_fast)

    def scores(c, slot):
        m_parts = []
        for cs, s_piece in raw_scores(c)[0]:
            s_scr[slot][:, cs] = s_piece
            m_parts.append(jnp.max(s_piece, axis=0, keepdims=True))
        return jnp.concatenate(m_parts, axis=1)

    def step(c, par, m_chunk, state, do_scores):
        m, l = state
        m_new = jnp.maximum(m, m_chunk)
        alpha = jnp.exp2(m - m_new)
        l = alpha * l
        m_next = scores(c + 1, 1 - par) if do_scores else None
        acc = alpha * acc_scr[...]
        for j in range(n_sub):
            p = jnp.exp2(s_scr[par][j * tv:(j + 1) * tv, :] - m_new)
            l = l + jnp.sum(p, axis=0, keepdims=True)
            acc = acc + jnp.dot(v_ref[0, c * n_sub + j], p.astype(BF16), preferred_element_type=F32)
        acc_scr[...] = acc
        return m_next, (m_new, l)

    def group(base, carry, last):
        m_chunk, state = carry
        for u in range(unroll):
            m_chunk, state = step(base + u, u % 2, m_chunk, state, not (last and u == unroll - 1))
        return m_chunk, state

    @pl.when(jnp.max(jump) > LAZY_MAX_JUMP)
    def _():
        acc_scr[...] = jnp.zeros((HEAD_DIM, cols), F32)
        carry = (scores(0, 0), (jnp.full((1, cols), NEG_BIG, F32), jnp.zeros((1, cols), F32)))
        if n_groups > 1:
            carry = lax.fori_loop(0, n_groups - 1, lambda t, cr: group(t * unroll, cr, False), carry)
        _, (_, l_exact) = group((n_groups - 1) * unroll, carry, True)
        finalize(acc_scr[...], l_exact)


def _flash(q_t, k, v_t, *, groups, groups_stacked, dk, tq, tk, name, diff_args=None):
    s = k.shape[0]
    n_vblk, tv = v_t.shape[1], v_t.shape[3]
    unroll = min(FLASH_KEYS_PER_TRIP, s) // tk
    assert unroll % 2 == 0 and (s // tk) % unroll == 0 and tk % tv == 0
    diff = diff_args is not None
    kern = functools.partial(_flash_kernel, groups_stacked=groups_stacked, dk=dk, tq=tq, tk=tk, tv=tv,
                             seq=s, unroll=unroll, diff=diff)
    in_specs = [pl.BlockSpec((groups_stacked * dk, tq), lambda g, i: (g, i)),
                pl.BlockSpec((s, dk), lambda g, i: (0, g)),
                pl.BlockSpec((1, n_vblk, HEAD_DIM, tv), lambda g, i: (g, 0, 0, 0))]
    args = [q_t, k, v_t]
    if diff:
        bias, far, scal, gain = diff_args
        in_specs += [pl.BlockSpec((1,) + bias.shape[1:], lambda g, i: (g, 0, 0, 0),
                                  pipeline_mode=pl.Buffered(1)),
                     pl.BlockSpec(memory_space=pltpu.SMEM),
                     pl.BlockSpec(memory_space=pltpu.SMEM),
                     pl.BlockSpec((1, HEAD_DIM), lambda g, i: (0, 0))]
        args += [bias, far, scal, gain.reshape(1, HEAD_DIM)]
        out_w = HEAD_DIM
    else:
        out_w = groups_stacked * HEAD_DIM
    return pl.pallas_call(
        kern,
        grid=(groups, s // tq),
        in_specs=in_specs,
        out_specs=pl.BlockSpec((tq, out_w), lambda g, i: (i, g)),
        out_shape=jax.ShapeDtypeStruct((s, groups * out_w), BF16),
        scratch_shapes=([pltpu.VMEM((tk, groups_stacked * tq), F32)] * 2
                        + [pltpu.VMEM((HEAD_DIM, groups_stacked * tq), F32)]
                        + [pltpu.VMEM((tk, groups_stacked * tq), BF16)] * 2),
        compiler_params=_params(("arbitrary", "arbitrary")),
        name=name,
    )(*args)


def _window_kernel(q_ref, kp_ref, kc_ref, kn_ref, vp_ref, vc_ref, vn_ref, bias_ref, sink_ref, o_ref,
                   kbuf, vbuf, *, tw, seq):
    w = WINDOW
    h = pl.program_id(0)
    i = pl.program_id(1)
    kbuf[0:w] = kp_ref[tw - w:tw]
    kbuf[w:w + tw] = kc_ref[...]
    kbuf[w + tw:w + tw + w] = kn_ref[0:w]
    vbuf[0:w] = vp_ref[tw - w:tw]
    vbuf[w:w + tw] = vc_ref[...]
    vbuf[w + tw:w + tw + w] = vn_ref[0:w]
    sink = sink_ref[h] * LOG2E
    nq = tw // w
    q3 = q_ref[...].reshape(nq, w, HEAD_DIM)
    k3 = jnp.stack([kbuf[j * w:(j + 3) * w] for j in range(nq)])
    v3 = jnp.stack([vbuf[j * w:(j + 3) * w] for j in range(nq)])
    s = jnp.einsum("jqd,jkd->jqk", q3, k3, preferred_element_type=F32) + bias_ref[...]
    kpos = (i * tw - w + w * lax.broadcasted_iota(jnp.int32, (nq, w, 3 * w), 0)
            + lax.broadcasted_iota(jnp.int32, (nq, w, 3 * w), 2))
    s = jnp.where((kpos >= 0) & (kpos < seq), s, NEG_BIG)
    m = jnp.maximum(jnp.max(s, axis=-1, keepdims=True), sink)
    p = jnp.exp2(s - m)
    denom = jnp.sum(p, axis=-1, keepdims=True) + jnp.exp2(sink - m)
    o = jnp.einsum("jqk,jkd->jqd", p.astype(BF16), v3, preferred_element_type=F32) / denom
    o_ref[...] = o.reshape(tw, HEAD_DIM).astype(o_ref.dtype)


def _window(q, k, v, bias, sink, tw=1024):
    s = q.shape[0]
    hd = HEAD_DIM
    nb = s // tw
    grp = C_HEADS // C_KV_HEADS
    prev = lambda h, i: (jnp.maximum(i - 1, 0), h // grp)
    cur = lambda h, i: (i, h // grp)
    nxt = lambda h, i: (jnp.minimum(i + 1, nb - 1), h // grp)
    kv_specs = [pl.BlockSpec((tw, hd), f) for f in (prev, cur, nxt)]
    return pl.pallas_call(
        functools.partial(_window_kernel, tw=tw, seq=s),
        grid=(C_HEADS, nb),
        in_specs=[pl.BlockSpec((tw, hd), lambda h, i: (i, h))] + kv_specs + kv_specs
                 + [pl.BlockSpec((1, WINDOW, 3 * WINDOW), lambda h, i: (h, 0, 0)),
                    pl.BlockSpec(memory_space=pltpu.SMEM)],
        out_specs=pl.BlockSpec((tw, hd), lambda h, i: (i, h)),
        out_shape=jax.ShapeDtypeStruct((s, C_HEADS * hd), BF16),
        scratch_shapes=[pltpu.VMEM((tw + 2 * WINDOW, hd), BF16),
                        pltpu.VMEM((tw + 2 * WINDOW, hd), BF16)],
        compiler_params=_params(("arbitrary", "arbitrary")),
        name="window_attn",
    )(q, k, k, k, v, v, v, bias, sink)


def _final_norm_kernel(x_ref, g_ref, o_ref):
    o_ref[...] = _rms(x_ref[...], g_ref[...])


def _final_norm(x, gain, tm=512):
    s, d = x.shape
    return pl.pallas_call(
        _final_norm_kernel,
        grid=(s // tm,),
        in_specs=[pl.BlockSpec((tm, d), lambda i: (i, 0)), pl.BlockSpec((1, d), lambda i: (0, 0))],
        out_specs=pl.BlockSpec((tm, d), lambda i: (i, 0)),
        out_shape=jax.ShapeDtypeStruct((s, d), F32),
        compiler_params=_params(("arbitrary",)),
        name="final_norm",
    )(x, gain.reshape(1, d))


def _rope_tables(pos, dim):
    inv = ROPE_THETA ** (-jnp.arange(0, dim, 2, dtype=F32) / dim)
    ang = pos.astype(F32)[:, None] * inv[None, :]
    ang = jnp.concatenate([ang, ang], axis=-1)
    return jnp.cos(ang), jnp.sin(ang)


def _rope_pack(cos, sin, half):
    width = cos.shape[1]
    lane = jnp.arange(width)
    first = (lane % (2 * half)) < half
    tabs = [cos, jnp.where(first, -sin, 0.0), jnp.where(first, 0.0, sin)]
    return jnp.stack([jnp.pad(t, ((0, 0), (0, LANES - width))) for t in tabs])


def _t5_bucket(rel):
    half = N_BUCKETS // 2
    max_exact = half // 2
    n = jnp.abs(rel)
    large = max_exact + (jnp.log(jnp.maximum(n, 1).astype(F32) / max_exact)
                         / math.log(MAX_DISTANCE / max_exact) * (half - max_exact)).astype(jnp.int32)
    large = jnp.minimum(large, half - 1)
    return jnp.where(rel > 0, half, 0) + jnp.where(n < max_exact, n, large)


def _toeplitz(fn, rows, cols, offset):
    n = rows + cols + 1
    vec = fn(jnp.arange(n) - rows + offset)
    flat = jnp.tile(vec, (1, rows))[:, :rows * (n - 1)]
    return flat.reshape(vec.shape[0], rows, n - 1)[:, :, rows:rows + cols]


TQ_A, TQ_B, TQ_D = 512, 1024, 256
TK_DENSE = 1024
FLASH_KEYS_PER_TRIP = 4096


def kernel(x, c, w_ada, b_ada, norm1, w_in, a_q_norm, a_k_norm, b_q_norm, b_kv_norm, b_w_uq, b_w_ukv,
           c_sink, d_lambda, d_sub_norm, w_out, norm2, w_ff1, w_ff2, rel_bias, final_norm):
    _, s, d = x.shape
    depth = w_ada.shape[0]
    hd = HEAD_DIM
    x2 = x.reshape(s, d)

    t = jnp.arange(s)
    grid_h = s // GRID_W
    cos_r, sin_r = (jnp.repeat(tab, GRID_W, axis=0) for tab in _rope_tables(jnp.arange(grid_h), hd // 2))
    cos_c, sin_c = (jnp.tile(tab, (grid_h, 1)) for tab in _rope_tables(jnp.arange(GRID_W), hd // 2))
    ax_tab = _rope_pack(jnp.concatenate([cos_r, cos_c], -1), jnp.concatenate([sin_r, sin_c], -1), hd // 4)
    sq_tab = _rope_pack(*_rope_tables(t, B_ROPE), B_ROPE // 2)

    bias_fn = lambda heads: (lambda rel: (rel_bias[_t5_bucket(rel)][:, heads] * LOG2E).T)
    c_heads = jnp.arange(C_HEADS)
    d_heads = C_HEADS + jnp.arange(D_HEADS)
    rel_c = jnp.arange(3 * WINDOW)[None, :] - WINDOW - jnp.arange(WINDOW)[:, None]
    bias_c = jnp.where(jnp.abs(rel_c)[None] <= WINDOW, _toeplitz(bias_fn(c_heads), WINDOW, 3 * WINDOW, -WINDOW),
                       NEG_BIG)
    n_off = TK_DENSE // TQ_D + 2
    bias_d = jnp.stack([_toeplitz(bias_fn(d_heads), TQ_D, TK_DENSE, o * TQ_D - TK_DENSE)
                        for o in range(n_off)], axis=1)
    far_d = bias_fn(d_heads)(jnp.array([-MAX_DISTANCE, MAX_DISTANCE]))
    bias_d = jnp.concatenate(
        [bias_d, jnp.broadcast_to(far_d[:, :, None, None], (D_HEADS, 2, TQ_D, TK_DENSE))], axis=1)
    bias_d = jnp.swapaxes(bias_d, 2, 3)

    mod_all = _ada_mod(c, w_ada, b_ada)

    for l in range(depth):
        mod = mod_all[l]
        lam_init = 0.8 - 0.6 * math.exp(-0.3 * l)
        lf = d_lambda[l].astype(F32)
        lam = jnp.exp(jnp.sum(lf[0] * lf[1])) - jnp.exp(jnp.sum(lf[2] * lf[3])) + lam_init
        scal = jnp.stack([lam, jnp.asarray(1.0 - lam_init, F32)])

        w_in_l = w_in[l]
        w_in_p = jnp.concatenate([w_in_l[:, :COL_B_KR + B_ROPE], jnp.zeros((d, B_KR_PAD), F32),
                                  w_in_l[:, COL_B_KR + B_ROPE:]], axis=1).astype(BF16)
        wuq = jnp.pad(b_w_uq[l].reshape(B_Q_RANK, B_HEADS, B_NOPE + B_ROPE),
                      ((0, 0), (0, 0), (0, B_QK_PAD - B_NOPE - B_ROPE))).reshape(B_Q_RANK, -1).astype(BF16)
        wukv = b_w_ukv[l].astype(BF16)

        qkv = _norm_matmul(x2, norm1[l], mod, w_in_p, shift_row=0, scale_row=1, relu2=False,
                           out_dtype=F32, tm=512, tn=IN_COLS_PAD // 2, name="in_proj")
        (qa, ka, va, qb, kb, vb, qc, kc, vc, qd, kd, vd) = _prep(
            qkv, ax_tab, sq_tab, a_q_norm[l], a_k_norm[l], b_q_norm[l], b_kv_norm[l], wuq, wukv)

        oa = _flash(qa, ka, va, groups=A_KV_HEADS, groups_stacked=A_HEADS // A_KV_HEADS, dk=hd,
                    tq=TQ_A, tk=TK_DENSE, name="flash_a")
        ob = _flash(qb, kb, vb, groups=B_HEADS, groups_stacked=1, dk=B_QK_PAD,
                    tq=TQ_B, tk=TK_DENSE, name="flash_b")
        oc = _window(qc, kc, vc, bias_c, c_sink[l])
        od = _flash(qd, kd, vd, groups=D_HEADS, groups_stacked=2, dk=hd, tq=TQ_D, tk=TK_DENSE,
                    name="flash_d", diff_args=(bias_d, far_d, scal, d_sub_norm[l]))

        w_out_l = w_out[l].astype(BF16)
        heads = [oa, ob, oc, od]
        w_parts = [w_out_l[k * 4 * hd:(k + 1) * 4 * hd] for k in range(4)]
        x2 = _matmul_res(heads, w_parts, x2, mod, gate_row=2, tm=512, tn=d, name="out_proj")

        hid = _norm_matmul(x2, norm2[l], mod, w_ff1[l].astype(BF16), shift_row=3, scale_row=4, relu2=True,
                           out_dtype=BF16, tm=1024, tn=1024, name="ff1")
        x2 = _matmul_res([hid], [w_ff2[l].astype(BF16)], x2, mod, gate_row=5, tm=512, tn=512, name="ff2")

    return _final_norm(x2, final_norm).reshape(1, s, d)
```

```python
import functools
import math

import jax
import jax.numpy as jnp
from jax import lax
from jax.experimental import pallas as pl
from jax.experimental.pallas import tpu as pltpu

F32 = jnp.float32
BF16 = jnp.bfloat16

HEAD_DIM = 128
GRID_W = 64
ROPE_THETA = 10000.0
EPS = 1e-6
A_HEADS, A_KV_HEADS = 4, 2
B_HEADS, B_NOPE, B_ROPE, B_Q_RANK, B_KV_RANK = 4, 128, 64, 384, 256
C_HEADS, C_KV_HEADS, WINDOW = 4, 2, 128
D_HEADS = 4
D_HALF = HEAD_DIM // 2
N_BUCKETS, MAX_DISTANCE = 32, 128
N_MOD = 6

LOG2E = math.log2(math.e)
NEG_BIG = -1e30
LAZY_MAX_JUMP = 60.0
N_TILED = 3

LANES = 128
MXU_DIM = 256
VMEM_LIMIT = 56 * 1024 * 1024

TV = MXU_DIM

B_KR_PAD = LANES - B_ROPE
COL_A_Q = 0
COL_A_K = COL_A_Q + A_HEADS * HEAD_DIM
COL_A_V = COL_A_K + A_KV_HEADS * HEAD_DIM
COL_B_CQ = COL_A_V + A_KV_HEADS * HEAD_DIM
COL_B_CKV = COL_B_CQ + B_Q_RANK
COL_B_KR = COL_B_CKV + B_KV_RANK
COL_C_Q = COL_B_KR + B_ROPE + B_KR_PAD
COL_C_K = COL_C_Q + C_HEADS * HEAD_DIM
COL_C_V = COL_C_K + C_KV_HEADS * HEAD_DIM
COL_D_Q = COL_C_V + C_KV_HEADS * HEAD_DIM
COL_D_K = COL_D_Q + D_HEADS * HEAD_DIM
COL_D_V = COL_D_K + D_HEADS * HEAD_DIM
IN_COLS_PAD = COL_D_V + D_HEADS * HEAD_DIM
B_QK_PAD = 2 * LANES


def _params(sem):
    return pltpu.CompilerParams(dimension_semantics=sem, vmem_limit_bytes=VMEM_LIMIT)


def _ada_kernel(c_ref, w_ref, b_ref, o_ref):
    c = c_ref[...]
    cs = c * (1.0 / (1.0 + jnp.exp(-c)))
    cs8 = jnp.broadcast_to(cs, (8, cs.shape[1])).astype(BF16)
    r = jnp.dot(cs8, w_ref[0].astype(BF16), preferred_element_type=F32)
    o_ref[0] = r[0:1] + b_ref[0]


def _ada_mod(c, w_ada, b_ada, tn=1024):
    depth, d, n = w_ada.shape
    out = pl.pallas_call(
        _ada_kernel,
        grid=(depth, n // tn),
        in_specs=[pl.BlockSpec((1, d), lambda l, j: (0, 0)),
                  pl.BlockSpec((1, d, tn), lambda l, j: (l, 0, j)),
                  pl.BlockSpec((1, 1, tn), lambda l, j: (l, 0, j))],
        out_specs=pl.BlockSpec((1, 1, tn), lambda l, j: (l, 0, j)),
        out_shape=jax.ShapeDtypeStruct((depth, 1, n), F32),
        compiler_params=_params(("arbitrary", "arbitrary")),
        name="ada_mod",
    )(c, w_ada, b_ada.reshape(depth, 1, n))
    return out.reshape(depth, N_MOD, d)


def _norm_mm_kernel(x_ref, g_ref, mod_ref, w_ref, o_ref, h_ref, *, shift_row, scale_row, relu2):
    @pl.when(pl.program_id(1) == 0)
    def _():
        x = x_ref[...]
        y = x * lax.rsqrt(jnp.mean(x * x, axis=-1, keepdims=True) + EPS) * g_ref[...]
        h = y * (1.0 + mod_ref[scale_row:scale_row + 1, :]) + mod_ref[shift_row:shift_row + 1, :]
        h_ref[...] = h.astype(BF16)

    acc = jnp.dot(h_ref[...], w_ref[...], preferred_element_type=F32)
    if relu2:
        acc = jnp.square(jnp.maximum(acc, 0.0))
    o_ref[...] = acc.astype(o_ref.dtype)


def _norm_matmul(x, gain, mod, w, *, shift_row, scale_row, relu2, out_dtype, tm, tn, name):
    s, d = x.shape
    n = w.shape[1]
    return pl.pallas_call(
        functools.partial(_norm_mm_kernel, shift_row=shift_row, scale_row=scale_row, relu2=relu2),
        grid=(s // tm, n // tn),
        in_specs=[pl.BlockSpec((tm, d), lambda i, j: (i, 0)),
                  pl.BlockSpec((1, d), lambda i, j: (0, 0)),
                  pl.BlockSpec((N_MOD, d), lambda i, j: (0, 0)),
                  pl.BlockSpec((d, tn), lambda i, j: (0, j))],
        out_specs=pl.BlockSpec((tm, tn), lambda i, j: (i, j)),
        out_shape=jax.ShapeDtypeStruct((s, n), out_dtype),
        scratch_shapes=[pltpu.VMEM((tm, d), BF16)],
        compiler_params=_params(("arbitrary", "arbitrary")),
        name=name,
    )(x, gain.reshape(1, d), mod, w)


def _mm_res_kernel(*refs, n_in, gate_row):
    a_refs, w_refs = refs[:n_in], refs[n_in:2 * n_in]
    x_ref, mod_ref, o_ref = refs[2 * n_in:]
    acc = jnp.dot(a_refs[0][...], w_refs[0][...], preferred_element_type=F32)
    for a_ref, w_ref in zip(a_refs[1:], w_refs[1:]):
        acc = acc + jnp.dot(a_ref[...], w_ref[...], preferred_element_type=F32)
    o_ref[...] = x_ref[...] + mod_ref[gate_row:gate_row + 1, :] * acc


def _matmul_res(a_list, w_list, x, mod, *, gate_row, tm, tn, name):
    s, d = x.shape
    n_in = len(a_list)
    in_specs = ([pl.BlockSpec((tm, a.shape[1]), lambda i, j: (i, 0)) for a in a_list]
                + [pl.BlockSpec((w.shape[0], tn), lambda i, j: (0, j)) for w in w_list]
                + [pl.BlockSpec((tm, tn), lambda i, j: (i, j)),
                   pl.BlockSpec((N_MOD, tn), lambda i, j: (0, j))])
    return pl.pallas_call(
        functools.partial(_mm_res_kernel, n_in=n_in, gate_row=gate_row),
        grid=(s // tm, d // tn),
        in_specs=in_specs,
        out_specs=pl.BlockSpec((tm, tn), lambda i, j: (i, j)),
        out_shape=jax.ShapeDtypeStruct((s, d), F32),
        compiler_params=_params(("arbitrary", "arbitrary")),
        name=name,
    )(*a_list, *w_list, x, mod)


def _rope(x, cos, sin_lo, sin_hi):
    return (x * cos + pltpu.roll(x, LANES - 32, 1) * sin_lo + pltpu.roll(x, 32, 1) * sin_hi)


def _rms(x, gain):
    return x * lax.rsqrt(jnp.mean(x * x, axis=-1, keepdims=True) + EPS) * gain


def _store_val_t(ref, h, v):
    v_t = v.T.astype(BF16)
    n_blk, tv = ref.shape[1], ref.shape[3]
    for j in range(n_blk):
        ref[h, j] = v_t[:, j * tv:(j + 1) * tv]


def _prep_kernel(qkv_ref, ax_ref, sq_ref, aqn_ref, akn_ref, bqn_ref, bkvn_ref, wuq_ref, wukv_ref,
                 qa_ref, ka_ref, va_ref, qb_ref, kb_ref, vb_ref, qc_ref, kc_ref, vc_ref,
                 qd_ref, kd_ref, vd_ref):
    hd = HEAD_DIM
    ax_cos, ax_lo, ax_hi = ax_ref[0], ax_ref[1], ax_ref[2]
    sq_cos, sq_lo, sq_hi = sq_ref[0], sq_ref[1], sq_ref[2]

    def col(start, width=hd):
        return qkv_ref[:, start:start + width]

    sc_a = HEAD_DIM ** -0.5 * LOG2E
    for h in range(A_HEADS):
        q = _rope(_rms(col(COL_A_Q + h * hd), aqn_ref[...]), ax_cos, ax_lo, ax_hi)
        qa_ref[h * hd:(h + 1) * hd, :] = (q * sc_a).T.astype(BF16)
    for h in range(A_KV_HEADS):
        k = _rope(_rms(col(COL_A_K + h * hd), akn_ref[...]), ax_cos, ax_lo, ax_hi)
        ka_ref[:, h * hd:(h + 1) * hd] = k.astype(BF16)
        _store_val_t(va_ref, h, col(COL_A_V + h * hd))

    sc_b = (B_NOPE + B_ROPE) ** -0.5 * LOG2E
    cq = _rms(col(COL_B_CQ, B_Q_RANK), bqn_ref[...]).astype(BF16)
    qb = jnp.dot(cq, wuq_ref[...], preferred_element_type=F32)
    ckv = _rms(col(COL_B_CKV, B_KV_RANK), bkvn_ref[...]).astype(BF16)
    kvb = jnp.dot(ckv, wukv_ref[...], preferred_element_type=F32)
    kr = _rope(col(COL_B_KR, LANES), sq_cos, sq_lo, sq_hi).astype(BF16)
    for h in range(B_HEADS):
        base = h * B_QK_PAD
        qb_ref[base:base + hd, :] = (qb[:, base:base + hd] * sc_b).T.astype(BF16)
        q_r = _rope(qb[:, base + hd:base + 2 * hd], sq_cos, sq_lo, sq_hi)
        qb_ref[base + hd:base + 2 * hd, :] = (q_r * sc_b).T.astype(BF16)
        kb_ref[:, base:base + hd] = kvb[:, base:base + hd].astype(BF16)
        kb_ref[:, base + hd:base + 2 * hd] = kr
        _store_val_t(vb_ref, h, kvb[:, base + hd:base + 2 * hd])

    sc_c = HEAD_DIM ** -0.5 * LOG2E
    qc_ref[...] = (col(COL_C_Q, C_HEADS * hd) * sc_c).astype(BF16)
    kc_ref[...] = col(COL_C_K, C_KV_HEADS * hd).astype(BF16)
    vc_ref[...] = col(COL_C_V, C_KV_HEADS * hd).astype(BF16)

    sc_d = D_HALF ** -0.5 * LOG2E
    lane = lax.broadcasted_iota(jnp.int32, (qkv_ref.shape[0], hd), 1)
    lo = lane < D_HALF
    for h in range(D_HEADS):
        q = col(COL_D_Q + h * hd) * sc_d
        qd_ref[2 * h * hd:(2 * h + 1) * hd, :] = jnp.where(lo, q, 0.0).T.astype(BF16)
        qd_ref[(2 * h + 1) * hd:(2 * h + 2) * hd, :] = jnp.where(lo, 0.0, q).T.astype(BF16)
        _store_val_t(vd_ref, h, col(COL_D_V + h * hd))
    kd_ref[...] = col(COL_D_K, D_HEADS * hd).astype(BF16)


def _prep(qkv, ax_tab, sq_tab, aqn, akn, bqn, bkvn, wuq, wukv, tm=512):
    s = qkv.shape[0]
    hd = HEAD_DIM
    row = lambda i: (i, 0)
    full = lambda i: (0, 0)

    def row_major(w):
        return pl.BlockSpec((tm, w), row), jax.ShapeDtypeStruct((s, w), BF16)

    def feat_major(w):
        return pl.BlockSpec((w, tm), lambda i: (0, i)), jax.ShapeDtypeStruct((w, s), BF16)

    def val_t(h):
        return (pl.BlockSpec((h, tm // TV, hd, TV), lambda i: (0, i, 0, 0)),
                jax.ShapeDtypeStruct((h, s // TV, hd, TV), BF16))

    outs = [feat_major(A_HEADS * hd), row_major(A_KV_HEADS * hd), val_t(A_KV_HEADS),
            feat_major(B_HEADS * B_QK_PAD), row_major(B_HEADS * B_QK_PAD), val_t(B_HEADS),
            row_major(C_HEADS * hd), row_major(C_KV_HEADS * hd), row_major(C_KV_HEADS * hd),
            feat_major(2 * D_HEADS * hd), row_major(D_HEADS * hd), val_t(D_HEADS)]
    return pl.pallas_call(
        _prep_kernel,
        grid=(s // tm,),
        in_specs=[pl.BlockSpec((tm, IN_COLS_PAD), row),
                  pl.BlockSpec((3, tm, LANES), lambda i: (0, i, 0)),
                  pl.BlockSpec((3, tm, LANES), lambda i: (0, i, 0)),
                  pl.BlockSpec((1, hd), full), pl.BlockSpec((1, hd), full),
                  pl.BlockSpec((1, B_Q_RANK), full), pl.BlockSpec((1, B_KV_RANK), full),
                  pl.BlockSpec(wuq.shape, full), pl.BlockSpec(wukv.shape, full)],
        out_specs=[spec for spec, _ in outs],
        out_shape=[shape for _, shape in outs],
        compiler_params=_params(("arbitrary",)),
        name="prep",
    )(qkv, ax_tab, sq_tab, aqn.reshape(1, hd), akn.reshape(1, hd),
      bqn.reshape(1, B_Q_RANK), bkvn.reshape(1, B_KV_RANK), wuq, wukv)


def _flash_kernel(*refs, groups_stacked, dk, tq, tk, tv, seq, unroll, diff):
    if diff:
        q_ref, k_ref, v_ref, bias_ref, far_ref, scal_ref, gain_ref, o_ref = refs[:8]
    else:
        q_ref, k_ref, v_ref, o_ref = refs[:4]
    s_scr, acc_scr, p_scr = refs[-5:-3], refs[-3], refs[-2:]
    r_stack = groups_stacked
    cols = r_stack * tq
    i = pl.program_id(1)
    n_chunks = seq // tk
    n_sub = tk // tv
    q0 = i * tq
    q_t = jnp.concatenate([q_ref[r * dk:(r + 1) * dk, :] for r in range(r_stack)], axis=1)

    def raw_scores(c, far=False, n_keys=tk):
        start = c * tk
        if not isinstance(start, int):
            start = pl.multiple_of(start, tk)
        s = jnp.dot(k_ref[pl.ds(start, n_keys), :], q_t, preferred_element_type=F32)
        if not diff:
            return [(slice(0, cols), s)], 0.0
        if far:
            g = pl.program_id(0)
            return [(slice(0, cols), s)], jnp.where((c + 1) * tk <= q0, far_ref[g, 0], far_ref[g, 1])
        n_near = tk // tq + 2
        off = c * tk - q0
        idx = jnp.where(off < -tk, n_near, jnp.where(off > tq, n_near + 1, (off + tk) // tq))
        tile = bias_ref[0, idx, 0:n_keys, :]
        return [(slice(r * tq, (r + 1) * tq), s[:, r * tq:(r + 1) * tq] + tile) for r in range(r_stack)], 0.0

    def finalize(acc_t, l):
        o_t = acc_t / l
        o = [o_t[:, r * tq:(r + 1) * tq].T for r in range(r_stack)]
        if not diff:
            for r in range(r_stack):
                o_ref[:, r * HEAD_DIM:(r + 1) * HEAD_DIM] = o[r].astype(o_ref.dtype)
            return
        lam = scal_ref[0]
        od = o[0] - lam * o[1]
        y = od * lax.rsqrt(jnp.mean(od * od, axis=-1, keepdims=True) + EPS) * gain_ref[...]
        o_ref[...] = (y * scal_ref[1]).astype(o_ref.dtype)

    def values(c, p_ref, acc):
        for j in range(n_sub):
            acc = acc + jnp.dot(v_ref[0, c * n_sub + j], p_ref[j * tv:(j + 1) * tv, :], preferred_element_type=F32)
        return acc

    n_groups = n_chunks // unroll

    def scores_probs(c, slot, m_ref_row, far):
        pieces, offset = raw_scores(c, far)
        ref_row = m_ref_row - offset
        m_parts, l_parts = [], []
        for cs, s_piece in pieces:
            e = jnp.exp2(s_piece - ref_row[:, cs])
            p_scr[slot][:, cs] = e.astype(BF16)
            m_parts.append(jnp.max(s_piece, axis=0, keepdims=True))
            l_parts.append(jnp.sum(e, axis=0, keepdims=True))
        return jnp.concatenate(m_parts, axis=1) + offset, jnp.concatenate(l_parts, axis=1)

    first_tiled = jnp.clip(q0 // tk - 1, 0, n_chunks - N_TILED)

    def is_far(pos):
        return diff and not (isinstance(pos, int) and pos < N_TILED)

    def chunk_at(pos):
        if not diff:
            return pos
        if not is_far(pos):
            return first_tiled + pos
        j = pos - N_TILED
        return j + jnp.where(j >= first_tiled, N_TILED, 0)

    def fast_step(pos, par, state, do_scores):
        m_run, beta, l, jump = state
        c = chunk_at(pos)
        if do_scores:
            m_chunk, l_chunk = scores_probs(chunk_at(pos + 1), 1 - par, m_run, is_far(pos + 1))
        acc_scr[...] = beta * values(c, p_scr[par], acc_scr[...])
        if not do_scores:
            return state
        m_new = jnp.maximum(m_run, m_chunk)
        beta_new = jnp.exp2(m_run - m_new)
        return m_new, beta_new, beta_new * (l + l_chunk), jnp.maximum(jump, m_chunk - m_run)

    def fast_group(base, state, last):
        for u in range(unroll):
            state = fast_step(base + u, u % 2, state, not (last and u == unroll - 1))
        return state

    pieces, _ = raw_scores(chunk_at(0), n_keys=tv)
    m_guess = jnp.concatenate([jnp.max(sp, axis=0, keepdims=True) for _, sp in pieces], axis=1)
    m_first, l_first = scores_probs(chunk_at(0), 0, m_guess, False)
    beta_first = jnp.exp2(m_guess - m_first)
    acc_scr[...] = jnp.zeros((HEAD_DIM, cols), F32)
    state = (m_first, beta_first, beta_first * l_first, m_first - m_guess)
    first_loop_group = 0
    if diff and n_groups > 1:
        state = fast_group(0, state, False)
        first_loop_group = 1
    if n_groups - 1 > first_loop_group:
        state = lax.fori_loop(first_loop_group, n_groups - 1, lambda t, st: fast_group(t * unroll, st, False), state)
    _, _, l_fast, jump = fast_group((n_groups - 1) * unroll, state, True)
    finalize(acc_scr[...], l_fast)

    def scores(c, slot):
        m_parts = []
        for cs, s_piece in raw_scores(c)[0]:
            s_scr[slot][:, cs] = s_piece
            m_parts.append(jnp.max(s_piece, axis=0, keepdims=True))
        return jnp.concatenate(m_parts, axis=1)

    def step(c, par, m_chunk, state, do_scores):
        m, l = state
        m_new = jnp.maximum(m, m_chunk)
        alpha = jnp.exp2(m - m_new)
        l = alpha * l
        m_next = scores(c + 1, 1 - par) if do_scores else None
        acc = alpha * acc_scr[...]
        for j in range(n_sub):
            p = jnp.exp2(s_scr[par][j * tv:(j + 1) * tv, :] - m_new)
            l = l + jnp.sum(p, axis=0, keepdims=True)
            acc = acc + jnp.dot(v_ref[0, c * n_sub + j], p.astype(BF16), preferred_element_type=F32)
        acc_scr[...] = acc
        return m_next, (m_new, l)

    def group(base, carry, last):
        m_chunk, state = carry
        for u in range(unroll):
            m_chunk, state = step(base + u, u % 2, m_chunk, state, not (last and u == unroll - 1))
        return m_chunk, state

    @pl.when(jnp.max(jump) > LAZY_MAX_JUMP)
    def _():
        acc_scr[...] = jnp.zeros((HEAD_DIM, cols), F32)
        carry = (scores(0, 0), (jnp.full((1, cols), NEG_BIG, F32), jnp.zeros((1, cols), F32)))
        if n_groups > 1:
            carry = lax.fori_loop(0, n_groups - 1, lambda t, cr: group(t * unroll, cr, False), carry)
        _, (_, l_exact) = group((n_groups - 1) * unroll, carry, True)
        finalize(acc_scr[...], l_exact)


def _flash(q_t, k, v_t, *, groups, groups_stacked, dk, tq, tk, name, diff_args=None):
    s = k.shape[0]
    n_vblk, tv = v_t.shape[1], v_t.shape[3]
    unroll = min(FLASH_KEYS_PER_TRIP, s) // tk
    assert unroll % 2 == 0 and (s // tk) % unroll == 0 and tk % tv == 0
    diff = diff_args is not None
    kern = functools.partial(_flash_kernel, groups_stacked=groups_stacked, dk=dk, tq=tq, tk=tk, tv=tv,
                             seq=s, unroll=unroll, diff=diff)
    in_specs = [pl.BlockSpec((groups_stacked * dk, tq), lambda g, i: (g, i)),
                pl.BlockSpec((s, dk), lambda g, i: (0, g)),
                pl.BlockSpec((1, n_vblk, HEAD_DIM, tv), lambda g, i: (g, 0, 0, 0))]
    args = [q_t, k, v_t]
    if diff:
        bias, far, scal, gain = diff_args
        in_specs += [pl.BlockSpec((1,) + bias.shape[1:], lambda g, i: (g, 0, 0, 0),
                                  pipeline_mode=pl.Buffered(1)),
                     pl.BlockSpec(memory_space=pltpu.SMEM),
                     pl.BlockSpec(memory_space=pltpu.SMEM),
                     pl.BlockSpec((1, HEAD_DIM), lambda g, i: (0, 0))]
        args += [bias, far, scal, gain.reshape(1, HEAD_DIM)]
        out_w = HEAD_DIM
    else:
        out_w = groups_stacked * HEAD_DIM
    return pl.pallas_call(
        kern,
        grid=(groups, s // tq),
        in_specs=in_specs,
        out_specs=pl.BlockSpec((tq, out_w), lambda g, i: (i, g)),
        out_shape=jax.ShapeDtypeStruct((s, groups * out_w), BF16),
        scratch_shapes=([pltpu.VMEM((tk, groups_stacked * tq), F32)] * 2
                        + [pltpu.VMEM((HEAD_DIM, groups_stacked * tq), F32)]
                        + [pltpu.VMEM((tk, groups_stacked * tq), BF16)] * 2),
        compiler_params=_params(("arbitrary", "arbitrary")),
        name=name,
    )(*args)


def _window_kernel(q_ref, kp_ref, kc_ref, kn_ref, vp_ref, vc_ref, vn_ref, bias_ref, sink_ref, o_ref,
                   kbuf, vbuf, *, tw, seq):
    w = WINDOW
    h = pl.program_id(0)
    i = pl.program_id(1)
    kbuf[0:w] = kp_ref[tw - w:tw]
    kbuf[w:w + tw] = kc_ref[...]
    kbuf[w + tw:w + tw + w] = kn_ref[0:w]
    vbuf[0:w] = vp_ref[tw - w:tw]
    vbuf[w:w + tw] = vc_ref[...]
    vbuf[w + tw:w + tw + w] = vn_ref[0:w]
    sink = sink_ref[h] * LOG2E
    nq = tw // w
    q3 = q_ref[...].reshape(nq, w, HEAD_DIM)
    k3 = jnp.stack([kbuf[j * w:(j + 3) * w] for j in range(nq)])
    v3 = jnp.stack([vbuf[j * w:(j + 3) * w] for j in range(nq)])
    s = jnp.einsum("jqd,jkd->jqk", q3, k3, preferred_element_type=F32) + bias_ref[...]
    kpos = (i * tw - w + w * lax.broadcasted_iota(jnp.int32, (nq, w, 3 * w), 0)
            + lax.broadcasted_iota(jnp.int32, (nq, w, 3 * w), 2))
    s = jnp.where((kpos >= 0) & (kpos < seq), s, NEG_BIG)
    m = jnp.maximum(jnp.max(s, axis=-1, keepdims=True), sink)
    p = jnp.exp2(s - m)
    denom = jnp.sum(p, axis=-1, keepdims=True) + jnp.exp2(sink - m)
    o = jnp.einsum("jqk,jkd->jqd", p.astype(BF16), v3, preferred_element_type=F32) / denom
    o_ref[...] = o.reshape(tw, HEAD_DIM).astype(o_ref.dtype)


def _window(q, k, v, bias, sink, tw=1024):
    s = q.shape[0]
    hd = HEAD_DIM
    nb = s // tw
    grp = C_HEADS // C_KV_HEADS
    prev = lambda h, i: (jnp.maximum(i - 1, 0), h // grp)
    cur = lambda h, i: (i, h // grp)
    nxt = lambda h, i: (jnp.minimum(i + 1, nb - 1), h // grp)
    kv_specs = [pl.BlockSpec((tw, hd), f) for f in (prev, cur, nxt)]
    return pl.pallas_call(
        functools.partial(_window_kernel, tw=tw, seq=s),
        grid=(C_HEADS, nb),
        in_specs=[pl.BlockSpec((tw, hd), lambda h, i: (i, h))] + kv_specs + kv_specs
                 + [pl.BlockSpec((1, WINDOW, 3 * WINDOW), lambda h, i: (h, 0, 0)),
                    pl.BlockSpec(memory_space=pltpu.SMEM)],
        out_specs=pl.BlockSpec((tw, hd), lambda h, i: (i, h)),
        out_shape=jax.ShapeDtypeStruct((s, C_HEADS * hd), BF16),
        scratch_shapes=[pltpu.VMEM((tw + 2 * WINDOW, hd), BF16),
                        pltpu.VMEM((tw + 2 * WINDOW, hd), BF16)],
        compiler_params=_params(("arbitrary", "arbitrary")),
        name="window_attn",
    )(q, k, k, k, v, v, v, bias, sink)


def _final_norm_kernel(x_ref, g_ref, o_ref):
    o_ref[...] = _rms(x_ref[...], g_ref[...])


def _final_norm(x, gain, tm=512):
    s, d = x.shape
    return pl.pallas_call(
        _final_norm_kernel,
        grid=(s // tm,),
        in_specs=[pl.BlockSpec((tm, d), lambda i: (i, 0)), pl.BlockSpec((1, d), lambda i: (0, 0))],
        out_specs=pl.BlockSpec((tm, d), lambda i: (i, 0)),
        out_shape=jax.ShapeDtypeStruct((s, d), F32),
        compiler_params=_params(("arbitrary",)),
        name="final_norm",
    )(x, gain.reshape(1, d))


def _rope_tables(pos, dim):
    inv = ROPE_THETA ** (-jnp.arange(0, dim, 2, dtype=F32) / dim)
    ang = pos.astype(F32)[:, None] * inv[None, :]
    ang = jnp.concatenate([ang, ang], axis=-1)
    return jnp.cos(ang), jnp.sin(ang)


def _rope_pack(cos, sin, half):
    width = cos.shape[1]
    lane = jnp.arange(width)
    first = (lane % (2 * half)) < half
    tabs = [cos, jnp.where(first, -sin, 0.0), jnp.where(first, 0.0, sin)]
    return jnp.stack([jnp.pad(t, ((0, 0), (0, LANES - width))) for t in tabs])


def _t5_bucket(rel):
    half = N_BUCKETS // 2
    max_exact = half // 2
    n = jnp.abs(rel)
    large = max_exact + (jnp.log(jnp.maximum(n, 1).astype(F32) / max_exact)
                         / math.log(MAX_DISTANCE / max_exact) * (half - max_exact)).astype(jnp.int32)
    large = jnp.minimum(large, half - 1)
    return jnp.where(rel > 0, half, 0) + jnp.where(n < max_exact, n, large)


def _toeplitz(fn, rows, cols, offset):
    n = rows + cols + 1
    vec = fn(jnp.arange(n) - rows + offset)
    flat = jnp.tile(vec, (1, rows))[:, :rows * (n - 1)]
    return flat.reshape(vec.shape[0], rows, n - 1)[:, :, rows:rows + cols]


TQ_A, TQ_B, TQ_D = 256, 512, 256
TK_DENSE = 1024
FLASH_KEYS_PER_TRIP = 4096


def kernel(x, c, w_ada, b_ada, norm1, w_in, a_q_norm, a_k_norm, b_q_norm, b_kv_norm, b_w_uq, b_w_ukv,
           c_sink, d_lambda, d_sub_norm, w_out, norm2, w_ff1, w_ff2, rel_bias, final_norm):
    _, s, d = x.shape
    depth = w_ada.shape[0]
    hd = HEAD_DIM
    x2 = x.reshape(s, d)

    t = jnp.arange(s)
    grid_h = s // GRID_W
    cos_r, sin_r = (jnp.repeat(tab, GRID_W, axis=0) for tab in _rope_tables(jnp.arange(grid_h), hd // 2))
    cos_c, sin_c = (jnp.tile(tab, (grid_h, 1)) for tab in _rope_tables(jnp.arange(GRID_W), hd // 2))
    ax_tab = _rope_pack(jnp.concatenate([cos_r, cos_c], -1), jnp.concatenate([sin_r, sin_c], -1), hd // 4)
    sq_tab = _rope_pack(*_rope_tables(t, B_ROPE), B_ROPE // 2)

    bias_fn = lambda heads: (lambda rel: (rel_bias[_t5_bucket(rel)][:, heads] * LOG2E).T)
    c_heads = jnp.arange(C_HEADS)
    d_heads = C_HEADS + jnp.arange(D_HEADS)
    rel_c = jnp.arange(3 * WINDOW)[None, :] - WINDOW - jnp.arange(WINDOW)[:, None]
    bias_c = jnp.where(jnp.abs(rel_c)[None] <= WINDOW, _toeplitz(bias_fn(c_heads), WINDOW, 3 * WINDOW, -WINDOW),
                       NEG_BIG)
    n_off = TK_DENSE // TQ_D + 2
    bias_d = jnp.stack([_toeplitz(bias_fn(d_heads), TQ_D, TK_DENSE, o * TQ_D - TK_DENSE)
                        for o in range(n_off)], axis=1)
    far_d = bias_fn(d_heads)(jnp.array([-MAX_DISTANCE, MAX_DISTANCE]))
    bias_d = jnp.concatenate(
        [bias_d, jnp.broadcast_to(far_d[:, :, None, None], (D_HEADS, 2, TQ_D, TK_DENSE))], axis=1)
    bias_d = jnp.swapaxes(bias_d, 2, 3)

    mod_all = _ada_mod(c, w_ada, b_ada)

    for l in range(depth):
        mod = mod_all[l]
        lam_init = 0.8 - 0.6 * math.exp(-0.3 * l)
        lf = d_lambda[l].astype(F32)
        lam = jnp.exp(jnp.sum(lf[0] * lf[1])) - jnp.exp(jnp.sum(lf[2] * lf[3])) + lam_init
        scal = jnp.stack([lam, jnp.asarray(1.0 - lam_init, F32)])

        w_in_l = w_in[l]
        w_in_p = jnp.concatenate([w_in_l[:, :COL_B_KR + B_ROPE], jnp.zeros((d, B_KR_PAD), F32),
                                  w_in_l[:, COL_B_KR + B_ROPE:]], axis=1).astype(BF16)
        wuq = jnp.pad(b_w_uq[l].reshape(B_Q_RANK, B_HEADS, B_NOPE + B_ROPE),
                      ((0, 0), (0, 0), (0, B_QK_PAD - B_NOPE - B_ROPE))).reshape(B_Q_RANK, -1).astype(BF16)
        wukv = b_w_ukv[l].astype(BF16)

        qkv = _norm_matmul(x2, norm1[l], mod, w_in_p, shift_row=0, scale_row=1, relu2=False,
                           out_dtype=F32, tm=512, tn=IN_COLS_PAD // 2, name="in_proj")
        (qa, ka, va, qb, kb, vb, qc, kc, vc, qd, kd, vd) = _prep(
            qkv, ax_tab, sq_tab, a_q_norm[l], a_k_norm[l], b_q_norm[l], b_kv_norm[l], wuq, wukv)

        oa = _flash(qa, ka, va, groups=A_KV_HEADS, groups_stacked=A_HEADS // A_KV_HEADS, dk=hd,
                    tq=TQ_A, tk=TK_DENSE, name="flash_a")
        ob = _flash(qb, kb, vb, groups=B_HEADS, groups_stacked=1, dk=B_QK_PAD,
                    tq=TQ_B, tk=TK_DENSE, name="flash_b")
        oc = _window(qc, kc, vc, bias_c, c_sink[l])
        od = _flash(qd, kd, vd, groups=D_HEADS, groups_stacked=2, dk=hd, tq=TQ_D, tk=TK_DENSE,
                    name="flash_d", diff_args=(bias_d, far_d, scal, d_sub_norm[l]))

        w_out_l = w_out[l].astype(BF16)
        heads = [oa, ob, oc, od]
        w_parts = [w_out_l[k * 4 * hd:(k + 1) * 4 * hd] for k in range(4)]
        x2 = _matmul_res(heads, w_parts, x2, mod, gate_row=2, tm=512, tn=d, name="out_proj")

        hid = _norm_matmul(x2, norm2[l], mod, w_ff1[l].astype(BF16), shift_row=3, scale_row=4, relu2=True,
                           out_dtype=BF16, tm=1024, tn=1024, name="ff1")
        x2 = _matmul_res([hid], [w_ff2[l].astype(BF16)], x2, mod, gate_row=5, tm=512, tn=512, name="ff2")

    return _final_norm(x2, final_norm).reshape(1, s, d)
```

```python
import functools
import math

import jax
import jax.numpy as jnp
from jax import lax
from jax.experimental import pallas as pl
from jax.experimental.pallas import tpu as pltpu

F32 = jnp.float32
BF16 = jnp.bfloat16

HEAD_DIM = 128
GRID_W = 64
ROPE_THETA = 10000.0
EPS = 1e-6
A_HEADS, A_KV_HEADS = 4, 2
B_HEADS, B_NOPE, B_ROPE, B_Q_RANK, B_KV_RANK = 4, 128, 64, 384, 256
C_HEADS, C_KV_HEADS, WINDOW = 4, 2, 128
D_HEADS = 4
D_HALF = HEAD_DIM // 2
N_BUCKETS, MAX_DISTANCE = 32, 128
N_MOD = 6

LOG2E = math.log2(math.e)
NEG_BIG = -1e30
LAZY_MAX_JUMP = 60.0
N_TILED = 3

LANES = 128
MXU_DIM = 256
VMEM_LIMIT = 56 * 1024 * 1024

TV = MXU_DIM

B_KR_PAD = LANES - B_ROPE
COL_A_Q = 0
COL_A_K = COL_A_Q + A_HEADS * HEAD_DIM
COL_A_V = COL_A_K + A_KV_HEADS * HEAD_DIM
COL_B_CQ = COL_A_V + A_KV_HEADS * HEAD_DIM
COL_B_CKV = COL_B_CQ + B_Q_RANK
COL_B_KR = COL_B_CKV + B_KV_RANK
COL_C_Q = COL_B_KR + B_ROPE + B_KR_PAD
COL_C_K = COL_C_Q + C_HEADS * HEAD_DIM
COL_C_V = COL_C_K + C_KV_HEADS * HEAD_DIM
COL_D_Q = COL_C_V + C_KV_HEADS * HEAD_DIM
COL_D_K = COL_D_Q + D_HEADS * HEAD_DIM
COL_D_V = COL_D_K + D_HEADS * HEAD_DIM
IN_COLS_PAD = COL_D_V + D_HEADS * HEAD_DIM
B_QK_PAD = 2 * LANES


def _params(sem):
    return pltpu.CompilerParams(dimension_semantics=sem, vmem_limit_bytes=VMEM_LIMIT)


def _ada_kernel(c_ref, w_ref, b_ref, o_ref):
    c = c_ref[...]
    cs = c * (1.0 / (1.0 + jnp.exp(-c)))
    cs8 = jnp.broadcast_to(cs, (8, cs.shape[1])).astype(BF16)
    r = jnp.dot(cs8, w_ref[0].astype(BF16), preferred_element_type=F32)
    o_ref[0] = r[0:1] + b_ref[0]


def _ada_mod(c, w_ada, b_ada, tn=1024):
    depth, d, n = w_ada.shape
    out = pl.pallas_call(
        _ada_kernel,
        grid=(depth, n // tn),
        in_specs=[pl.BlockSpec((1, d), lambda l, j: (0, 0)),
                  pl.BlockSpec((1, d, tn), lambda l, j: (l, 0, j)),
                  pl.BlockSpec((1, 1, tn), lambda l, j: (l, 0, j))],
        out_specs=pl.BlockSpec((1, 1, tn), lambda l, j: (l, 0, j)),
        out_shape=jax.ShapeDtypeStruct((depth, 1, n), F32),
        compiler_params=_params(("arbitrary", "arbitrary")),
        name="ada_mod",
    )(c, w_ada, b_ada.reshape(depth, 1, n))
    return out.reshape(depth, N_MOD, d)


def _norm_mm_kernel(x_ref, g_ref, mod_ref, w_ref, o_ref, h_ref, *, shift_row, scale_row, relu2):
    @pl.when(pl.program_id(1) == 0)
    def _():
        x = x_ref[...]
        y = x * lax.rsqrt(jnp.mean(x * x, axis=-1, keepdims=True) + EPS) * g_ref[...]
        h = y * (1.0 + mod_ref[scale_row:scale_row + 1, :]) + mod_ref[shift_row:shift_row + 1, :]
        h_ref[...] = h.astype(BF16)

    acc = jnp.dot(h_ref[...], w_ref[...], preferred_element_type=F32)
    if relu2:
        acc = jnp.square(jnp.maximum(acc, 0.0))
    o_ref[...] = acc.astype(o_ref.dtype)


def _norm_matmul(x, gain, mod, w, *, shift_row, scale_row, relu2, out_dtype, tm, tn, name):
    s, d = x.shape
    n = w.shape[1]
    return pl.pallas_call(
        functools.partial(_norm_mm_kernel, shift_row=shift_row, scale_row=scale_row, relu2=relu2),
        grid=(s // tm, n // tn),
        in_specs=[pl.BlockSpec((tm, d), lambda i, j: (i, 0)),
                  pl.BlockSpec((1, d), lambda i, j: (0, 0)),
                  pl.BlockSpec((N_MOD, d), lambda i, j: (0, 0)),
                  pl.BlockSpec((d, tn), lambda i, j: (0, j))],
        out_specs=pl.BlockSpec((tm, tn), lambda i, j: (i, j)),
        out_shape=jax.ShapeDtypeStruct((s, n), out_dtype),
        scratch_shapes=[pltpu.VMEM((tm, d), BF16)],
        compiler_params=_params(("arbitrary", "arbitrary")),
        name=name,
    )(x, gain.reshape(1, d), mod, w)


def _mm_res_kernel(*refs, n_in, gate_row):
    a_refs, w_refs = refs[:n_in], refs[n_in:2 * n_in]
    x_ref, mod_ref, o_ref = refs[2 * n_in:]
    acc = jnp.dot(a_refs[0][...], w_refs[0][...], preferred_element_type=F32)
    for a_ref, w_ref in zip(a_refs[1:], w_refs[1:]):
        acc = acc + jnp.dot(a_ref[...], w_ref[...], preferred_element_type=F32)
    o_ref[...] = x_ref[...] + mod_ref[gate_row:gate_row + 1, :] * acc


def _matmul_res(a_list, w_list, x, mod, *, gate_row, tm, tn, name):
    s, d = x.shape
    n_in = len(a_list)
    in_specs = ([pl.BlockSpec((tm, a.shape[1]), lambda i, j: (i, 0)) for a in a_list]
                + [pl.BlockSpec((w.shape[0], tn), lambda i, j: (0, j)) for w in w_list]
                + [pl.BlockSpec((tm, tn), lambda i, j: (i, j)),
                   pl.BlockSpec((N_MOD, tn), lambda i, j: (0, j))])
    return pl.pallas_call(
        functools.partial(_mm_res_kernel, n_in=n_in, gate_row=gate_row),
        grid=(s // tm, d // tn),
        in_specs=in_specs,
        out_specs=pl.BlockSpec((tm, tn), lambda i, j: (i, j)),
        out_shape=jax.ShapeDtypeStruct((s, d), F32),
        compiler_params=_params(("arbitrary", "arbitrary")),
        name=name,
    )(*a_list, *w_list, x, mod)


def _rope(x, cos, sin_lo, sin_hi):
    return (x * cos + pltpu.roll(x, LANES - 32, 1) * sin_lo + pltpu.roll(x, 32, 1) * sin_hi)


def _rms(x, gain):
    return x * lax.rsqrt(jnp.mean(x * x, axis=-1, keepdims=True) + EPS) * gain


def _store_val_t(ref, h, v):
    v_t = v.T.astype(BF16)
    n_blk, tv = ref.shape[1], ref.shape[3]
    for j in range(n_blk):
        ref[h, j] = v_t[:, j * tv:(j + 1) * tv]


def _prep_kernel(qkv_ref, ax_ref, sq_ref, aqn_ref, akn_ref, bqn_ref, bkvn_ref, wuq_ref, wukv_ref,
                 qa_ref, ka_ref, va_ref, qb_ref, kb_ref, vb_ref, qc_ref, kc_ref, vc_ref,
                 qd_ref, kd_ref, vd_ref):
    hd = HEAD_DIM
    ax_cos, ax_lo, ax_hi = ax_ref[0], ax_ref[1], ax_ref[2]
    sq_cos, sq_lo, sq_hi = sq_ref[0], sq_ref[1], sq_ref[2]

    def col(start, width=hd):
        return qkv_ref[:, start:start + width]

    sc_a = HEAD_DIM ** -0.5 * LOG2E
    for h in range(A_HEADS):
        q = _rope(_rms(col(COL_A_Q + h * hd), aqn_ref[...]), ax_cos, ax_lo, ax_hi)
        qa_ref[h * hd:(h + 1) * hd, :] = (q * sc_a).T.astype(BF16)
    for h in range(A_KV_HEADS):
        k = _rope(_rms(col(COL_A_K + h * hd), akn_ref[...]), ax_cos, ax_lo, ax_hi)
        ka_ref[:, h * hd:(h + 1) * hd] = k.astype(BF16)
        _store_val_t(va_ref, h, col(COL_A_V + h * hd))

    sc_b = (B_NOPE + B_ROPE) ** -0.5 * LOG2E
    cq = _rms(col(COL_B_CQ, B_Q_RANK), bqn_ref[...]).astype(BF16)
    qb = jnp.dot(cq, wuq_ref[...], preferred_element_type=F32)
    ckv = _rms(col(COL_B_CKV, B_KV_RANK), bkvn_ref[...]).astype(BF16)
    kvb = jnp.dot(ckv, wukv_ref[...], preferred_element_type=F32)
    kr = _rope(col(COL_B_KR, LANES), sq_cos, sq_lo, sq_hi).astype(BF16)
    for h in range(B_HEADS):
        base = h * B_QK_PAD
        qb_ref[base:base + hd, :] = (qb[:, base:base + hd] * sc_b).T.astype(BF16)
        q_r = _rope(qb[:, base + hd:base + 2 * hd], sq_cos, sq_lo, sq_hi)
        qb_ref[base + hd:base + 2 * hd, :] = (q_r * sc_b).T.astype(BF16)
        kb_ref[:, base:base + hd] = kvb[:, base:base + hd].astype(BF16)
        kb_ref[:, base + hd:base + 2 * hd] = kr
        _store_val_t(vb_ref, h, kvb[:, base + hd:base + 2 * hd])

    sc_c = HEAD_DIM ** -0.5 * LOG2E
    qc_ref[...] = (col(COL_C_Q, C_HEADS * hd) * sc_c).astype(BF16)
    kc_ref[...] = col(COL_C_K, C_KV_HEADS * hd).astype(BF16)
    vc_ref[...] = col(COL_C_V, C_KV_HEADS * hd).astype(BF16)

    sc_d = D_HALF ** -0.5 * LOG2E
    lane = lax.broadcasted_iota(jnp.int32, (qkv_ref.shape[0], hd), 1)
    lo = lane < D_HALF
    for h in range(D_HEADS):
        q = col(COL_D_Q + h * hd) * sc_d
        qd_ref[2 * h * hd:(2 * h + 1) * hd, :] = jnp.where(lo, q, 0.0).T.astype(BF16)
        qd_ref[(2 * h + 1) * hd:(2 * h + 2) * hd, :] = jnp.where(lo, 0.0, q).T.astype(BF16)
        _store_val_t(vd_ref, h, col(COL_D_V + h * hd))
    kd_ref[...] = col(COL_D_K, D_HEADS * hd).astype(BF16)


def _prep(qkv, ax_tab, sq_tab, aqn, akn, bqn, bkvn, wuq, wukv, tm=512):
    s = qkv.shape[0]
    hd = HEAD_DIM
    row = lambda i: (i, 0)
    full = lambda i: (0, 0)

    def row_major(w):
        return pl.BlockSpec((tm, w), row), jax.ShapeDtypeStruct((s, w), BF16)

    def feat_major(w):
        return pl.BlockSpec((w, tm), lambda i: (0, i)), jax.ShapeDtypeStruct((w, s), BF16)

    def val_t(h):
        return (pl.BlockSpec((h, tm // TV, hd, TV), lambda i: (0, i, 0, 0)),
                jax.ShapeDtypeStruct((h, s // TV, hd, TV), BF16))

    outs = [feat_major(A_HEADS * hd), row_major(A_KV_HEADS * hd), val_t(A_KV_HEADS),
            feat_major(B_HEADS * B_QK_PAD), row_major(B_HEADS * B_QK_PAD), val_t(B_HEADS),
            row_major(C_HEADS * hd), row_major(C_KV_HEADS * hd), row_major(C_KV_HEADS * hd),
            feat_major(2 * D_HEADS * hd), row_major(D_HEADS * hd), val_t(D_HEADS)]
    return pl.pallas_call(
        _prep_kernel,
        grid=(s // tm,),
        in_specs=[pl.BlockSpec((tm, IN_COLS_PAD), row),
                  pl.BlockSpec((3, tm, LANES), lambda i: (0, i, 0)),
                  pl.BlockSpec((3, tm, LANES), lambda i: (0, i, 0)),
                  pl.BlockSpec((1, hd), full), pl.BlockSpec((1, hd), full),
                  pl.BlockSpec((1, B_Q_RANK), full), pl.BlockSpec((1, B_KV_RANK), full),
                  pl.BlockSpec(wuq.shape, full), pl.BlockSpec(wukv.shape, full)],
        out_specs=[spec for spec, _ in outs],
        out_shape=[shape for _, shape in outs],
        compiler_params=_params(("arbitrary",)),
        name="prep",
    )(qkv, ax_tab, sq_tab, aqn.reshape(1, hd), akn.reshape(1, hd),
      bqn.reshape(1, B_Q_RANK), bkvn.reshape(1, B_KV_RANK), wuq, wukv)


def _flash_kernel(*refs, groups_stacked, dk, tq, tk, tv, seq, unroll, diff):
    if diff:
        q_ref, k_ref, v_ref, bias_ref, far_ref, scal_ref, gain_ref, o_ref = refs[:8]
    else:
        q_ref, k_ref, v_ref, o_ref = refs[:4]
    s_scr, acc_scr, p_scr = refs[-5:-3], refs[-3], refs[-2:]
    r_stack = groups_stacked
    cols = r_stack * tq
    i = pl.program_id(1)
    n_chunks = seq // tk
    n_sub = tk // tv
    q0 = i * tq
    q_t = jnp.concatenate([q_ref[r * dk:(r + 1) * dk, :] for r in range(r_stack)], axis=1)

    def raw_scores(c, far=False, n_keys=tk):
        start = c * tk
        if not isinstance(start, int):
            start = pl.multiple_of(start, tk)
        s = jnp.dot(k_ref[pl.ds(start, n_keys), :], q_t, preferred_element_type=F32)
        if not diff:
            return [(slice(0, cols), s)], 0.0
        if far:
            g = pl.program_id(0)
            return [(slice(0, cols), s)], jnp.where((c + 1) * tk <= q0, far_ref[g, 0], far_ref[g, 1])
        n_near = tk // tq + 2
        off = c * tk - q0
        idx = jnp.where(off < -tk, n_near, jnp.where(off > tq, n_near + 1, (off + tk) // tq))
        tile = bias_ref[0, idx, 0:n_keys, :]
        return [(slice(r * tq, (r + 1) * tq), s[:, r * tq:(r + 1) * tq] + tile) for r in range(r_stack)], 0.0

    def finalize(acc_t, l):
        o_t = acc_t / l
        o = [o_t[:, r * tq:(r + 1) * tq].T for r in range(r_stack)]
        if not diff:
            for r in range(r_stack):
                o_ref[:, r * HEAD_DIM:(r + 1) * HEAD_DIM] = o[r].astype(o_ref.dtype)
            return
        lam = scal_ref[0]
        od = o[0] - lam * o[1]
        y = od * lax.rsqrt(jnp.mean(od * od, axis=-1, keepdims=True) + EPS) * gain_ref[...]
        o_ref[...] = (y * scal_ref[1]).astype(o_ref.dtype)

    def values(c, p_ref, acc):
        for j in range(n_sub):
            acc = acc + jnp.dot(v_ref[0, c * n_sub + j], p_ref[j * tv:(j + 1) * tv, :], preferred_element_type=F32)
        return acc

    n_groups = n_chunks // unroll

    def scores_probs(c, slot, m_ref_row, far):
        pieces, offset = raw_scores(c, far)
        ref_row = m_ref_row - offset
        m_parts, l_parts = [], []
        for cs, s_piece in pieces:
            e = jnp.exp2(s_piece - ref_row[:, cs])
            p_scr[slot][:, cs] = e.astype(BF16)
            m_parts.append(jnp.max(s_piece, axis=0, keepdims=True))
            l_parts.append(jnp.sum(e, axis=0, keepdims=True))
        return jnp.concatenate(m_parts, axis=1) + offset, jnp.concatenate(l_parts, axis=1)

    first_tiled = jnp.clip(q0 // tk - 1, 0, n_chunks - N_TILED)

    def is_far(pos):
        return diff and not (isinstance(pos, int) and pos < N_TILED)

    def chunk_at(pos):
        if not diff:
            return pos
        if not is_far(pos):
            return first_tiled + pos
        j = pos - N_TILED
        return j + jnp.where(j >= first_tiled, N_TILED, 0)

    def fast_step(pos, par, state, do_scores):
        m_run, beta, l, jump = state
        c = chunk_at(pos)
        if do_scores:
            m_chunk, l_chunk = scores_probs(chunk_at(pos + 1), 1 - par, m_run, is_far(pos + 1))
        acc_scr[...] = beta * values(c, p_scr[par], acc_scr[...])
        if not do_scores:
            return state
        m_new = jnp.maximum(m_run, m_chunk)
        beta_new = jnp.exp2(m_run - m_new)
        return m_new, beta_new, beta_new * (l + l_chunk), jnp.maximum(jump, m_chunk - m_run)

    def fast_group(base, state, last):
        for u in range(unroll):
            state = fast_step(base + u, u % 2, state, not (last and u == unroll - 1))
        return state

    m_first, l_first = scores_probs(chunk_at(0), 0, jnp.zeros((1, cols), F32), False)
    beta_first = jnp.exp2(-m_first)
    acc_scr[...] = jnp.zeros((HEAD_DIM, cols), F32)
    state = (m_first, beta_first, beta_first * l_first, jnp.abs(m_first))
    first_loop_group = 0
    if diff and n_groups > 1:
        state = fast_group(0, state, False)
        first_loop_group = 1
    if n_groups - 1 > first_loop_group:
        state = lax.fori_loop(first_loop_group, n_groups - 1, lambda t, st: fast_group(t * unroll, st, False), state)
    _, _, l_fast, jump = fast_group((n_groups - 1) * unroll, state, True)
    finalize(acc_scr[...], l_fast)

    def scores(c, slot):
        m_parts = []
        for cs, s_piece in raw_scores(c)[0]:
            s_scr[slot][:, cs] = s_piece
            m_parts.append(jnp.max(s_piece, axis=0, keepdims=True))
        return jnp.concatenate(m_parts, axis=1)

    def step(c, par, m_chunk, state, do_scores):
        m, l = state
        m_new = jnp.maximum(m, m_chunk)
        alpha = jnp.exp2(m - m_new)
        l = alpha * l
        m_next = scores(c + 1, 1 - par) if do_scores else None
        acc = alpha * acc_scr[...]
        for j in range(n_sub):
            p = jnp.exp2(s_scr[par][j * tv:(j + 1) * tv, :] - m_new)
            l = l + jnp.sum(p, axis=0, keepdims=True)
            acc = acc + jnp.dot(v_ref[0, c * n_sub + j], p.astype(BF16), preferred_element_type=F32)
        acc_scr[...] = acc
        return m_next, (m_new, l)

    def group(base, carry, last):
        m_chunk, state = carry
        for u in range(unroll):
            m_chunk, state = step(base + u, u % 2, m_chunk, state, not (last and u == unroll - 1))
        return m_chunk, state

    @pl.when(jnp.max(jump) > LAZY_MAX_JUMP)
    def _():
        acc_scr[...] = jnp.zeros((HEAD_DIM, cols), F32)
        carry = (scores(0, 0), (jnp.full((1, cols), NEG_BIG, F32), jnp.zeros((1, cols), F32)))
        if n_groups > 1:
            carry = lax.fori_loop(0, n_groups - 1, lambda t, cr: group(t * unroll, cr, False), carry)
        _, (_, l_exact) = group((n_groups - 1) * unroll, carry, True)
        finalize(acc_scr[...], l_exact)


def _flash(q_t, k, v_t, *, groups, groups_stacked, dk, tq, tk, name, diff_args=None):
    s = k.shape[0]
    n_vblk, tv = v_t.shape[1], v_t.shape[3]
    unroll = min(FLASH_KEYS_PER_TRIP, s) // tk
    assert unroll % 2 == 0 and (s // tk) % unroll == 0 and tk % tv == 0
    diff = diff_args is not None
    kern = functools.partial(_flash_kernel, groups_stacked=groups_stacked, dk=dk, tq=tq, tk=tk, tv=tv,
                             seq=s, unroll=unroll, diff=diff)
    in_specs = [pl.BlockSpec((groups_stacked * dk, tq), lambda g, i: (g, i)),
                pl.BlockSpec((s, dk), lambda g, i: (0, g)),
                pl.BlockSpec((1, n_vblk, HEAD_DIM, tv), lambda g, i: (g, 0, 0, 0))]
    args = [q_t, k, v_t]
    if diff:
        bias, far, scal, gain = diff_args
        in_specs += [pl.BlockSpec((1,) + bias.shape[1:], lambda g, i: (g, 0, 0, 0),
                                  pipeline_mode=pl.Buffered(1)),
                     pl.BlockSpec(memory_space=pltpu.SMEM),
                     pl.BlockSpec(memory_space=pltpu.SMEM),
                     pl.BlockSpec((1, HEAD_DIM), lambda g, i: (0, 0))]
        args += [bias, far, scal, gain.reshape(1, HEAD_DIM)]
        out_w = HEAD_DIM
    else:
        out_w = groups_stacked * HEAD_DIM
    return pl.pallas_call(
        kern,
        grid=(groups, s // tq),
        in_specs=in_specs,
        out_specs=pl.BlockSpec((tq, out_w), lambda g, i: (i, g)),
        out_shape=jax.ShapeDtypeStruct((s, groups * out_w), BF16),
        scratch_shapes=([pltpu.VMEM((tk, groups_stacked * tq), F32)] * 2
                        + [pltpu.VMEM((HEAD_DIM, groups_stacked * tq), F32)]
                        + [pltpu.VMEM((tk, groups_stacked * tq), BF16)] * 2),
        compiler_params=_params(("arbitrary", "arbitrary")),
        name=name,
    )(*args)


def _window_kernel(q_ref, kp_ref, kc_ref, kn_ref, vp_ref, vc_ref, vn_ref, bias_ref, sink_ref, o_ref,
                   kbuf, vbuf, *, tw, seq):
    w = WINDOW
    h = pl.program_id(0)
    i = pl.program_id(1)
    kbuf[0:w] = kp_ref[tw - w:tw]
    kbuf[w:w + tw] = kc_ref[...]
    kbuf[w + tw:w + tw + w] = kn_ref[0:w]
    vbuf[0:w] = vp_ref[tw - w:tw]
    vbuf[w:w + tw] = vc_ref[...]
    vbuf[w + tw:w + tw + w] = vn_ref[0:w]
    sink = sink_ref[h] * LOG2E
    nq = tw // w
    q3 = q_ref[...].reshape(nq, w, HEAD_DIM)
    k3 = jnp.stack([kbuf[j * w:(j + 3) * w] for j in range(nq)])
    v3 = jnp.stack([vbuf[j * w:(j + 3) * w] for j in range(nq)])
    s = jnp.einsum("jqd,jkd->jqk", q3, k3, preferred_element_type=F32) + bias_ref[...]
    kpos = (i * tw - w + w * lax.broadcasted_iota(jnp.int32, (nq, w, 3 * w), 0)
            + lax.broadcasted_iota(jnp.int32, (nq, w, 3 * w), 2))
    s = jnp.where((kpos >= 0) & (kpos < seq), s, NEG_BIG)
    m = jnp.maximum(jnp.max(s, axis=-1, keepdims=True), sink)
    p = jnp.exp2(s - m)
    denom = jnp.sum(p, axis=-1, keepdims=True) + jnp.exp2(sink - m)
    o = jnp.einsum("jqk,jkd->jqd", p.astype(BF16), v3, preferred_element_type=F32) / denom
    o_ref[...] = o.reshape(tw, HEAD_DIM).astype(o_ref.dtype)


def _window(q, k, v, bias, sink, tw=1024):
    s = q.shape[0]
    hd = HEAD_DIM
    nb = s // tw
    grp = C_HEADS // C_KV_HEADS
    prev = lambda h, i: (jnp.maximum(i - 1, 0), h // grp)
    cur = lambda h, i: (i, h // grp)
    nxt = lambda h, i: (jnp.minimum(i + 1, nb - 1), h // grp)
    kv_specs = [pl.BlockSpec((tw, hd), f) for f in (prev, cur, nxt)]
    return pl.pallas_call(
        functools.partial(_window_kernel, tw=tw, seq=s),
        grid=(C_HEADS, nb),
        in_specs=[pl.BlockSpec((tw, hd), lambda h, i: (i, h))] + kv_specs + kv_specs
                 + [pl.BlockSpec((1, WINDOW, 3 * WINDOW), lambda h, i: (h, 0, 0)),
                    pl.BlockSpec(memory_space=pltpu.SMEM)],
        out_specs=pl.BlockSpec((tw, hd), lambda h, i: (i, h)),
        out_shape=jax.ShapeDtypeStruct((s, C_HEADS * hd), BF16),
        scratch_shapes=[pltpu.VMEM((tw + 2 * WINDOW, hd), BF16),
                        pltpu.VMEM((tw + 2 * WINDOW, hd), BF16)],
        compiler_params=_params(("arbitrary", "arbitrary")),
        name="window_attn",
    )(q, k, k, k, v, v, v, bias, sink)


def _final_norm_kernel(x_ref, g_ref, o_ref):
    o_ref[...] = _rms(x_ref[...], g_ref[...])


def _final_norm(x, gain, tm=512):
    s, d = x.shape
    return pl.pallas_call(
        _final_norm_kernel,
        grid=(s // tm,),
        in_specs=[pl.BlockSpec((tm, d), lambda i: (i, 0)), pl.BlockSpec((1, d), lambda i: (0, 0))],
        out_specs=pl.BlockSpec((tm, d), lambda i: (i, 0)),
        out_shape=jax.ShapeDtypeStruct((s, d), F32),
        compiler_params=_params(("arbitrary",)),
        name="final_norm",
    )(x, gain.reshape(1, d))


def _rope_tables(pos, dim):
    inv = ROPE_THETA ** (-jnp.arange(0, dim, 2, dtype=F32) / dim)
    ang = pos.astype(F32)[:, None] * inv[None, :]
    ang = jnp.concatenate([ang, ang], axis=-1)
    return jnp.cos(ang), jnp.sin(ang)


def _rope_pack(cos, sin, half):
    width = cos.shape[1]
    lane = jnp.arange(width)
    first = (lane % (2 * half)) < half
    tabs = [cos, jnp.where(first, -sin, 0.0), jnp.where(first, 0.0, sin)]
    return jnp.stack([jnp.pad(t, ((0, 0), (0, LANES - width))) for t in tabs])


def _t5_bucket(rel):
    half = N_BUCKETS // 2
    max_exact = half // 2
    n = jnp.abs(rel)
    large = max_exact + (jnp.log(jnp.maximum(n, 1).astype(F32) / max_exact)
                         / math.log(MAX_DISTANCE / max_exact) * (half - max_exact)).astype(jnp.int32)
    large = jnp.minimum(large, half - 1)
    return jnp.where(rel > 0, half, 0) + jnp.where(n < max_exact, n, large)


def _toeplitz(fn, rows, cols, offset):
    n = rows + cols + 1
    vec = fn(jnp.arange(n) - rows + offset)
    flat = jnp.tile(vec, (1, rows))[:, :rows * (n - 1)]
    return flat.reshape(vec.shape[0], rows, n - 1)[:, :, rows:rows + cols]


TQ_A, TQ_B, TQ_D = 256, 512, 256
TK_DENSE = 1024
FLASH_KEYS_PER_TRIP = 4096


def kernel(x, c, w_ada, b_ada, norm1, w_in, a_q_norm, a_k_norm, b_q_norm, b_kv_norm, b_w_uq, b_w_ukv,
           c_sink, d_lambda, d_sub_norm, w_out, norm2, w_ff1, w_ff2, rel_bias, final_norm):
    _, s, d = x.shape
    depth = w_ada.shape[0]
    hd = HEAD_DIM
    x2 = x.reshape(s, d)

    t = jnp.arange(s)
    grid_h = s // GRID_W
    cos_r, sin_r = (jnp.repeat(tab, GRID_W, axis=0) for tab in _rope_tables(jnp.arange(grid_h), hd // 2))
    cos_c, sin_c = (jnp.tile(tab, (grid_h, 1)) for tab in _rope_tables(jnp.arange(GRID_W), hd // 2))
    ax_tab = _rope_pack(jnp.concatenate([cos_r, cos_c], -1), jnp.concatenate([sin_r, sin_c], -1), hd // 4)
    sq_tab = _rope_pack(*_rope_tables(t, B_ROPE), B_ROPE // 2)

    bias_fn = lambda heads: (lambda rel: (rel_bias[_t5_bucket(rel)][:, heads] * LOG2E).T)
    c_heads = jnp.arange(C_HEADS)
    d_heads = C_HEADS + jnp.arange(D_HEADS)
    rel_c = jnp.arange(3 * WINDOW)[None, :] - WINDOW - jnp.arange(WINDOW)[:, None]
    bias_c = jnp.where(jnp.abs(rel_c)[None] <= WINDOW, _toeplitz(bias_fn(c_heads), WINDOW, 3 * WINDOW, -WINDOW),
                       NEG_BIG)
    n_off = TK_DENSE // TQ_D + 2
    bias_d = jnp.stack([_toeplitz(bias_fn(d_heads), TQ_D, TK_DENSE, o * TQ_D - TK_DENSE)
                        for o in range(n_off)], axis=1)
    far_d = bias_fn(d_heads)(jnp.array([-MAX_DISTANCE, MAX_DISTANCE]))
    bias_d = jnp.concatenate(
        [bias_d, jnp.broadcast_to(far_d[:, :, None, None], (D_HEADS, 2, TQ_D, TK_DENSE))], axis=1)
    bias_d = jnp.swapaxes(bias_d, 2, 3)

    mod_all = _ada_mod(c, w_ada, b_ada)

    for l in range(depth):
        mod = mod_all[l]
        lam_init = 0.8 - 0.6 * math.exp(-0.3 * l)
        lf = d_lambda[l].astype(F32)
        lam = jnp.exp(jnp.sum(lf[0] * lf[1])) - jnp.exp(jnp.sum(lf[2] * lf[3])) + lam_init
        scal = jnp.stack([lam, jnp.asarray(1.0 - lam_init, F32)])

        w_in_l = w_in[l]
        w_in_p = jnp.concatenate([w_in_l[:, :COL_B_KR + B_ROPE], jnp.zeros((d, B_KR_PAD), F32),
                                  w_in_l[:, COL_B_KR + B_ROPE:]], axis=1).astype(BF16)
        wuq = jnp.pad(b_w_uq[l].reshape(B_Q_RANK, B_HEADS, B_NOPE + B_ROPE),
                      ((0, 0), (0, 0), (0, B_QK_PAD - B_NOPE - B_ROPE))).reshape(B_Q_RANK, -1).astype(BF16)
        wukv = b_w_ukv[l].astype(BF16)

        qkv = _norm_matmul(x2, norm1[l], mod, w_in_p, shift_row=0, scale_row=1, relu2=False,
                           out_dtype=F32, tm=512, tn=IN_COLS_PAD // 2, name="in_proj")
        (qa, ka, va, qb, kb, vb, qc, kc, vc, qd, kd, vd) = _prep(
            qkv, ax_tab, sq_tab, a_q_norm[l], a_k_norm[l], b_q_norm[l], b_kv_norm[l], wuq, wukv)

        oa = _flash(qa, ka, va, groups=A_KV_HEADS, groups_stacked=A_HEADS // A_KV_HEADS, dk=hd,
                    tq=TQ_A, tk=TK_DENSE, name="flash_a")
        ob = _flash(qb, kb, vb, groups=B_HEADS, groups_stacked=1, dk=B_QK_PAD,
                    tq=TQ_B, tk=TK_DENSE, name="flash_b")
        oc = _window(qc, kc, vc, bias_c, c_sink[l])
        od = _flash(qd, kd, vd, groups=D_HEADS, groups_stacked=2, dk=hd, tq=TQ_D, tk=TK_DENSE,
                    name="flash_d", diff_args=(bias_d, far_d, scal, d_sub_norm[l]))

        w_out_l = w_out[l].astype(BF16)
        heads = [oa, ob, oc, od]
        w_parts = [w_out_l[k * 4 * hd:(k + 1) * 4 * hd] for k in range(4)]
        x2 = _matmul_res(heads, w_parts, x2, mod, gate_row=2, tm=512, tn=d, name="out_proj")

        hid = _norm_matmul(x2, norm2[l], mod, w_ff1[l].astype(BF16), shift_row=3, scale_row=4, relu2=True,
                           out_dtype=BF16, tm=1024, tn=1024, name="ff1")
        x2 = _matmul_res([hid], [w_ff2[l].astype(BF16)], x2, mod, gate_row=5, tm=512, tn=512, name="ff2")

    return _final_norm(x2, final_norm).reshape(1, s, d)
```

```python
import functools
import math

import jax
import jax.numpy as jnp
from jax import lax
from jax.experimental import pallas as pl
from jax.experimental.pallas import tpu as pltpu

F32 = jnp.float32
BF16 = jnp.bfloat16

HEAD_DIM = 128
GRID_W = 64
ROPE_THETA = 10000.0
EPS = 1e-6
A_HEADS, A_KV_HEADS = 4, 2
B_HEADS, B_NOPE, B_ROPE, B_Q_RANK, B_KV_RANK = 4, 128, 64, 384, 256
C_HEADS, C_KV_HEADS, WINDOW = 4, 2, 128
D_HEADS = 4
D_HALF = HEAD_DIM // 2
N_BUCKETS, MAX_DISTANCE = 32, 128
N_MOD = 6

LOG2E = math.log2(math.e)
NEG_BIG = -1e30
LAZY_MAX_JUMP = 60.0
N_TILED = 3

LANES = 128
MXU_DIM = 256
VMEM_LIMIT = 56 * 1024 * 1024

TV = MXU_DIM

B_KR_PAD = LANES - B_ROPE
COL_A_Q = 0
COL_A_K = COL_A_Q + A_HEADS * HEAD_DIM
COL_A_V = COL_A_K + A_KV_HEADS * HEAD_DIM
COL_B_CQ = COL_A_V + A_KV_HEADS * HEAD_DIM
COL_B_CKV = COL_B_CQ + B_Q_RANK
COL_B_KR = COL_B_CKV + B_KV_RANK
COL_C_Q = COL_B_KR + B_ROPE + B_KR_PAD
COL_C_K = COL_C_Q + C_HEADS * HEAD_DIM
COL_C_V = COL_C_K + C_KV_HEADS * HEAD_DIM
COL_D_Q = COL_C_V + C_KV_HEADS * HEAD_DIM
COL_D_K = COL_D_Q + D_HEADS * HEAD_DIM
COL_D_V = COL_D_K + D_HEADS * HEAD_DIM
IN_COLS_PAD = COL_D_V + D_HEADS * HEAD_DIM
B_QK_PAD = 2 * LANES


def _params(sem):
    return pltpu.CompilerParams(dimension_semantics=sem, vmem_limit_bytes=VMEM_LIMIT)


def _ada_kernel(c_ref, w_ref, b_ref, o_ref):
    c = c_ref[...]
    cs = c * (1.0 / (1.0 + jnp.exp(-c)))
    cs8 = jnp.broadcast_to(cs, (8, cs.shape[1])).astype(BF16)
    r = jnp.dot(cs8, w_ref[0].astype(BF16), preferred_element_type=F32)
    o_ref[0] = r[0:1] + b_ref[0]


def _ada_mod(c, w_ada, b_ada, tn=1024):
    depth, d, n = w_ada.shape
    out = pl.pallas_call(
        _ada_kernel,
        grid=(depth, n // tn),
        in_specs=[pl.BlockSpec((1, d), lambda l, j: (0, 0)),
                  pl.BlockSpec((1, d, tn), lambda l, j: (l, 0, j)),
                  pl.BlockSpec((1, 1, tn), lambda l, j: (l, 0, j))],
        out_specs=pl.BlockSpec((1, 1, tn), lambda l, j: (l, 0, j)),
        out_shape=jax.ShapeDtypeStruct((depth, 1, n), F32),
        compiler_params=_params(("arbitrary", "arbitrary")),
        name="ada_mod",
    )(c, w_ada, b_ada.reshape(depth, 1, n))
    return out.reshape(depth, N_MOD, d)


def _norm_mm_kernel(x_ref, g_ref, mod_ref, w_ref, o_ref, h_ref, *, shift_row, scale_row, relu2):
    @pl.when(pl.program_id(1) == 0)
    def _():
        x = x_ref[...]
        y = x * lax.rsqrt(jnp.mean(x * x, axis=-1, keepdims=True) + EPS) * g_ref[...]
        h = y * (1.0 + mod_ref[scale_row:scale_row + 1, :]) + mod_ref[shift_row:shift_row + 1, :]
        h_ref[...] = h.astype(BF16)

    acc = jnp.dot(h_ref[...], w_ref[...], preferred_element_type=F32)
    if relu2:
        acc = jnp.square(jnp.maximum(acc, 0.0))
    o_ref[...] = acc.astype(o_ref.dtype)


def _norm_matmul(x, gain, mod, w, *, shift_row, scale_row, relu2, out_dtype, tm, tn, name):
    s, d = x.shape
    n = w.shape[1]
    return pl.pallas_call(
        functools.partial(_norm_mm_kernel, shift_row=shift_row, scale_row=scale_row, relu2=relu2),
        grid=(s // tm, n // tn),
        in_specs=[pl.BlockSpec((tm, d), lambda i, j: (i, 0)),
                  pl.BlockSpec((1, d), lambda i, j: (0, 0)),
                  pl.BlockSpec((N_MOD, d), lambda i, j: (0, 0)),
                  pl.BlockSpec((d, tn), lambda i, j: (0, j), pipeline_mode=pl.Buffered(1 if tn == n else 2))],
        out_specs=pl.BlockSpec((tm, tn), lambda i, j: (i, j)),
        out_shape=jax.ShapeDtypeStruct((s, n), out_dtype),
        scratch_shapes=[pltpu.VMEM((tm, d), BF16)],
        compiler_params=_params(("arbitrary", "arbitrary")),
        name=name,
    )(x, gain.reshape(1, d), mod, w)


def _mm_res_kernel(*refs, n_in, gate_row):
    a_refs, w_refs = refs[:n_in], refs[n_in:2 * n_in]
    x_ref, mod_ref, o_ref = refs[2 * n_in:]
    acc = jnp.dot(a_refs[0][...], w_refs[0][...], preferred_element_type=F32)
    for a_ref, w_ref in zip(a_refs[1:], w_refs[1:]):
        acc = acc + jnp.dot(a_ref[...], w_ref[...], preferred_element_type=F32)
    o_ref[...] = x_ref[...] + mod_ref[gate_row:gate_row + 1, :] * acc


def _matmul_res(a_list, w_list, x, mod, *, gate_row, tm, tn, name):
    s, d = x.shape
    n_in = len(a_list)
    in_specs = ([pl.BlockSpec((tm, a.shape[1]), lambda i, j: (i, 0)) for a in a_list]
                + [pl.BlockSpec((w.shape[0], tn), lambda i, j: (0, j)) for w in w_list]
                + [pl.BlockSpec((tm, tn), lambda i, j: (i, j)),
                   pl.BlockSpec((N_MOD, tn), lambda i, j: (0, j))])
    return pl.pallas_call(
        functools.partial(_mm_res_kernel, n_in=n_in, gate_row=gate_row),
        grid=(s // tm, d // tn),
        in_specs=in_specs,
        out_specs=pl.BlockSpec((tm, tn), lambda i, j: (i, j)),
        out_shape=jax.ShapeDtypeStruct((s, d), F32),
        compiler_params=_params(("arbitrary", "arbitrary")),
        name=name,
    )(*a_list, *w_list, x, mod)


def _rope(x, cos, sin_lo, sin_hi):
    return (x * cos + pltpu.roll(x, LANES - 32, 1) * sin_lo + pltpu.roll(x, 32, 1) * sin_hi)


def _rms(x, gain):
    return x * lax.rsqrt(jnp.mean(x * x, axis=-1, keepdims=True) + EPS) * gain


def _store_val_t(ref, h, v):
    v_t = v.T.astype(BF16)
    n_blk, tv = ref.shape[1], ref.shape[3]
    for j in range(n_blk):
        ref[h, j] = v_t[:, j * tv:(j + 1) * tv]


def _prep_kernel(qkv_ref, ax_ref, sq_ref, aqn_ref, akn_ref, bqn_ref, bkvn_ref, wuq_ref, wukv_ref,
                 qa_ref, ka_ref, va_ref, qb_ref, kb_ref, vb_ref, qc_ref, kc_ref, vc_ref,
                 qd_ref, kd_ref, vd_ref):
    hd = HEAD_DIM
    ax_cos, ax_lo, ax_hi = ax_ref[0], ax_ref[1], ax_ref[2]
    sq_cos, sq_lo, sq_hi = sq_ref[0], sq_ref[1], sq_ref[2]

    def col(start, width=hd):
        return qkv_ref[:, start:start + width]

    sc_a = HEAD_DIM ** -0.5 * LOG2E
    for h in range(A_HEADS):
        q = _rope(_rms(col(COL_A_Q + h * hd), aqn_ref[...]), ax_cos, ax_lo, ax_hi)
        qa_ref[h * hd:(h + 1) * hd, :] = (q * sc_a).T.astype(BF16)
    for h in range(A_KV_HEADS):
        k = _rope(_rms(col(COL_A_K + h * hd), akn_ref[...]), ax_cos, ax_lo, ax_hi)
        ka_ref[:, h * hd:(h + 1) * hd] = k.astype(BF16)
        _store_val_t(va_ref, h, col(COL_A_V + h * hd))

    sc_b = (B_NOPE + B_ROPE) ** -0.5 * LOG2E
    cq = _rms(col(COL_B_CQ, B_Q_RANK), bqn_ref[...]).astype(BF16)
    qb = jnp.dot(cq, wuq_ref[...], preferred_element_type=F32)
    ckv = _rms(col(COL_B_CKV, B_KV_RANK), bkvn_ref[...]).astype(BF16)
    kvb = jnp.dot(ckv, wukv_ref[...], preferred_element_type=F32)
    kr = _rope(col(COL_B_KR, LANES), sq_cos, sq_lo, sq_hi).astype(BF16)
    for h in range(B_HEADS):
        base = h * B_QK_PAD
        qb_ref[base:base + hd, :] = (qb[:, base:base + hd] * sc_b).T.astype(BF16)
        q_r = _rope(qb[:, base + hd:base + 2 * hd], sq_cos, sq_lo, sq_hi)
        qb_ref[base + hd:base + 2 * hd, :] = (q_r * sc_b).T.astype(BF16)
        kb_ref[:, base:base + hd] = kvb[:, base:base + hd].astype(BF16)
        kb_ref[:, base + hd:base + 2 * hd] = kr
        _store_val_t(vb_ref, h, kvb[:, base + hd:base + 2 * hd])

    sc_c = HEAD_DIM ** -0.5 * LOG2E
    qc_ref[...] = (col(COL_C_Q, C_HEADS * hd) * sc_c).astype(BF16)
    kc_ref[...] = col(COL_C_K, C_KV_HEADS * hd).astype(BF16)
    vc_ref[...] = col(COL_C_V, C_KV_HEADS * hd).astype(BF16)

    sc_d = D_HALF ** -0.5 * LOG2E
    lane = lax.broadcasted_iota(jnp.int32, (qkv_ref.shape[0], hd), 1)
    lo = lane < D_HALF
    for h in range(D_HEADS):
        q = col(COL_D_Q + h * hd) * sc_d
        qd_ref[2 * h * hd:(2 * h + 1) * hd, :] = jnp.where(lo, q, 0.0).T.astype(BF16)
        qd_ref[(2 * h + 1) * hd:(2 * h + 2) * hd, :] = jnp.where(lo, 0.0, q).T.astype(BF16)
        _store_val_t(vd_ref, h, col(COL_D_V + h * hd))
    kd_ref[...] = col(COL_D_K, D_HEADS * hd).astype(BF16)


def _prep(qkv, ax_tab, sq_tab, aqn, akn, bqn, bkvn, wuq, wukv, tm=512):
    s = qkv.shape[0]
    hd = HEAD_DIM
    row = lambda i: (i, 0)
    full = lambda i: (0, 0)

    def row_major(w):
        return pl.BlockSpec((tm, w), row), jax.ShapeDtypeStruct((s, w), BF16)

    def feat_major(w):
        return pl.BlockSpec((w, tm), lambda i: (0, i)), jax.ShapeDtypeStruct((w, s), BF16)

    def val_t(h):
        return (pl.BlockSpec((h, tm // TV, hd, TV), lambda i: (0, i, 0, 0)),
                jax.ShapeDtypeStruct((h, s // TV, hd, TV), BF16))

    outs = [feat_major(A_HEADS * hd), row_major(A_KV_HEADS * hd), val_t(A_KV_HEADS),
            feat_major(B_HEADS * B_QK_PAD), row_major(B_HEADS * B_QK_PAD), val_t(B_HEADS),
            row_major(C_HEADS * hd), row_major(C_KV_HEADS * hd), row_major(C_KV_HEADS * hd),
            feat_major(2 * D_HEADS * hd), row_major(D_HEADS * hd), val_t(D_HEADS)]
    return pl.pallas_call(
        _prep_kernel,
        grid=(s // tm,),
        in_specs=[pl.BlockSpec((tm, IN_COLS_PAD), row),
                  pl.BlockSpec((3, tm, LANES), lambda i: (0, i, 0)),
                  pl.BlockSpec((3, tm, LANES), lambda i: (0, i, 0)),
                  pl.BlockSpec((1, hd), full), pl.BlockSpec((1, hd), full),
                  pl.BlockSpec((1, B_Q_RANK), full), pl.BlockSpec((1, B_KV_RANK), full),
                  pl.BlockSpec(wuq.shape, full), pl.BlockSpec(wukv.shape, full)],
        out_specs=[spec for spec, _ in outs],
        out_shape=[shape for _, shape in outs],
        compiler_params=_params(("arbitrary",)),
        name="prep",
    )(qkv, ax_tab, sq_tab, aqn.reshape(1, hd), akn.reshape(1, hd),
      bqn.reshape(1, B_Q_RANK), bkvn.reshape(1, B_KV_RANK), wuq, wukv)


def _flash_kernel(*refs, groups_stacked, dk, tq, tk, tv, seq, unroll, diff):
    if diff:
        q_ref, k_ref, v_ref, bias_ref, far_ref, scal_ref, gain_ref, o_ref = refs[:8]
    else:
        q_ref, k_ref, v_ref, o_ref = refs[:4]
    s_scr, acc_scr, p_scr = refs[-5:-3], refs[-3], refs[-2:]
    r_stack = groups_stacked
    cols = r_stack * tq
    i = pl.program_id(1)
    n_chunks = seq // tk
    n_sub = tk // tv
    q0 = i * tq
    q_t = jnp.concatenate([q_ref[r * dk:(r + 1) * dk, :] for r in range(r_stack)], axis=1)

    def raw_scores(c, far=False, n_keys=tk):
        start = c * tk
        if not isinstance(start, int):
            start = pl.multiple_of(start, tk)
        s = jnp.dot(k_ref[pl.ds(start, n_keys), :], q_t, preferred_element_type=F32)
        if not diff:
            return [(slice(0, cols), s)], 0.0
        if far:
            g = pl.program_id(0)
            return [(slice(0, cols), s)], jnp.where((c + 1) * tk <= q0, far_ref[g, 0], far_ref[g, 1])
        n_near = tk // tq + 2
        off = c * tk - q0
        idx = jnp.where(off < -tk, n_near, jnp.where(off > tq, n_near + 1, (off + tk) // tq))
        tile = bias_ref[0, idx, 0:n_keys, :]
        return [(slice(r * tq, (r + 1) * tq), s[:, r * tq:(r + 1) * tq] + tile) for r in range(r_stack)], 0.0

    def finalize(acc_t, l):
        o_t = acc_t / l
        o = [o_t[:, r * tq:(r + 1) * tq].T for r in range(r_stack)]
        if not diff:
            for r in range(r_stack):
                o_ref[:, r * HEAD_DIM:(r + 1) * HEAD_DIM] = o[r].astype(o_ref.dtype)
            return
        lam = scal_ref[0]
        od = o[0] - lam * o[1]
        y = od * lax.rsqrt(jnp.mean(od * od, axis=-1, keepdims=True) + EPS) * gain_ref[...]
        o_ref[...] = (y * scal_ref[1]).astype(o_ref.dtype)

    def values(c, p_ref, acc):
        for j in range(n_sub):
            acc = acc + jnp.dot(v_ref[0, c * n_sub + j], p_ref[j * tv:(j + 1) * tv, :], preferred_element_type=F32)
        return acc

    n_groups = n_chunks // unroll

    def scores_probs(c, slot, m_ref_row, far):
        pieces, offset = raw_scores(c, far)
        ref_row = m_ref_row - offset
        m_parts, l_parts = [], []
        for cs, s_piece in pieces:
            e = jnp.exp2(s_piece - ref_row[:, cs])
            p_scr[slot][:, cs] = e.astype(BF16)
            m_parts.append(jnp.max(s_piece, axis=0, keepdims=True))
            l_parts.append(jnp.sum(e, axis=0, keepdims=True))
        return jnp.concatenate(m_parts, axis=1) + offset, jnp.concatenate(l_parts, axis=1)

    first_tiled = jnp.clip(q0 // tk - 1, 0, n_chunks - N_TILED)

    def is_far(pos):
        return diff and not (isinstance(pos, int) and pos < N_TILED)

    def chunk_at(pos):
        if not diff:
            return pos
        if not is_far(pos):
            return first_tiled + pos
        j = pos - N_TILED
        return j + jnp.where(j >= first_tiled, N_TILED, 0)

    def fast_step(pos, par, state, do_scores):
        m_run, beta, l, jump = state
        c = chunk_at(pos)
        if do_scores:
            m_chunk, l_chunk = scores_probs(chunk_at(pos + 1), 1 - par, m_run, is_far(pos + 1))
        acc_scr[...] = beta * values(c, p_scr[par], acc_scr[...])
        if not do_scores:
            return state
        m_new = jnp.maximum(m_run, m_chunk)
        beta_new = jnp.exp2(m_run - m_new)
        return m_new, beta_new, beta_new * (l + l_chunk), jnp.maximum(jump, m_chunk - m_run)

    def fast_group(base, state, last):
        for u in range(unroll):
            state = fast_step(base + u, u % 2, state, not (last and u == unroll - 1))
        return state

    m_first, l_first = scores_probs(chunk_at(0), 0, jnp.zeros((1, cols), F32), False)
    beta_first = jnp.exp2(-m_first)
    acc_scr[...] = jnp.zeros((HEAD_DIM, cols), F32)
    state = (m_first, beta_first, beta_first * l_first, jnp.abs(m_first))
    first_loop_group = 0
    if diff and n_groups > 1:
        state = fast_group(0, state, False)
        first_loop_group = 1
    if n_groups - 1 > first_loop_group:
        state = lax.fori_loop(first_loop_group, n_groups - 1, lambda t, st: fast_group(t * unroll, st, False), state)
    _, _, l_fast, jump = fast_group((n_groups - 1) * unroll, state, True)
    finalize(acc_scr[...], l_fast)

    def scores(c, slot):
        m_parts = []
        for cs, s_piece in raw_scores(c)[0]:
            s_scr[slot][:, cs] = s_piece
            m_parts.append(jnp.max(s_piece, axis=0, keepdims=True))
        return jnp.concatenate(m_parts, axis=1)

    def step(c, par, m_chunk, state, do_scores):
        m, l = state
        m_new = jnp.maximum(m, m_chunk)
        alpha = jnp.exp2(m - m_new)
        l = alpha * l
        m_next = scores(c + 1, 1 - par) if do_scores else None
        acc = alpha * acc_scr[...]
        for j in range(n_sub):
            p = jnp.exp2(s_scr[par][j * tv:(j + 1) * tv, :] - m_new)
            l = l + jnp.sum(p, axis=0, keepdims=True)
            acc = acc + jnp.dot(v_ref[0, c * n_sub + j], p.astype(BF16), preferred_element_type=F32)
        acc_scr[...] = acc
        return m_next, (m_new, l)

    def group(base, carry, last):
        m_chunk, state = carry
        for u in range(unroll):
            m_chunk, state = step(base + u, u % 2, m_chunk, state, not (last and u == unroll - 1))
        return m_chunk, state

    @pl.when(jnp.max(jump) > LAZY_MAX_JUMP)
    def _():
        acc_scr[...] = jnp.zeros((HEAD_DIM, cols), F32)
        carry = (scores(0, 0), (jnp.full((1, cols), NEG_BIG, F32), jnp.zeros((1, cols), F32)))
        if n_groups > 1:
            carry = lax.fori_loop(0, n_groups - 1, lambda t, cr: group(t * unroll, cr, False), carry)
        _, (_, l_exact) = group((n_groups - 1) * unroll, carry, True)
        finalize(acc_scr[...], l_exact)


def _flash(q_t, k, v_t, *, groups, groups_stacked, dk, tq, tk, name, diff_args=None):
    s = k.shape[0]
    n_vblk, tv = v_t.shape[1], v_t.shape[3]
    unroll = min(FLASH_KEYS_PER_TRIP, s) // tk
    assert unroll % 2 == 0 and (s // tk) % unroll == 0 and tk % tv == 0
    diff = diff_args is not None
    kern = functools.partial(_flash_kernel, groups_stacked=groups_stacked, dk=dk, tq=tq, tk=tk, tv=tv,
                             seq=s, unroll=unroll, diff=diff)
    in_specs = [pl.BlockSpec((groups_stacked * dk, tq), lambda g, i: (g, i)),
                pl.BlockSpec((s, dk), lambda g, i: (0, g)),
                pl.BlockSpec((1, n_vblk, HEAD_DIM, tv), lambda g, i: (g, 0, 0, 0))]
    args = [q_t, k, v_t]
    if diff:
        bias, far, scal, gain = diff_args
        in_specs += [pl.BlockSpec((1,) + bias.shape[1:], lambda g, i: (g, 0, 0, 0),
                                  pipeline_mode=pl.Buffered(1)),
                     pl.BlockSpec(memory_space=pltpu.SMEM),
                     pl.BlockSpec(memory_space=pltpu.SMEM),
                     pl.BlockSpec((1, HEAD_DIM), lambda g, i: (0, 0))]
        args += [bias, far, scal, gain.reshape(1, HEAD_DIM)]
        out_w = HEAD_DIM
    else:
        out_w = groups_stacked * HEAD_DIM
    return pl.pallas_call(
        kern,
        grid=(groups, s // tq),
        in_specs=in_specs,
        out_specs=pl.BlockSpec((tq, out_w), lambda g, i: (i, g)),
        out_shape=jax.ShapeDtypeStruct((s, groups * out_w), BF16),
        scratch_shapes=([pltpu.VMEM((tk, groups_stacked * tq), F32)] * 2
                        + [pltpu.VMEM((HEAD_DIM, groups_stacked * tq), F32)]
                        + [pltpu.VMEM((tk, groups_stacked * tq), BF16)] * 2),
        compiler_params=_params(("arbitrary", "arbitrary")),
        name=name,
    )(*args)


def _window_kernel(q_ref, kp_ref, kc_ref, kn_ref, vp_ref, vc_ref, vn_ref, bias_ref, sink_ref, o_ref,
                   kbuf, vbuf, *, tw, seq):
    w = WINDOW
    h = pl.program_id(0)
    i = pl.program_id(1)
    kbuf[0:w] = kp_ref[tw - w:tw]
    kbuf[w:w + tw] = kc_ref[...]
    kbuf[w + tw:w + tw + w] = kn_ref[0:w]
    vbuf[0:w] = vp_ref[tw - w:tw]
    vbuf[w:w + tw] = vc_ref[...]
    vbuf[w + tw:w + tw + w] = vn_ref[0:w]
    sink = sink_ref[h] * LOG2E
    nq = tw // w
    q3 = q_ref[...].reshape(nq, w, HEAD_DIM)
    k3 = jnp.stack([kbuf[j * w:(j + 3) * w] for j in range(nq)])
    v3 = jnp.stack([vbuf[j * w:(j + 3) * w] for j in range(nq)])
    s = jnp.einsum("jqd,jkd->jqk", q3, k3, preferred_element_type=F32) + bias_ref[...]
    kpos = (i * tw - w + w * lax.broadcasted_iota(jnp.int32, (nq, w, 3 * w), 0)
            + lax.broadcasted_iota(jnp.int32, (nq, w, 3 * w), 2))
    s = jnp.where((kpos >= 0) & (kpos < seq), s, NEG_BIG)
    m = jnp.maximum(jnp.max(s, axis=-1, keepdims=True), sink)
    p = jnp.exp2(s - m)
    denom = jnp.sum(p, axis=-1, keepdims=True) + jnp.exp2(sink - m)
    o = jnp.einsum("jqk,jkd->jqd", p.astype(BF16), v3, preferred_element_type=F32) / denom
    o_ref[...] = o.reshape(tw, HEAD_DIM).astype(o_ref.dtype)


def _window(q, k, v, bias, sink, tw=1024):
    s = q.shape[0]
    hd = HEAD_DIM
    nb = s // tw
    grp = C_HEADS // C_KV_HEADS
    prev = lambda h, i: (jnp.maximum(i - 1, 0), h // grp)
    cur = lambda h, i: (i, h // grp)
    nxt = lambda h, i: (jnp.minimum(i + 1, nb - 1), h // grp)
    kv_specs = [pl.BlockSpec((tw, hd), f) for f in (prev, cur, nxt)]
    return pl.pallas_call(
        functools.partial(_window_kernel, tw=tw, seq=s),
        grid=(C_HEADS, nb),
        in_specs=[pl.BlockSpec((tw, hd), lambda h, i: (i, h))] + kv_specs + kv_specs
                 + [pl.BlockSpec((1, WINDOW, 3 * WINDOW), lambda h, i: (h, 0, 0)),
                    pl.BlockSpec(memory_space=pltpu.SMEM)],
        out_specs=pl.BlockSpec((tw, hd), lambda h, i: (i, h)),
        out_shape=jax.ShapeDtypeStruct((s, C_HEADS * hd), BF16),
        scratch_shapes=[pltpu.VMEM((tw + 2 * WINDOW, hd), BF16),
                        pltpu.VMEM((tw + 2 * WINDOW, hd), BF16)],
        compiler_params=_params(("arbitrary", "arbitrary")),
        name="window_attn",
    )(q, k, k, k, v, v, v, bias, sink)


def _final_norm_kernel(x_ref, g_ref, o_ref):
    o_ref[...] = _rms(x_ref[...], g_ref[...])


def _final_norm(x, gain, tm=512):
    s, d = x.shape
    return pl.pallas_call(
        _final_norm_kernel,
        grid=(s // tm,),
        in_specs=[pl.BlockSpec((tm, d), lambda i: (i, 0)), pl.BlockSpec((1, d), lambda i: (0, 0))],
        out_specs=pl.BlockSpec((tm, d), lambda i: (i, 0)),
        out_shape=jax.ShapeDtypeStruct((s, d), F32),
        compiler_params=_params(("arbitrary",)),
        name="final_norm",
    )(x, gain.reshape(1, d))


def _rope_tables(pos, dim):
    inv = ROPE_THETA ** (-jnp.arange(0, dim, 2, dtype=F32) / dim)
    ang = pos.astype(F32)[:, None] * inv[None, :]
    ang = jnp.concatenate([ang, ang], axis=-1)
    return jnp.cos(ang), jnp.sin(ang)


def _rope_pack(cos, sin, half):
    width = cos.shape[1]
    lane = jnp.arange(width)
    first = (lane % (2 * half)) < half
    tabs = [cos, jnp.where(first, -sin, 0.0), jnp.where(first, 0.0, sin)]
    return jnp.stack([jnp.pad(t, ((0, 0), (0, LANES - width))) for t in tabs])


def _t5_bucket(rel):
    half = N_BUCKETS // 2
    max_exact = half // 2
    n = jnp.abs(rel)
    large = max_exact + (jnp.log(jnp.maximum(n, 1).astype(F32) / max_exact)
                         / math.log(MAX_DISTANCE / max_exact) * (half - max_exact)).astype(jnp.int32)
    large = jnp.minimum(large, half - 1)
    return jnp.where(rel > 0, half, 0) + jnp.where(n < max_exact, n, large)


def _toeplitz(fn, rows, cols, offset):
    n = rows + cols + 1
    vec = fn(jnp.arange(n) - rows + offset)
    flat = jnp.tile(vec, (1, rows))[:, :rows * (n - 1)]
    return flat.reshape(vec.shape[0], rows, n - 1)[:, :, rows:rows + cols]


TQ_A, TQ_B, TQ_D = 256, 512, 256
TK_DENSE = 1024
FLASH_KEYS_PER_TRIP = 4096


def kernel(x, c, w_ada, b_ada, norm1, w_in, a_q_norm, a_k_norm, b_q_norm, b_kv_norm, b_w_uq, b_w_ukv,
           c_sink, d_lambda, d_sub_norm, w_out, norm2, w_ff1, w_ff2, rel_bias, final_norm):
    _, s, d = x.shape
    depth = w_ada.shape[0]
    hd = HEAD_DIM
    x2 = x.reshape(s, d)

    t = jnp.arange(s)
    grid_h = s // GRID_W
    cos_r, sin_r = (jnp.repeat(tab, GRID_W, axis=0) for tab in _rope_tables(jnp.arange(grid_h), hd // 2))
    cos_c, sin_c = (jnp.tile(tab, (grid_h, 1)) for tab in _rope_tables(jnp.arange(GRID_W), hd // 2))
    ax_tab = _rope_pack(jnp.concatenate([cos_r, cos_c], -1), jnp.concatenate([sin_r, sin_c], -1), hd // 4)
    sq_tab = _rope_pack(*_rope_tables(t, B_ROPE), B_ROPE // 2)

    bias_fn = lambda heads: (lambda rel: (rel_bias[_t5_bucket(rel)][:, heads] * LOG2E).T)
    c_heads = jnp.arange(C_HEADS)
    d_heads = C_HEADS + jnp.arange(D_HEADS)
    rel_c = jnp.arange(3 * WINDOW)[None, :] - WINDOW - jnp.arange(WINDOW)[:, None]
    bias_c = jnp.where(jnp.abs(rel_c)[None] <= WINDOW, _toeplitz(bias_fn(c_heads), WINDOW, 3 * WINDOW, -WINDOW),
                       NEG_BIG)
    n_off = TK_DENSE // TQ_D + 2
    bias_d = jnp.stack([_toeplitz(bias_fn(d_heads), TQ_D, TK_DENSE, o * TQ_D - TK_DENSE)
                        for o in range(n_off)], axis=1)
    far_d = bias_fn(d_heads)(jnp.array([-MAX_DISTANCE, MAX_DISTANCE]))
    bias_d = jnp.concatenate(
        [bias_d, jnp.broadcast_to(far_d[:, :, None, None], (D_HEADS, 2, TQ_D, TK_DENSE))], axis=1)
    bias_d = jnp.swapaxes(bias_d, 2, 3)

    mod_all = _ada_mod(c, w_ada, b_ada)

    for l in range(depth):
        mod = mod_all[l]
        lam_init = 0.8 - 0.6 * math.exp(-0.3 * l)
        lf = d_lambda[l].astype(F32)
        lam = jnp.exp(jnp.sum(lf[0] * lf[1])) - jnp.exp(jnp.sum(lf[2] * lf[3])) + lam_init
        scal = jnp.stack([lam, jnp.asarray(1.0 - lam_init, F32)])

        w_in_l = w_in[l]
        w_in_p = jnp.concatenate([w_in_l[:, :COL_B_KR + B_ROPE], jnp.zeros((d, B_KR_PAD), F32),
                                  w_in_l[:, COL_B_KR + B_ROPE:]], axis=1).astype(BF16)
        wuq = jnp.pad(b_w_uq[l].reshape(B_Q_RANK, B_HEADS, B_NOPE + B_ROPE),
                      ((0, 0), (0, 0), (0, B_QK_PAD - B_NOPE - B_ROPE))).reshape(B_Q_RANK, -1).astype(BF16)
        wukv = b_w_ukv[l].astype(BF16)

        qkv = _norm_matmul(x2, norm1[l], mod, w_in_p, shift_row=0, scale_row=1, relu2=False,
                           out_dtype=F32, tm=512, tn=IN_COLS_PAD, name="in_proj")
        (qa, ka, va, qb, kb, vb, qc, kc, vc, qd, kd, vd) = _prep(
            qkv, ax_tab, sq_tab, a_q_norm[l], a_k_norm[l], b_q_norm[l], b_kv_norm[l], wuq, wukv)

        oa = _flash(qa, ka, va, groups=A_KV_HEADS, groups_stacked=A_HEADS // A_KV_HEADS, dk=hd,
                    tq=TQ_A, tk=TK_DENSE, name="flash_a")
        ob = _flash(qb, kb, vb, groups=B_HEADS, groups_stacked=1, dk=B_QK_PAD,
                    tq=TQ_B, tk=TK_DENSE, name="flash_b")
        oc = _window(qc, kc, vc, bias_c, c_sink[l])
        od = _flash(qd, kd, vd, groups=D_HEADS, groups_stacked=2, dk=hd, tq=TQ_D, tk=TK_DENSE,
                    name="flash_d", diff_args=(bias_d, far_d, scal, d_sub_norm[l]))

        w_out_l = w_out[l].astype(BF16)
        heads = [oa, ob, oc, od]
        w_parts = [w_out_l[k * 4 * hd:(k + 1) * 4 * hd] for k in range(4)]
        x2 = _matmul_res(heads, w_parts, x2, mod, gate_row=2, tm=512, tn=d, name="out_proj")

        hid = _norm_matmul(x2, norm2[l], mod, w_ff1[l].astype(BF16), shift_row=3, scale_row=4, relu2=True,
                           out_dtype=BF16, tm=1024, tn=1024, name="ff1")
        x2 = _matmul_res([hid], [w_ff2[l].astype(BF16)], x2, mod, gate_row=5, tm=1024, tn=256, name="ff2")

    return _final_norm(x2, final_norm).reshape(1, s, d)
```

```python
import functools
import math

import jax
import jax.numpy as jnp
from jax import lax
from jax.experimental import pallas as pl
from jax.experimental.pallas import tpu as pltpu

F32 = jnp.float32
BF16 = jnp.bfloat16

HEAD_DIM = 128
GRID_W = 64
ROPE_THETA = 10000.0
EPS = 1e-6
A_HEADS, A_KV_HEADS = 4, 2
B_HEADS, B_NOPE, B_ROPE, B_Q_RANK, B_KV_RANK = 4, 128, 64, 384, 256
C_HEADS, C_KV_HEADS, WINDOW = 4, 2, 128
D_HEADS = 4
D_HALF = HEAD_DIM // 2
N_BUCKETS, MAX_DISTANCE = 32, 128
N_MOD = 6

LOG2E = math.log2(math.e)
NEG_BIG = -1e30
LAZY_MAX_JUMP = 60.0
N_TILED = 3

LANES = 128
MXU_DIM = 256
VMEM_LIMIT = 56 * 1024 * 1024

TV = MXU_DIM

B_KR_PAD = LANES - B_ROPE
COL_A_Q = 0
COL_A_K = COL_A_Q + A_HEADS * HEAD_DIM
COL_A_V = COL_A_K + A_KV_HEADS * HEAD_DIM
COL_B_CQ = COL_A_V + A_KV_HEADS * HEAD_DIM
COL_B_CKV = COL_B_CQ + B_Q_RANK
COL_B_KR = COL_B_CKV + B_KV_RANK
COL_C_Q = COL_B_KR + B_ROPE + B_KR_PAD
COL_C_K = COL_C_Q + C_HEADS * HEAD_DIM
COL_C_V = COL_C_K + C_KV_HEADS * HEAD_DIM
COL_D_Q = COL_C_V + C_KV_HEADS * HEAD_DIM
COL_D_K = COL_D_Q + D_HEADS * HEAD_DIM
COL_D_V = COL_D_K + D_HEADS * HEAD_DIM
IN_COLS_PAD = COL_D_V + D_HEADS * HEAD_DIM
B_QK_PAD = 2 * LANES


def _params(sem):
    return pltpu.CompilerParams(dimension_semantics=sem, vmem_limit_bytes=VMEM_LIMIT)


def _ada_kernel(c_ref, w_ref, b_ref, o_ref):
    c = c_ref[...]
    cs = c * (1.0 / (1.0 + jnp.exp(-c)))
    cs8 = jnp.broadcast_to(cs, (8, cs.shape[1])).astype(BF16)
    r = jnp.dot(cs8, w_ref[0].astype(BF16), preferred_element_type=F32)
    o_ref[0] = r[0:1] + b_ref[0]


def _ada_mod(c, w_ada, b_ada, tn=1024):
    depth, d, n = w_ada.shape
    out = pl.pallas_call(
        _ada_kernel,
        grid=(depth, n // tn),
        in_specs=[pl.BlockSpec((1, d), lambda l, j: (0, 0)),
                  pl.BlockSpec((1, d, tn), lambda l, j: (l, 0, j)),
                  pl.BlockSpec((1, 1, tn), lambda l, j: (l, 0, j))],
        out_specs=pl.BlockSpec((1, 1, tn), lambda l, j: (l, 0, j)),
        out_shape=jax.ShapeDtypeStruct((depth, 1, n), F32),
        compiler_params=_params(("arbitrary", "arbitrary")),
        name="ada_mod",
    )(c, w_ada, b_ada.reshape(depth, 1, n))
    return out.reshape(depth, N_MOD, d)


def _norm_mm_kernel(x_ref, g_ref, mod_ref, w_ref, o_ref, h_ref, *, shift_row, scale_row, relu2):
    @pl.when(pl.program_id(1) == 0)
    def _():
        x = x_ref[...]
        y = x * lax.rsqrt(jnp.mean(x * x, axis=-1, keepdims=True) + EPS) * g_ref[...]
        h = y * (1.0 + mod_ref[scale_row:scale_row + 1, :]) + mod_ref[shift_row:shift_row + 1, :]
        h_ref[...] = h.astype(BF16)

    acc = jnp.dot(h_ref[...], w_ref[...], preferred_element_type=F32)
    if relu2:
        acc = jnp.square(jnp.maximum(acc, 0.0))
    o_ref[...] = acc.astype(o_ref.dtype)


def _norm_matmul(x, gain, mod, w, *, shift_row, scale_row, relu2, out_dtype, tm, tn, name):
    s, d = x.shape
    n = w.shape[1]
    return pl.pallas_call(
        functools.partial(_norm_mm_kernel, shift_row=shift_row, scale_row=scale_row, relu2=relu2),
        grid=(s // tm, n // tn),
        in_specs=[pl.BlockSpec((tm, d), lambda i, j: (i, 0)),
                  pl.BlockSpec((1, d), lambda i, j: (0, 0)),
                  pl.BlockSpec((N_MOD, d), lambda i, j: (0, 0)),
                  pl.BlockSpec((d, tn), lambda i, j: (0, j), pipeline_mode=pl.Buffered(1 if tn == n else 2))],
        out_specs=pl.BlockSpec((tm, tn), lambda i, j: (i, j)),
        out_shape=jax.ShapeDtypeStruct((s, n), out_dtype),
        scratch_shapes=[pltpu.VMEM((tm, d), BF16)],
        compiler_params=_params(("arbitrary", "arbitrary")),
        name=name,
    )(x, gain.reshape(1, d), mod, w)


def _mm_res_kernel(*refs, n_in, gate_row, final_norm):
    a_refs, w_refs = refs[:n_in], refs[n_in:2 * n_in]
    x_ref, mod_ref = refs[2 * n_in:2 * n_in + 2]
    o_ref = refs[-1]
    acc = jnp.dot(a_refs[0][...], w_refs[0][...], preferred_element_type=F32)
    for a_ref, w_ref in zip(a_refs[1:], w_refs[1:]):
        acc = acc + jnp.dot(a_ref[...], w_ref[...], preferred_element_type=F32)
    y = x_ref[...] + mod_ref[gate_row:gate_row + 1, :] * acc
    if final_norm:
        y = _rms(y, refs[-2][...])
    o_ref[...] = y


def _matmul_res(a_list, w_list, x, mod, *, gate_row, tm, tn, name, final_gain=None):
    s, d = x.shape
    n_in = len(a_list)
    assert final_gain is None or tn == d
    w_mode = pl.Buffered(1 if tn == d else 2)
    in_specs = ([pl.BlockSpec((tm, a.shape[1]), lambda i, j: (i, 0)) for a in a_list]
                + [pl.BlockSpec((w.shape[0], tn), lambda i, j: (0, j), pipeline_mode=w_mode) for w in w_list]
                + [pl.BlockSpec((tm, tn), lambda i, j: (i, j)),
                   pl.BlockSpec((N_MOD, tn), lambda i, j: (0, j))])
    args = [*a_list, *w_list, x, mod]
    if final_gain is not None:
        in_specs.append(pl.BlockSpec((1, d), lambda i, j: (0, 0)))
        args.append(final_gain.reshape(1, d))
    return pl.pallas_call(
        functools.partial(_mm_res_kernel, n_in=n_in, gate_row=gate_row, final_norm=final_gain is not None),
        grid=(s // tm, d // tn),
        in_specs=in_specs,
        out_specs=pl.BlockSpec((tm, tn), lambda i, j: (i, j)),
        out_shape=jax.ShapeDtypeStruct((s, d), F32),
        compiler_params=_params(("arbitrary", "arbitrary")),
        name=name,
    )(*args)


def _rope(x, cos, sin_lo, sin_hi):
    return (x * cos + pltpu.roll(x, LANES - 32, 1) * sin_lo + pltpu.roll(x, 32, 1) * sin_hi)


def _rms(x, gain):
    return x * lax.rsqrt(jnp.mean(x * x, axis=-1, keepdims=True) + EPS) * gain


def _store_val_t(ref, h, v):
    v_t = v.T.astype(BF16)
    n_blk, tv = ref.shape[1], ref.shape[3]
    for j in range(n_blk):
        ref[h, j] = v_t[:, j * tv:(j + 1) * tv]


def _prep_kernel(qkv_ref, ax_ref, sq_ref, aqn_ref, akn_ref, bqn_ref, bkvn_ref, wuq_ref, wukv_ref,
                 qa_ref, ka_ref, va_ref, qb_ref, kb_ref, vb_ref, qc_ref, kc_ref, vc_ref,
                 qd_ref, kd_ref, vd_ref):
    hd = HEAD_DIM
    ax_cos, ax_lo, ax_hi = ax_ref[0], ax_ref[1], ax_ref[2]
    sq_cos, sq_lo, sq_hi = sq_ref[0], sq_ref[1], sq_ref[2]

    def col(start, width=hd):
        return qkv_ref[:, start:start + width]

    sc_a = HEAD_DIM ** -0.5 * LOG2E
    for h in range(A_HEADS):
        q = _rope(_rms(col(COL_A_Q + h * hd), aqn_ref[...]), ax_cos, ax_lo, ax_hi)
        qa_ref[h * hd:(h + 1) * hd, :] = (q * sc_a).T.astype(BF16)
    for h in range(A_KV_HEADS):
        k = _rope(_rms(col(COL_A_K + h * hd), akn_ref[...]), ax_cos, ax_lo, ax_hi)
        ka_ref[:, h * hd:(h + 1) * hd] = k.astype(BF16)
        _store_val_t(va_ref, h, col(COL_A_V + h * hd))

    sc_b = (B_NOPE + B_ROPE) ** -0.5 * LOG2E
    cq = _rms(col(COL_B_CQ, B_Q_RANK), bqn_ref[...]).astype(BF16)
    qb = jnp.dot(cq, wuq_ref[...], preferred_element_type=F32)
    ckv = _rms(col(COL_B_CKV, B_KV_RANK), bkvn_ref[...]).astype(BF16)
    kvb = jnp.dot(ckv, wukv_ref[...], preferred_element_type=F32)
    kr = _rope(col(COL_B_KR, LANES), sq_cos, sq_lo, sq_hi).astype(BF16)
    for h in range(B_HEADS):
        base = h * B_QK_PAD
        qb_ref[base:base + hd, :] = (qb[:, base:base + hd] * sc_b).T.astype(BF16)
        q_r = _rope(qb[:, base + hd:base + 2 * hd], sq_cos, sq_lo, sq_hi)
        qb_ref[base + hd:base + 2 * hd, :] = (q_r * sc_b).T.astype(BF16)
        kb_ref[:, base:base + hd] = kvb[:, base:base + hd].astype(BF16)
        kb_ref[:, base + hd:base + 2 * hd] = kr
        _store_val_t(vb_ref, h, kvb[:, base + hd:base + 2 * hd])

    sc_c = HEAD_DIM ** -0.5 * LOG2E
    qc_ref[...] = (col(COL_C_Q, C_HEADS * hd) * sc_c).astype(BF16)
    kc_ref[...] = col(COL_C_K, C_KV_HEADS * hd).astype(BF16)
    vc_ref[...] = col(COL_C_V, C_KV_HEADS * hd).astype(BF16)

    sc_d = D_HALF ** -0.5 * LOG2E
    lane = lax.broadcasted_iota(jnp.int32, (qkv_ref.shape[0], hd), 1)
    lo = lane < D_HALF
    for h in range(D_HEADS):
        q = col(COL_D_Q + h * hd) * sc_d
        qd_ref[2 * h * hd:(2 * h + 1) * hd, :] = jnp.where(lo, q, 0.0).T.astype(BF16)
        qd_ref[(2 * h + 1) * hd:(2 * h + 2) * hd, :] = jnp.where(lo, 0.0, q).T.astype(BF16)
        _store_val_t(vd_ref, h, col(COL_D_V + h * hd))
    kd_ref[...] = col(COL_D_K, D_HEADS * hd).astype(BF16)


def _prep(qkv, ax_tab, sq_tab, aqn, akn, bqn, bkvn, wuq, wukv, tm=512):
    s = qkv.shape[0]
    hd = HEAD_DIM
    row = lambda i: (i, 0)
    full = lambda i: (0, 0)

    def row_major(w):
        return pl.BlockSpec((tm, w), row), jax.ShapeDtypeStruct((s, w), BF16)

    def feat_major(w):
        return pl.BlockSpec((w, tm), lambda i: (0, i)), jax.ShapeDtypeStruct((w, s), BF16)

    def val_t(h):
        return (pl.BlockSpec((h, tm // TV, hd, TV), lambda i: (0, i, 0, 0)),
                jax.ShapeDtypeStruct((h, s // TV, hd, TV), BF16))

    outs = [feat_major(A_HEADS * hd), row_major(A_KV_HEADS * hd), val_t(A_KV_HEADS),
            feat_major(B_HEADS * B_QK_PAD), row_major(B_HEADS * B_QK_PAD), val_t(B_HEADS),
            row_major(C_HEADS * hd), row_major(C_KV_HEADS * hd), row_major(C_KV_HEADS * hd),
            feat_major(2 * D_HEADS * hd), row_major(D_HEADS * hd), val_t(D_HEADS)]
    return pl.pallas_call(
        _prep_kernel,
        grid=(s // tm,),
        in_specs=[pl.BlockSpec((tm, IN_COLS_PAD), row),
                  pl.BlockSpec((3, tm, LANES), lambda i: (0, i, 0)),
                  pl.BlockSpec((3, tm, LANES), lambda i: (0, i, 0)),
                  pl.BlockSpec((1, hd), full), pl.BlockSpec((1, hd), full),
                  pl.BlockSpec((1, B_Q_RANK), full), pl.BlockSpec((1, B_KV_RANK), full),
                  pl.BlockSpec(wuq.shape, full), pl.BlockSpec(wukv.shape, full)],
        out_specs=[spec for spec, _ in outs],
        out_shape=[shape for _, shape in outs],
        compiler_params=_params(("arbitrary",)),
        name="prep",
    )(qkv, ax_tab, sq_tab, aqn.reshape(1, hd), akn.reshape(1, hd),
      bqn.reshape(1, B_Q_RANK), bkvn.reshape(1, B_KV_RANK), wuq, wukv)


def _flash_kernel(*refs, groups_stacked, dk, tq, tk, tv, seq, unroll, diff):
    if diff:
        q_ref, k_ref, v_ref, bias_ref, far_ref, scal_ref, gain_ref, o_ref = refs[:8]
    else:
        q_ref, k_ref, v_ref, o_ref = refs[:4]
    s_scr, acc_scr, p_scr = refs[-5:-3], refs[-3], refs[-2:]
    r_stack = groups_stacked
    cols = r_stack * tq
    i = pl.program_id(1)
    n_chunks = seq // tk
    n_sub = tk // tv
    q0 = i * tq
    q_t = jnp.concatenate([q_ref[r * dk:(r + 1) * dk, :] for r in range(r_stack)], axis=1)

    def raw_scores(c, far=False, n_keys=tk):
        start = c * tk
        if not isinstance(start, int):
            start = pl.multiple_of(start, tk)
        s = jnp.dot(k_ref[pl.ds(start, n_keys), :], q_t, preferred_element_type=F32)
        if not diff:
            return [(slice(0, cols), s)], 0.0
        if far:
            g = pl.program_id(0)
            return [(slice(0, cols), s)], jnp.where((c + 1) * tk <= q0, far_ref[g, 0], far_ref[g, 1])
        n_near = tk // tq + 2
        off = c * tk - q0
        idx = jnp.where(off < -tk, n_near, jnp.where(off > tq, n_near + 1, (off + tk) // tq))
        tile = bias_ref[0, idx, 0:n_keys, :]
        return [(slice(r * tq, (r + 1) * tq), s[:, r * tq:(r + 1) * tq] + tile) for r in range(r_stack)], 0.0

    def finalize(acc_t, l):
        o_t = acc_t / l
        o = [o_t[:, r * tq:(r + 1) * tq].T for r in range(r_stack)]
        if not diff:
            for r in range(r_stack):
                o_ref[:, r * HEAD_DIM:(r + 1) * HEAD_DIM] = o[r].astype(o_ref.dtype)
            return
        lam = scal_ref[0]
        od = o[0] - lam * o[1]
        y = od * lax.rsqrt(jnp.mean(od * od, axis=-1, keepdims=True) + EPS) * gain_ref[...]
        o_ref[...] = (y * scal_ref[1]).astype(o_ref.dtype)

    def values(c, p_ref, acc):
        for j in range(n_sub):
            acc = acc + jnp.dot(v_ref[0, c * n_sub + j], p_ref[j * tv:(j + 1) * tv, :], preferred_element_type=F32)
        return acc

    n_groups = n_chunks // unroll

    def scores_probs(c, slot, m_ref_row, far):
        pieces, offset = raw_scores(c, far)
        ref_row = m_ref_row - offset
        m_parts, l_parts = [], []
        for cs, s_piece in pieces:
            e = jnp.exp2(s_piece - ref_row[:, cs])
            p_scr[slot][:, cs] = e.astype(BF16)
            m_parts.append(jnp.max(s_piece, axis=0, keepdims=True))
            l_parts.append(jnp.sum(e, axis=0, keepdims=True))
        return jnp.concatenate(m_parts, axis=1) + offset, jnp.concatenate(l_parts, axis=1)

    first_tiled = jnp.clip(q0 // tk - 1, 0, n_chunks - N_TILED)

    def is_far(pos):
        return diff and not (isinstance(pos, int) and pos < N_TILED)

    def chunk_at(pos):
        if not diff:
            return pos
        if not is_far(pos):
            return first_tiled + pos
        j = pos - N_TILED
        return j + jnp.where(j >= first_tiled, N_TILED, 0)

    def fast_step(pos, par, state, do_scores):
        m_run, beta, l, jump = state
        c = chunk_at(pos)
        if do_scores:
            m_chunk, l_chunk = scores_probs(chunk_at(pos + 1), 1 - par, m_run, is_far(pos + 1))
        acc_scr[...] = beta * values(c, p_scr[par], acc_scr[...])
        if not do_scores:
            return state
        m_new = jnp.maximum(m_run, m_chunk)
        beta_new = jnp.exp2(m_run - m_new)
        return m_new, beta_new, beta_new * (l + l_chunk), jnp.maximum(jump, m_chunk - m_run)

    def fast_group(base, state, last):
        for u in range(unroll):
            state = fast_step(base + u, u % 2, state, not (last and u == unroll - 1))
        return state

    m_first, l_first = scores_probs(chunk_at(0), 0, jnp.zeros((1, cols), F32), False)
    beta_first = jnp.exp2(-m_first)
    acc_scr[...] = jnp.zeros((HEAD_DIM, cols), F32)
    state = (m_first, beta_first, beta_first * l_first, jnp.abs(m_first))
    first_loop_group = 0
    if diff and n_groups > 1:
        state = fast_group(0, state, False)
        first_loop_group = 1
    if n_groups - 1 > first_loop_group:
        state = lax.fori_loop(first_loop_group, n_groups - 1, lambda t, st: fast_group(t * unroll, st, False), state)
    _, _, l_fast, jump = fast_group((n_groups - 1) * unroll, state, True)
    finalize(acc_scr[...], l_fast)

    def scores(c, slot):
        m_parts = []
        for cs, s_piece in raw_scores(c)[0]:
            s_scr[slot][:, cs] = s_piece
            m_parts.append(jnp.max(s_piece, axis=0, keepdims=True))
        return jnp.concatenate(m_parts, axis=1)

    def step(c, par, m_chunk, state, do_scores):
        m, l = state
        m_new = jnp.maximum(m, m_chunk)
        alpha = jnp.exp2(m - m_new)
        l = alpha * l
        m_next = scores(c + 1, 1 - par) if do_scores else None
        acc = alpha * acc_scr[...]
        for j in range(n_sub):
            p = jnp.exp2(s_scr[par][j * tv:(j + 1) * tv, :] - m_new)
            l = l + jnp.sum(p, axis=0, keepdims=True)
            acc = acc + jnp.dot(v_ref[0, c * n_sub + j], p.astype(BF16), preferred_element_type=F32)
        acc_scr[...] = acc
        return m_next, (m_new, l)

    def group(base, carry, last):
        m_chunk, state = carry
        for u in range(unroll):
            m_chunk, state = step(base + u, u % 2, m_chunk, state, not (last and u == unroll - 1))
        return m_chunk, state

    @pl.when(jnp.max(jump) > LAZY_MAX_JUMP)
    def _():
        acc_scr[...] = jnp.zeros((HEAD_DIM, cols), F32)
        carry = (scores(0, 0), (jnp.full((1, cols), NEG_BIG, F32), jnp.zeros((1, cols), F32)))
        if n_groups > 1:
            carry = lax.fori_loop(0, n_groups - 1, lambda t, cr: group(t * unroll, cr, False), carry)
        _, (_, l_exact) = group((n_groups - 1) * unroll, carry, True)
        finalize(acc_scr[...], l_exact)


def _flash(q_t, k, v_t, *, groups, groups_stacked, dk, tq, tk, name, diff_args=None):
    s = k.shape[0]
    n_vblk, tv = v_t.shape[1], v_t.shape[3]
    unroll = min(FLASH_KEYS_PER_TRIP, s) // tk
    assert unroll % 2 == 0 and (s // tk) % unroll == 0 and tk % tv == 0
    diff = diff_args is not None
    kern = functools.partial(_flash_kernel, groups_stacked=groups_stacked, dk=dk, tq=tq, tk=tk, tv=tv,
                             seq=s, unroll=unroll, diff=diff)
    in_specs = [pl.BlockSpec((groups_stacked * dk, tq), lambda g, i: (g, i)),
                pl.BlockSpec((s, dk), lambda g, i: (0, g)),
                pl.BlockSpec((1, n_vblk, HEAD_DIM, tv), lambda g, i: (g, 0, 0, 0))]
    args = [q_t, k, v_t]
    if diff:
        bias, far, scal, gain = diff_args
        in_specs += [pl.BlockSpec((1,) + bias.shape[1:], lambda g, i: (g, 0, 0, 0),
                                  pipeline_mode=pl.Buffered(1)),
                     pl.BlockSpec(memory_space=pltpu.SMEM),
                     pl.BlockSpec(memory_space=pltpu.SMEM),
                     pl.BlockSpec((1, HEAD_DIM), lambda g, i: (0, 0))]
        args += [bias, far, scal, gain.reshape(1, HEAD_DIM)]
        out_w = HEAD_DIM
    else:
        out_w = groups_stacked * HEAD_DIM
    return pl.pallas_call(
        kern,
        grid=(groups, s // tq),
        in_specs=in_specs,
        out_specs=pl.BlockSpec((tq, out_w), lambda g, i: (i, g)),
        out_shape=jax.ShapeDtypeStruct((s, groups * out_w), BF16),
        scratch_shapes=([pltpu.VMEM((tk, groups_stacked * tq), F32)] * 2
                        + [pltpu.VMEM((HEAD_DIM, groups_stacked * tq), F32)]
                        + [pltpu.VMEM((tk, groups_stacked * tq), BF16)] * 2),
        compiler_params=_params(("arbitrary", "arbitrary")),
        name=name,
    )(*args)


def _window_kernel(q_ref, kp_ref, kc_ref, kn_ref, vp_ref, vc_ref, vn_ref, bias_ref, sink_ref, o_ref,
                   kbuf, vbuf, *, tw, seq):
    w = WINDOW
    h = pl.program_id(0)
    i = pl.program_id(1)
    kbuf[0:w] = kp_ref[tw - w:tw]
    kbuf[w:w + tw] = kc_ref[...]
    kbuf[w + tw:w + tw + w] = kn_ref[0:w]
    vbuf[0:w] = vp_ref[tw - w:tw]
    vbuf[w:w + tw] = vc_ref[...]
    vbuf[w + tw:w + tw + w] = vn_ref[0:w]
    sink = sink_ref[h] * LOG2E
    nq = tw // w
    q3 = q_ref[...].reshape(nq, w, HEAD_DIM)
    k3 = jnp.stack([kbuf[j * w:(j + 3) * w] for j in range(nq)])
    v3 = jnp.stack([vbuf[j * w:(j + 3) * w] for j in range(nq)])
    s = jnp.einsum("jqd,jkd->jqk", q3, k3, preferred_element_type=F32) + bias_ref[...]
    kpos = (i * tw - w + w * lax.broadcasted_iota(jnp.int32, (nq, w, 3 * w), 0)
            + lax.broadcasted_iota(jnp.int32, (nq, w, 3 * w), 2))
    s = jnp.where((kpos >= 0) & (kpos < seq), s, NEG_BIG)
    m = jnp.maximum(jnp.max(s, axis=-1, keepdims=True), sink)
    p = jnp.exp2(s - m)
    denom = jnp.sum(p, axis=-1, keepdims=True) + jnp.exp2(sink - m)
    o = jnp.einsum("jqk,jkd->jqd", p.astype(BF16), v3, preferred_element_type=F32) / denom
    o_ref[...] = o.reshape(tw, HEAD_DIM).astype(o_ref.dtype)


def _window(q, k, v, bias, sink, tw=1024):
    s = q.shape[0]
    hd = HEAD_DIM
    nb = s // tw
    grp = C_HEADS // C_KV_HEADS
    prev = lambda h, i: (jnp.maximum(i - 1, 0), h // grp)
    cur = lambda h, i: (i, h // grp)
    nxt = lambda h, i: (jnp.minimum(i + 1, nb - 1), h // grp)
    kv_specs = [pl.BlockSpec((tw, hd), f) for f in (prev, cur, nxt)]
    return pl.pallas_call(
        functools.partial(_window_kernel, tw=tw, seq=s),
        grid=(C_HEADS, nb),
        in_specs=[pl.BlockSpec((tw, hd), lambda h, i: (i, h))] + kv_specs + kv_specs
                 + [pl.BlockSpec((1, WINDOW, 3 * WINDOW), lambda h, i: (h, 0, 0)),
                    pl.BlockSpec(memory_space=pltpu.SMEM)],
        out_specs=pl.BlockSpec((tw, hd), lambda h, i: (i, h)),
        out_shape=jax.ShapeDtypeStruct((s, C_HEADS * hd), BF16),
        scratch_shapes=[pltpu.VMEM((tw + 2 * WINDOW, hd), BF16),
                        pltpu.VMEM((tw + 2 * WINDOW, hd), BF16)],
        compiler_params=_params(("arbitrary", "arbitrary")),
        name="window_attn",
    )(q, k, k, k, v, v, v, bias, sink)


def _final_norm_kernel(x_ref, g_ref, o_ref):
    o_ref[...] = _rms(x_ref[...], g_ref[...])


def _final_norm(x, gain, tm=512):
    s, d = x.shape
    return pl.pallas_call(
        _final_norm_kernel,
        grid=(s // tm,),
        in_specs=[pl.BlockSpec((tm, d), lambda i: (i, 0)), pl.BlockSpec((1, d), lambda i: (0, 0))],
        out_specs=pl.BlockSpec((tm, d), lambda i: (i, 0)),
        out_shape=jax.ShapeDtypeStruct((s, d), F32),
        compiler_params=_params(("arbitrary",)),
        name="final_norm",
    )(x, gain.reshape(1, d))


def _rope_tables(pos, dim):
    inv = ROPE_THETA ** (-jnp.arange(0, dim, 2, dtype=F32) / dim)
    ang = pos.astype(F32)[:, None] * inv[None, :]
    ang = jnp.concatenate([ang, ang], axis=-1)
    return jnp.cos(ang), jnp.sin(ang)


def _rope_pack(cos, sin, half):
    width = cos.shape[1]
    lane = jnp.arange(width)
    first = (lane % (2 * half)) < half
    tabs = [cos, jnp.where(first, -sin, 0.0), jnp.where(first, 0.0, sin)]
    return jnp.stack([jnp.pad(t, ((0, 0), (0, LANES - width))) for t in tabs])


def _t5_bucket(rel):
    half = N_BUCKETS // 2
    max_exact = half // 2
    n = jnp.abs(rel)
    large = max_exact + (jnp.log(jnp.maximum(n, 1).astype(F32) / max_exact)
                         / math.log(MAX_DISTANCE / max_exact) * (half - max_exact)).astype(jnp.int32)
    large = jnp.minimum(large, half - 1)
    return jnp.where(rel > 0, half, 0) + jnp.where(n < max_exact, n, large)


def _toeplitz(fn, rows, cols, offset):
    n = rows + cols + 1
    vec = fn(jnp.arange(n) - rows + offset)
    flat = jnp.tile(vec, (1, rows))[:, :rows * (n - 1)]
    return flat.reshape(vec.shape[0], rows, n - 1)[:, :, rows:rows + cols]


TQ_A, TQ_B, TQ_D = 256, 512, 256
TK_DENSE = 1024
FLASH_KEYS_PER_TRIP = 4096


def kernel(x, c, w_ada, b_ada, norm1, w_in, a_q_norm, a_k_norm, b_q_norm, b_kv_norm, b_w_uq, b_w_ukv,
           c_sink, d_lambda, d_sub_norm, w_out, norm2, w_ff1, w_ff2, rel_bias, final_norm):
    _, s, d = x.shape
    depth = w_ada.shape[0]
    hd = HEAD_DIM
    x2 = x.reshape(s, d)

    t = jnp.arange(s)
    grid_h = s // GRID_W
    cos_r, sin_r = (jnp.repeat(tab, GRID_W, axis=0) for tab in _rope_tables(jnp.arange(grid_h), hd // 2))
    cos_c, sin_c = (jnp.tile(tab, (grid_h, 1)) for tab in _rope_tables(jnp.arange(GRID_W), hd // 2))
    ax_tab = _rope_pack(jnp.concatenate([cos_r, cos_c], -1), jnp.concatenate([sin_r, sin_c], -1), hd // 4)
    sq_tab = _rope_pack(*_rope_tables(t, B_ROPE), B_ROPE // 2)

    bias_fn = lambda heads: (lambda rel: (rel_bias[_t5_bucket(rel)][:, heads] * LOG2E).T)
    c_heads = jnp.arange(C_HEADS)
    d_heads = C_HEADS + jnp.arange(D_HEADS)
    rel_c = jnp.arange(3 * WINDOW)[None, :] - WINDOW - jnp.arange(WINDOW)[:, None]
    bias_c = jnp.where(jnp.abs(rel_c)[None] <= WINDOW, _toeplitz(bias_fn(c_heads), WINDOW, 3 * WINDOW, -WINDOW),
                       NEG_BIG)
    n_off = TK_DENSE // TQ_D + 2
    bias_d = jnp.stack([_toeplitz(bias_fn(d_heads), TQ_D, TK_DENSE, o * TQ_D - TK_DENSE)
                        for o in range(n_off)], axis=1)
    far_d = bias_fn(d_heads)(jnp.array([-MAX_DISTANCE, MAX_DISTANCE]))
    bias_d = jnp.concatenate(
        [bias_d, jnp.broadcast_to(far_d[:, :, None, None], (D_HEADS, 2, TQ_D, TK_DENSE))], axis=1)
    bias_d = jnp.swapaxes(bias_d, 2, 3)

    mod_all = _ada_mod(c, w_ada, b_ada)

    for l in range(depth):
        mod = mod_all[l]
        lam_init = 0.8 - 0.6 * math.exp(-0.3 * l)
        lf = d_lambda[l].astype(F32)
        lam = jnp.exp(jnp.sum(lf[0] * lf[1])) - jnp.exp(jnp.sum(lf[2] * lf[3])) + lam_init
        scal = jnp.stack([lam, jnp.asarray(1.0 - lam_init, F32)])

        w_in_l = w_in[l]
        w_in_p = jnp.concatenate([w_in_l[:, :COL_B_KR + B_ROPE], jnp.zeros((d, B_KR_PAD), F32),
                                  w_in_l[:, COL_B_KR + B_ROPE:]], axis=1).astype(BF16)
        wuq = jnp.pad(b_w_uq[l].reshape(B_Q_RANK, B_HEADS, B_NOPE + B_ROPE),
                      ((0, 0), (0, 0), (0, B_QK_PAD - B_NOPE - B_ROPE))).reshape(B_Q_RANK, -1).astype(BF16)
        wukv = b_w_ukv[l].astype(BF16)

        qkv = _norm_matmul(x2, norm1[l], mod, w_in_p, shift_row=0, scale_row=1, relu2=False,
                           out_dtype=F32, tm=512, tn=IN_COLS_PAD, name="in_proj")
        (qa, ka, va, qb, kb, vb, qc, kc, vc, qd, kd, vd) = _prep(
            qkv, ax_tab, sq_tab, a_q_norm[l], a_k_norm[l], b_q_norm[l], b_kv_norm[l], wuq, wukv)

        oa = _flash(qa, ka, va, groups=A_KV_HEADS, groups_stacked=A_HEADS // A_KV_HEADS, dk=hd,
                    tq=TQ_A, tk=TK_DENSE, name="flash_a")
        ob = _flash(qb, kb, vb, groups=B_HEADS, groups_stacked=1, dk=B_QK_PAD,
                    tq=TQ_B, tk=TK_DENSE, name="flash_b")
        oc = _window(qc, kc, vc, bias_c, c_sink[l])
        od = _flash(qd, kd, vd, groups=D_HEADS, groups_stacked=2, dk=hd, tq=TQ_D, tk=TK_DENSE,
                    name="flash_d", diff_args=(bias_d, far_d, scal, d_sub_norm[l]))

        w_out_l = w_out[l].astype(BF16)
        heads = [oa, ob, oc, od]
        w_parts = [w_out_l[k * 4 * hd:(k + 1) * 4 * hd] for k in range(4)]
        x2 = _matmul_res(heads, w_parts, x2, mod, gate_row=2, tm=512, tn=d, name="out_proj")

        hid = _norm_matmul(x2, norm2[l], mod, w_ff1[l].astype(BF16), shift_row=3, scale_row=4, relu2=True,
                           out_dtype=BF16, tm=256, tn=w_ff1.shape[2], name="ff1")
        x2 = _matmul_res([hid], [w_ff2[l].astype(BF16)], x2, mod, gate_row=5, tm=256, tn=d, name="ff2",
                         final_gain=final_norm if l == depth - 1 else None)

    return x2.reshape(1, s, d)
```

```python
import functools
import math

import jax
import jax.numpy as jnp
from jax import lax
from jax.experimental import pallas as pl
from jax.experimental.pallas import tpu as pltpu

F32 = jnp.float32
BF16 = jnp.bfloat16

HEAD_DIM = 128
GRID_W = 64
ROPE_THETA = 10000.0
EPS = 1e-6
A_HEADS, A_KV_HEADS = 4, 2
B_HEADS, B_NOPE, B_ROPE, B_Q_RANK, B_KV_RANK = 4, 128, 64, 384, 256
C_HEADS, C_KV_HEADS, WINDOW = 4, 2, 128
D_HEADS = 4
D_HALF = HEAD_DIM // 2
N_BUCKETS, MAX_DISTANCE = 32, 128
N_MOD = 6

LOG2E = math.log2(math.e)
NEG_BIG = -1e30
LAZY_MAX_JUMP = 60.0

LANES = 128
MXU_DIM = 256
VMEM_LIMIT = 56 * 1024 * 1024

TV = MXU_DIM

B_KR_PAD = LANES - B_ROPE
COL_A_Q = 0
COL_A_K = COL_A_Q + A_HEADS * HEAD_DIM
COL_A_V = COL_A_K + A_KV_HEADS * HEAD_DIM
COL_B_CQ = COL_A_V + A_KV_HEADS * HEAD_DIM
COL_B_CKV = COL_B_CQ + B_Q_RANK
COL_B_KR = COL_B_CKV + B_KV_RANK
COL_C_Q = COL_B_KR + B_ROPE + B_KR_PAD
COL_C_K = COL_C_Q + C_HEADS * HEAD_DIM
COL_C_V = COL_C_K + C_KV_HEADS * HEAD_DIM
COL_D_Q = COL_C_V + C_KV_HEADS * HEAD_DIM
COL_D_K = COL_D_Q + D_HEADS * HEAD_DIM
COL_D_V = COL_D_K + D_HEADS * HEAD_DIM
IN_COLS_PAD = COL_D_V + D_HEADS * HEAD_DIM
B_QK_PAD = 2 * LANES


def _params(sem):
    return pltpu.CompilerParams(dimension_semantics=sem, vmem_limit_bytes=VMEM_LIMIT)


def _ada_kernel(c_ref, w_ref, b_ref, o_ref):
    c = c_ref[...]
    cs = c * (1.0 / (1.0 + jnp.exp(-c)))
    cs8 = jnp.broadcast_to(cs, (8, cs.shape[1])).astype(BF16)
    r = jnp.dot(cs8, w_ref[0].astype(BF16), preferred_element_type=F32)
    o_ref[0] = r[0:1] + b_ref[0]


def _ada_mod(c, w_ada, b_ada, tn=1024):
    depth, d, n = w_ada.shape
    out = pl.pallas_call(
        _ada_kernel,
        grid=(depth, n // tn),
        in_specs=[pl.BlockSpec((1, d), lambda l, j: (0, 0)),
                  pl.BlockSpec((1, d, tn), lambda l, j: (l, 0, j)),
                  pl.BlockSpec((1, 1, tn), lambda l, j: (l, 0, j))],
        out_specs=pl.BlockSpec((1, 1, tn), lambda l, j: (l, 0, j)),
        out_shape=jax.ShapeDtypeStruct((depth, 1, n), F32),
        compiler_params=_params(("arbitrary", "arbitrary")),
        name="ada_mod",
    )(c, w_ada, b_ada.reshape(depth, 1, n))
    return out.reshape(depth, N_MOD, d)


def _norm_mm_kernel(x_ref, g_ref, mod_ref, w_ref, o_ref, h_ref, *, shift_row, scale_row, relu2):
    @pl.when(pl.program_id(1) == 0)
    def _():
        x = x_ref[...]
        y = x * lax.rsqrt(jnp.mean(x * x, axis=-1, keepdims=True) + EPS) * g_ref[...]
        h = y * (1.0 + mod_ref[scale_row:scale_row + 1, :]) + mod_ref[shift_row:shift_row + 1, :]
        h_ref[...] = h.astype(BF16)

    acc = jnp.dot(h_ref[...], w_ref[...], preferred_element_type=F32)
    if relu2:
        acc = jnp.square(jnp.maximum(acc, 0.0))
    o_ref[...] = acc.astype(o_ref.dtype)


def _norm_matmul(x, gain, mod, w, *, shift_row, scale_row, relu2, out_dtype, tm, tn, name):
    s, d = x.shape
    n = w.shape[1]
    return pl.pallas_call(
        functools.partial(_norm_mm_kernel, shift_row=shift_row, scale_row=scale_row, relu2=relu2),
        grid=(s // tm, n // tn),
        in_specs=[pl.BlockSpec((tm, d), lambda i, j: (i, 0)),
                  pl.BlockSpec((1, d), lambda i, j: (0, 0)),
                  pl.BlockSpec((N_MOD, d), lambda i, j: (0, 0)),
                  pl.BlockSpec((d, tn), lambda i, j: (0, j), pipeline_mode=pl.Buffered(1 if tn == n else 2))],
        out_specs=pl.BlockSpec((tm, tn), lambda i, j: (i, j)),
        out_shape=jax.ShapeDtypeStruct((s, n), out_dtype),
        scratch_shapes=[pltpu.VMEM((tm, d), BF16)],
        compiler_params=_params(("arbitrary", "arbitrary")),
        name=name,
    )(x, gain.reshape(1, d), mod, w)


def _mm_res_kernel(*refs, n_in, gate_row, final_norm):
    a_refs, w_refs = refs[:n_in], refs[n_in:2 * n_in]
    x_ref, mod_ref = refs[2 * n_in:2 * n_in + 2]
    o_ref = refs[-1]
    acc = jnp.dot(a_refs[0][...], w_refs[0][...], preferred_element_type=F32)
    for a_ref, w_ref in zip(a_refs[1:], w_refs[1:]):
        acc = acc + jnp.dot(a_ref[...], w_ref[...], preferred_element_type=F32)
    y = x_ref[...] + mod_ref[gate_row:gate_row + 1, :] * acc
    if final_norm:
        y = _rms(y, refs[-2][...])
    o_ref[...] = y


def _matmul_res(a_list, w_list, x, mod, *, gate_row, tm, tn, name, final_gain=None):
    s, d = x.shape
    n_in = len(a_list)
    assert final_gain is None or tn == d
    w_mode = pl.Buffered(1 if tn == d else 2)
    in_specs = ([pl.BlockSpec((tm, a.shape[1]), lambda i, j: (i, 0)) for a in a_list]
                + [pl.BlockSpec((w.shape[0], tn), lambda i, j: (0, j), pipeline_mode=w_mode) for w in w_list]
                + [pl.BlockSpec((tm, tn), lambda i, j: (i, j)),
                   pl.BlockSpec((N_MOD, tn), lambda i, j: (0, j))])
    args = [*a_list, *w_list, x, mod]
    if final_gain is not None:
        in_specs.append(pl.BlockSpec((1, d), lambda i, j: (0, 0)))
        args.append(final_gain.reshape(1, d))
    return pl.pallas_call(
        functools.partial(_mm_res_kernel, n_in=n_in, gate_row=gate_row, final_norm=final_gain is not None),
        grid=(s // tm, d // tn),
        in_specs=in_specs,
        out_specs=pl.BlockSpec((tm, tn), lambda i, j: (i, j)),
        out_shape=jax.ShapeDtypeStruct((s, d), F32),
        compiler_params=_params(("arbitrary", "arbitrary")),
        name=name,
    )(*args)


def _rope(x, cos, sin_lo, sin_hi):
    return (x * cos + pltpu.roll(x, LANES - 32, 1) * sin_lo + pltpu.roll(x, 32, 1) * sin_hi)


def _rms(x, gain):
    return x * lax.rsqrt(jnp.mean(x * x, axis=-1, keepdims=True) + EPS) * gain


def _store_val_t(ref, h, v):
    v_t = v.astype(BF16).T
    n_blk, tv = ref.shape[1], ref.shape[3]
    for j in range(n_blk):
        ref[h, j] = v_t[:, j * tv:(j + 1) * tv]


def _prep_kernel(qkv_ref, ax_ref, sq_ref, aqn_ref, akn_ref, bqn_ref, bkvn_ref, wuq_ref, wukv_ref,
                 qa_ref, ka_ref, va_ref, qb_ref, kb_ref, vb_ref, qc_ref, kc_ref, vc_ref,
                 qd_ref, kd_ref, vd_ref):
    hd = HEAD_DIM
    ax_cos, ax_lo, ax_hi = ax_ref[0], ax_ref[1], ax_ref[2]
    sq_cos, sq_lo, sq_hi = sq_ref[0], sq_ref[1], sq_ref[2]

    def col(start, width=hd):
        return qkv_ref[:, start:start + width]

    sc_a = HEAD_DIM ** -0.5 * LOG2E
    for h in range(A_HEADS):
        q = _rope(_rms(col(COL_A_Q + h * hd), aqn_ref[...]), ax_cos, ax_lo, ax_hi)
        qa_ref[h * hd:(h + 1) * hd, :] = (q * sc_a).astype(BF16).T
    for h in range(A_KV_HEADS):
        k = _rope(_rms(col(COL_A_K + h * hd), akn_ref[...]), ax_cos, ax_lo, ax_hi)
        ka_ref[:, h * hd:(h + 1) * hd] = k.astype(BF16)
        _store_val_t(va_ref, h, col(COL_A_V + h * hd))

    sc_b = (B_NOPE + B_ROPE) ** -0.5 * LOG2E
    cq = _rms(col(COL_B_CQ, B_Q_RANK), bqn_ref[...]).astype(BF16)
    qb = jnp.dot(cq, wuq_ref[...], preferred_element_type=F32)
    ckv = _rms(col(COL_B_CKV, B_KV_RANK), bkvn_ref[...]).astype(BF16)
    kvb = jnp.dot(ckv, wukv_ref[...], preferred_element_type=F32)
    kr = _rope(col(COL_B_KR, LANES), sq_cos, sq_lo, sq_hi).astype(BF16)
    for h in range(B_HEADS):
        base = h * B_QK_PAD
        qb_ref[base:base + hd, :] = (qb[:, base:base + hd] * sc_b).astype(BF16).T
        q_r = _rope(qb[:, base + hd:base + 2 * hd], sq_cos, sq_lo, sq_hi)
        qb_ref[base + hd:base + 2 * hd, :] = (q_r * sc_b).astype(BF16).T
        kb_ref[:, base:base + hd] = kvb[:, base:base + hd].astype(BF16)
        kb_ref[:, base + hd:base + 2 * hd] = kr
        _store_val_t(vb_ref, h, kvb[:, base + hd:base + 2 * hd])

    sc_c = HEAD_DIM ** -0.5 * LOG2E
    qc_ref[...] = (col(COL_C_Q, C_HEADS * hd) * sc_c).astype(BF16)
    kc_ref[...] = col(COL_C_K, C_KV_HEADS * hd).astype(BF16)
    vc_ref[...] = col(COL_C_V, C_KV_HEADS * hd).astype(BF16)

    sc_d = D_HALF ** -0.5 * LOG2E
    lane = lax.broadcasted_iota(jnp.int32, (qkv_ref.shape[0], hd), 1)
    lo = lane < D_HALF
    for h in range(D_HEADS):
        q = col(COL_D_Q + h * hd) * sc_d
        qd_ref[2 * h * hd:(2 * h + 1) * hd, :] = jnp.where(lo, q, 0.0).astype(BF16).T
        qd_ref[(2 * h + 1) * hd:(2 * h + 2) * hd, :] = jnp.where(lo, 0.0, q).astype(BF16).T
        _store_val_t(vd_ref, h, col(COL_D_V + h * hd))
    kd_ref[...] = col(COL_D_K, D_HEADS * hd).astype(BF16)


def _prep(qkv, ax_tab, sq_tab, aqn, akn, bqn, bkvn, wuq, wukv, tm=512):
    s = qkv.shape[0]
    hd = HEAD_DIM
    row = lambda i: (i, 0)
    full = lambda i: (0, 0)

    def row_major(w):
        return pl.BlockSpec((tm, w), row), jax.ShapeDtypeStruct((s, w), BF16)

    def feat_major(w):
        return pl.BlockSpec((w, tm), lambda i: (0, i)), jax.ShapeDtypeStruct((w, s), BF16)

    def val_t(h):
        return (pl.BlockSpec((h, tm // TV, hd, TV), lambda i: (0, i, 0, 0)),
                jax.ShapeDtypeStruct((h, s // TV, hd, TV), BF16))

    outs = [feat_major(A_HEADS * hd), row_major(A_KV_HEADS * hd), val_t(A_KV_HEADS),
            feat_major(B_HEADS * B_QK_PAD), row_major(B_HEADS * B_QK_PAD), val_t(B_HEADS),
            row_major(C_HEADS * hd), row_major(C_KV_HEADS * hd), row_major(C_KV_HEADS * hd),
            feat_major(2 * D_HEADS * hd), row_major(D_HEADS * hd), val_t(D_HEADS)]
    return pl.pallas_call(
        _prep_kernel,
        grid=(s // tm,),
        in_specs=[pl.BlockSpec((tm, IN_COLS_PAD), row),
                  pl.BlockSpec((3, tm, LANES), lambda i: (0, i, 0)),
                  pl.BlockSpec((3, tm, LANES), lambda i: (0, i, 0)),
                  pl.BlockSpec((1, hd), full), pl.BlockSpec((1, hd), full),
                  pl.BlockSpec((1, B_Q_RANK), full), pl.BlockSpec((1, B_KV_RANK), full),
                  pl.BlockSpec(wuq.shape, full), pl.BlockSpec(wukv.shape, full)],
        out_specs=[spec for spec, _ in outs],
        out_shape=[shape for _, shape in outs],
        compiler_params=_params(("arbitrary",)),
        name="prep",
    )(qkv, ax_tab, sq_tab, aqn.reshape(1, hd), akn.reshape(1, hd),
      bqn.reshape(1, B_Q_RANK), bkvn.reshape(1, B_KV_RANK), wuq, wukv)


def _flash_kernel(*refs, groups_stacked, dk, tq, tk, tv, seq, unroll, diff):
    if diff:
        q_ref, k_ref, v_ref, bias_ref, far_ref, scal_ref, gain_ref, o_ref = refs[:8]
    else:
        q_ref, k_ref, v_ref, o_ref = refs[:4]
    s_scr, acc_scr, p_scr = refs[-5:-3], refs[-3], refs[-2:]
    r_stack = groups_stacked
    cols = r_stack * tq
    i = pl.program_id(1)
    n_chunks = seq // tk
    n_sub = tk // tv
    q0 = i * tq
    q_t = jnp.concatenate([q_ref[r * dk:(r + 1) * dk, :] for r in range(r_stack)], axis=1)

    def raw_scores(c, far=False, n_keys=tk):
        start = c * tk
        if not isinstance(start, int):
            start = pl.multiple_of(start, tk)
        s = jnp.dot(k_ref[pl.ds(start, n_keys), :], q_t, preferred_element_type=F32)
        if not diff:
            return [(slice(0, cols), s)], 0.0
        if far:
            g = pl.program_id(0)
            return [(slice(0, cols), s)], jnp.where((c + 1) * tk <= q0, far_ref[g, 0], far_ref[g, 1])
        n_near = tk // tq + 2
        off = c * tk - q0
        idx = jnp.where(off < -tk, n_near, jnp.where(off > tq, n_near + 1, (off + tk) // tq))
        tile = bias_ref[0, idx, 0:n_keys, :]
        return [(slice(r * tq, (r + 1) * tq), s[:, r * tq:(r + 1) * tq] + tile) for r in range(r_stack)], 0.0

    def finalize(acc_t, l):
        o_t = acc_t / l
        o = [o_t[:, r * tq:(r + 1) * tq].T for r in range(r_stack)]
        if not diff:
            for r in range(r_stack):
                o_ref[:, r * HEAD_DIM:(r + 1) * HEAD_DIM] = o[r].astype(o_ref.dtype)
            return
        lam = scal_ref[0]
        od = o[0] - lam * o[1]
        y = od * lax.rsqrt(jnp.mean(od * od, axis=-1, keepdims=True) + EPS) * gain_ref[...]
        o_ref[...] = (y * scal_ref[1]).astype(o_ref.dtype)

    def values(c, p_ref, acc):
        for j in range(n_sub):
            acc = acc + jnp.dot(v_ref[0, c * n_sub + j], p_ref[j * tv:(j + 1) * tv, :], preferred_element_type=F32)
        return acc

    n_groups = n_chunks // unroll

    def scores_probs(c, slot, m_ref_row, far):
        pieces, offset = raw_scores(c, far)
        ref_row = m_ref_row - offset
        m_parts, l_parts = [], []
        for cs, s_piece in pieces:
            e = jnp.exp2(s_piece - ref_row[:, cs])
            p_scr[slot][:, cs] = e.astype(BF16)
            m_parts.append(jnp.max(s_piece, axis=0, keepdims=True))
            l_parts.append(jnp.sum(e, axis=0, keepdims=True))
        return jnp.concatenate(m_parts, axis=1) + offset, jnp.concatenate(l_parts, axis=1)

    n_tiled = pl.cdiv(tq + 2 * MAX_DISTANCE - 1, tk) + 1
    first_tiled = jnp.clip((q0 - MAX_DISTANCE) // tk, 0, n_chunks - n_tiled)

    def is_far(pos):
        return diff and not (isinstance(pos, int) and pos < n_tiled)

    def chunk_at(pos):
        if not diff:
            return pos
        if not is_far(pos):
            return first_tiled + pos
        j = pos - n_tiled
        return j + jnp.where(j >= first_tiled, n_tiled, 0)

    def fast_step(pos, par, state, do_scores):
        m_run, beta, l, jump = state
        c = chunk_at(pos)
        if do_scores:
            m_chunk, l_chunk = scores_probs(chunk_at(pos + 1), 1 - par, m_run, is_far(pos + 1))
        acc_scr[...] = beta * values(c, p_scr[par], acc_scr[...])
        if not do_scores:
            return state
        m_new = jnp.maximum(m_run, m_chunk)
        beta_new = jnp.exp2(m_run - m_new)
        return m_new, beta_new, beta_new * (l + l_chunk), jnp.maximum(jump, m_chunk - m_run)

    def fast_group(base, state, last):
        for u in range(unroll):
            state = fast_step(base + u, u % 2, state, not (last and u == unroll - 1))
        return state

    m_first, l_first = scores_probs(chunk_at(0), 0, jnp.zeros((1, cols), F32), False)
    beta_first = jnp.exp2(-m_first)
    acc_scr[...] = jnp.zeros((HEAD_DIM, cols), F32)
    state = (m_first, beta_first, beta_first * l_first, jnp.abs(m_first))
    first_loop_group = 0
    if diff and n_groups > 1:
        state = fast_group(0, state, False)
        first_loop_group = 1
    if n_groups - 1 > first_loop_group:
        state = lax.fori_loop(first_loop_group, n_groups - 1, lambda t, st: fast_group(t * unroll, st, False), state)
    _, _, l_fast, jump = fast_group((n_groups - 1) * unroll, state, True)
    finalize(acc_scr[...], l_fast)

    def scores(c, slot):
        m_parts = []
        for cs, s_piece in raw_scores(c)[0]:
            s_scr[slot][:, cs] = s_piece
            m_parts.append(jnp.max(s_piece, axis=0, keepdims=True))
        return jnp.concatenate(m_parts, axis=1)

    def step(c, par, m_chunk, state, do_scores):
        m, l = state
        m_new = jnp.maximum(m, m_chunk)
        alpha = jnp.exp2(m - m_new)
        l = alpha * l
        m_next = scores(c + 1, 1 - par) if do_scores else None
        acc = alpha * acc_scr[...]
        for j in range(n_sub):
            p = jnp.exp2(s_scr[par][j * tv:(j + 1) * tv, :] - m_new)
            l = l + jnp.sum(p, axis=0, keepdims=True)
            acc = acc + jnp.dot(v_ref[0, c * n_sub + j], p.astype(BF16), preferred_element_type=F32)
        acc_scr[...] = acc
        return m_next, (m_new, l)

    def group(base, carry, last):
        m_chunk, state = carry
        for u in range(unroll):
            m_chunk, state = step(base + u, u % 2, m_chunk, state, not (last and u == unroll - 1))
        return m_chunk, state

    @pl.when(jnp.max(jump) > LAZY_MAX_JUMP)
    def _():
        acc_scr[...] = jnp.zeros((HEAD_DIM, cols), F32)
        carry = (scores(0, 0), (jnp.full((1, cols), NEG_BIG, F32), jnp.zeros((1, cols), F32)))
        if n_groups > 1:
            carry = lax.fori_loop(0, n_groups - 1, lambda t, cr: group(t * unroll, cr, False), carry)
        _, (_, l_exact) = group((n_groups - 1) * unroll, carry, True)
        finalize(acc_scr[...], l_exact)


def _flash(q_t, k, v_t, *, groups, groups_stacked, dk, tq, tk, name, diff_args=None):
    s = k.shape[0]
    n_vblk, tv = v_t.shape[1], v_t.shape[3]
    unroll = min(FLASH_KEYS_PER_TRIP, s) // tk
    assert unroll % 2 == 0 and (s // tk) % unroll == 0 and tk % tv == 0
    diff = diff_args is not None
    kern = functools.partial(_flash_kernel, groups_stacked=groups_stacked, dk=dk, tq=tq, tk=tk, tv=tv,
                             seq=s, unroll=unroll, diff=diff)
    in_specs = [pl.BlockSpec((groups_stacked * dk, tq), lambda g, i: (g, i)),
                pl.BlockSpec((s, dk), lambda g, i: (0, g)),
                pl.BlockSpec((1, n_vblk, HEAD_DIM, tv), lambda g, i: (g, 0, 0, 0))]
    args = [q_t, k, v_t]
    if diff:
        bias, far, scal, gain = diff_args
        in_specs += [pl.BlockSpec((1,) + bias.shape[1:], lambda g, i: (g, 0, 0, 0),
                                  pipeline_mode=pl.Buffered(1)),
                     pl.BlockSpec(memory_space=pltpu.SMEM),
                     pl.BlockSpec(memory_space=pltpu.SMEM),
                     pl.BlockSpec((1, HEAD_DIM), lambda g, i: (0, 0))]
        args += [bias, far, scal, gain.reshape(1, HEAD_DIM)]
        out_w = HEAD_DIM
    else:
        out_w = groups_stacked * HEAD_DIM
    return pl.pallas_call(
        kern,
        grid=(groups, s // tq),
        in_specs=in_specs,
        out_specs=pl.BlockSpec((tq, out_w), lambda g, i: (i, g)),
        out_shape=jax.ShapeDtypeStruct((s, groups * out_w), BF16),
        scratch_shapes=([pltpu.VMEM((tk, groups_stacked * tq), F32)] * 2
                        + [pltpu.VMEM((HEAD_DIM, groups_stacked * tq), F32)]
                        + [pltpu.VMEM((tk, groups_stacked * tq), BF16)] * 2),
        compiler_params=_params(("arbitrary", "arbitrary")),
        name=name,
    )(*args)


def _window_kernel(q_ref, kp_ref, kc_ref, kn_ref, vp_ref, vc_ref, vn_ref, bias_ref, sink_ref, o_ref,
                   kbuf, vbuf, *, tw, seq):
    w = WINDOW
    h = pl.program_id(0)
    i = pl.program_id(1)
    kbuf[0:w] = kp_ref[tw - w:tw]
    kbuf[w:w + tw] = kc_ref[...]
    kbuf[w + tw:w + tw + w] = kn_ref[0:w]
    vbuf[0:w] = vp_ref[tw - w:tw]
    vbuf[w:w + tw] = vc_ref[...]
    vbuf[w + tw:w + tw + w] = vn_ref[0:w]
    sink = sink_ref[h] * LOG2E
    nq = tw // w
    q3 = q_ref[...].reshape(nq, w, HEAD_DIM)
    k3 = jnp.stack([kbuf[j * w:(j + 3) * w] for j in range(nq)])
    v3 = jnp.stack([vbuf[j * w:(j + 3) * w] for j in range(nq)])
    s = jnp.einsum("jqd,jkd->jqk", q3, k3, preferred_element_type=F32) + bias_ref[...]
    kpos = (i * tw - w + w * lax.broadcasted_iota(jnp.int32, (nq, w, 3 * w), 0)
            + lax.broadcasted_iota(jnp.int32, (nq, w, 3 * w), 2))
    s = jnp.where((kpos >= 0) & (kpos < seq), s, NEG_BIG)
    m = jnp.maximum(jnp.max(s, axis=-1, keepdims=True), sink)
    p = jnp.exp2(s - m)
    denom = jnp.sum(p, axis=-1, keepdims=True) + jnp.exp2(sink - m)
    o = jnp.einsum("jqk,jkd->jqd", p.astype(BF16), v3, preferred_element_type=F32) / denom
    o_ref[...] = o.reshape(tw, HEAD_DIM).astype(o_ref.dtype)


def _window(q, k, v, bias, sink, tw=1024):
    s = q.shape[0]
    hd = HEAD_DIM
    nb = s // tw
    grp = C_HEADS // C_KV_HEADS
    prev = lambda h, i: (jnp.maximum(i - 1, 0), h // grp)
    cur = lambda h, i: (i, h // grp)
    nxt = lambda h, i: (jnp.minimum(i + 1, nb - 1), h // grp)
    kv_specs = [pl.BlockSpec((tw, hd), f) for f in (prev, cur, nxt)]
    return pl.pallas_call(
        functools.partial(_window_kernel, tw=tw, seq=s),
        grid=(C_HEADS, nb),
        in_specs=[pl.BlockSpec((tw, hd), lambda h, i: (i, h))] + kv_specs + kv_specs
                 + [pl.BlockSpec((1, WINDOW, 3 * WINDOW), lambda h, i: (h, 0, 0)),
                    pl.BlockSpec(memory_space=pltpu.SMEM)],
        out_specs=pl.BlockSpec((tw, hd), lambda h, i: (i, h)),
        out_shape=jax.ShapeDtypeStruct((s, C_HEADS * hd), BF16),
        scratch_shapes=[pltpu.VMEM((tw + 2 * WINDOW, hd), BF16),
                        pltpu.VMEM((tw + 2 * WINDOW, hd), BF16)],
        compiler_params=_params(("arbitrary", "arbitrary")),
        name="window_attn",
    )(q, k, k, k, v, v, v, bias, sink)


def _final_norm_kernel(x_ref, g_ref, o_ref):
    o_ref[...] = _rms(x_ref[...], g_ref[...])


def _final_norm(x, gain, tm=512):
    s, d = x.shape
    return pl.pallas_call(
        _final_norm_kernel,
        grid=(s // tm,),
        in_specs=[pl.BlockSpec((tm, d), lambda i: (i, 0)), pl.BlockSpec((1, d), lambda i: (0, 0))],
        out_specs=pl.BlockSpec((tm, d), lambda i: (i, 0)),
        out_shape=jax.ShapeDtypeStruct((s, d), F32),
        compiler_params=_params(("arbitrary",)),
        name="final_norm",
    )(x, gain.reshape(1, d))


def _rope_tables(pos, dim):
    inv = ROPE_THETA ** (-jnp.arange(0, dim, 2, dtype=F32) / dim)
    ang = pos.astype(F32)[:, None] * inv[None, :]
    ang = jnp.concatenate([ang, ang], axis=-1)
    return jnp.cos(ang), jnp.sin(ang)


def _rope_pack(cos, sin, half):
    width = cos.shape[1]
    lane = jnp.arange(width)
    first = (lane % (2 * half)) < half
    tabs = [cos, jnp.where(first, -sin, 0.0), jnp.where(first, 0.0, sin)]
    return jnp.stack([jnp.pad(t, ((0, 0), (0, LANES - width))) for t in tabs])


def _t5_bucket(rel):
    half = N_BUCKETS // 2
    max_exact = half // 2
    n = jnp.abs(rel)
    large = max_exact + (jnp.log(jnp.maximum(n, 1).astype(F32) / max_exact)
                         / math.log(MAX_DISTANCE / max_exact) * (half - max_exact)).astype(jnp.int32)
    large = jnp.minimum(large, half - 1)
    return jnp.where(rel > 0, half, 0) + jnp.where(n < max_exact, n, large)


def _toeplitz(fn, rows, cols, offset):
    n = rows + cols + 1
    vec = fn(jnp.arange(n) - rows + offset)
    flat = jnp.tile(vec, (1, rows))[:, :rows * (n - 1)]
    return flat.reshape(vec.shape[0], rows, n - 1)[:, :, rows:rows + cols]


TQ_A, TQ_B, TQ_D = 256, 512, 256
TK_DENSE = 1024
FLASH_KEYS_PER_TRIP = 4096


def kernel(x, c, w_ada, b_ada, norm1, w_in, a_q_norm, a_k_norm, b_q_norm, b_kv_norm, b_w_uq, b_w_ukv,
           c_sink, d_lambda, d_sub_norm, w_out, norm2, w_ff1, w_ff2, rel_bias, final_norm):
    _, s, d = x.shape
    depth = w_ada.shape[0]
    hd = HEAD_DIM
    x2 = x.reshape(s, d)

    t = jnp.arange(s)
    grid_h = s // GRID_W
    cos_r, sin_r = (jnp.repeat(tab, GRID_W, axis=0) for tab in _rope_tables(jnp.arange(grid_h), hd // 2))
    cos_c, sin_c = (jnp.tile(tab, (grid_h, 1)) for tab in _rope_tables(jnp.arange(GRID_W), hd // 2))
    ax_tab = _rope_pack(jnp.concatenate([cos_r, cos_c], -1), jnp.concatenate([sin_r, sin_c], -1), hd // 4)
    sq_tab = _rope_pack(*_rope_tables(t, B_ROPE), B_ROPE // 2)

    bias_fn = lambda heads: (lambda rel: (rel_bias[_t5_bucket(rel)][:, heads] * LOG2E).T)
    c_heads = jnp.arange(C_HEADS)
    d_heads = C_HEADS + jnp.arange(D_HEADS)
    rel_c = jnp.arange(3 * WINDOW)[None, :] - WINDOW - jnp.arange(WINDOW)[:, None]
    bias_c = jnp.where(jnp.abs(rel_c)[None] <= WINDOW, _toeplitz(bias_fn(c_heads), WINDOW, 3 * WINDOW, -WINDOW),
                       NEG_BIG)
    n_off = TK_DENSE // TQ_D + 2
    bias_d = jnp.stack([_toeplitz(bias_fn(d_heads), TQ_D, TK_DENSE, o * TQ_D - TK_DENSE)
                        for o in range(n_off)], axis=1)
    far_d = bias_fn(d_heads)(jnp.array([-MAX_DISTANCE, MAX_DISTANCE]))
    bias_d = jnp.concatenate(
        [bias_d, jnp.broadcast_to(far_d[:, :, None, None], (D_HEADS, 2, TQ_D, TK_DENSE))], axis=1)
    bias_d = jnp.swapaxes(bias_d, 2, 3)

    mod_all = _ada_mod(c, w_ada, b_ada)

    for l in range(depth):
        mod = mod_all[l]
        lam_init = 0.8 - 0.6 * math.exp(-0.3 * l)
        lf = d_lambda[l].astype(F32)
        lam = jnp.exp(jnp.sum(lf[0] * lf[1])) - jnp.exp(jnp.sum(lf[2] * lf[3])) + lam_init
        scal = jnp.stack([lam, jnp.asarray(1.0 - lam_init, F32)])

        w_in_l = w_in[l]
        w_in_p = jnp.concatenate([w_in_l[:, :COL_B_KR + B_ROPE], jnp.zeros((d, B_KR_PAD), F32),
                                  w_in_l[:, COL_B_KR + B_ROPE:]], axis=1).astype(BF16)
        wuq = jnp.pad(b_w_uq[l].reshape(B_Q_RANK, B_HEADS, B_NOPE + B_ROPE),
                      ((0, 0), (0, 0), (0, B_QK_PAD - B_NOPE - B_ROPE))).reshape(B_Q_RANK, -1).astype(BF16)
        wukv = b_w_ukv[l].astype(BF16)

        qkv = _norm_matmul(x2, norm1[l], mod, w_in_p, shift_row=0, scale_row=1, relu2=False,
                           out_dtype=F32, tm=512, tn=IN_COLS_PAD, name="in_proj")
        (qa, ka, va, qb, kb, vb, qc, kc, vc, qd, kd, vd) = _prep(
            qkv, ax_tab, sq_tab, a_q_norm[l], a_k_norm[l], b_q_norm[l], b_kv_norm[l], wuq, wukv)

        oa = _flash(qa, ka, va, groups=A_KV_HEADS, groups_stacked=A_HEADS // A_KV_HEADS, dk=hd,
                    tq=TQ_A, tk=TK_DENSE, name="flash_a")
        ob = _flash(qb, kb, vb, groups=B_HEADS, groups_stacked=1, dk=B_QK_PAD,
                    tq=TQ_B, tk=TK_DENSE, name="flash_b")
        oc = _window(qc, kc, vc, bias_c, c_sink[l])
        od = _flash(qd, kd, vd, groups=D_HEADS, groups_stacked=2, dk=hd, tq=TQ_D, tk=TK_DENSE,
                    name="flash_d", diff_args=(bias_d, far_d, scal, d_sub_norm[l]))

        w_out_l = w_out[l].astype(BF16)
        heads = [oa, ob, oc, od]
        w_parts = [w_out_l[k * 4 * hd:(k + 1) * 4 * hd] for k in range(4)]
        x2 = _matmul_res(heads, w_parts, x2, mod, gate_row=2, tm=512, tn=d, name="out_proj")

        hid = _norm_matmul(x2, norm2[l], mod, w_ff1[l].astype(BF16), shift_row=3, scale_row=4, relu2=True,
                           out_dtype=BF16, tm=256, tn=w_ff1.shape[2], name="ff1")
        x2 = _matmul_res([hid], [w_ff2[l].astype(BF16)], x2, mod, gate_row=5, tm=256, tn=d, name="ff2",
                         final_gain=final_norm if l == depth - 1 else None)

    return x2.reshape(1, s, d)
```

```python
import functools
import math

import jax
import jax.numpy as jnp
from jax import lax
from jax.experimental import pallas as pl
from jax.experimental.pallas import tpu as pltpu

F32 = jnp.float32
BF16 = jnp.bfloat16

HEAD_DIM = 128
GRID_W = 64
ROPE_THETA = 10000.0
EPS = 1e-6
A_HEADS, A_KV_HEADS = 4, 2
B_HEADS, B_NOPE, B_ROPE, B_Q_RANK, B_KV_RANK = 4, 128, 64, 384, 256
C_HEADS, C_KV_HEADS, WINDOW = 4, 2, 128
D_HEADS = 4
D_HALF = HEAD_DIM // 2
N_BUCKETS, MAX_DISTANCE = 32, 128
N_MOD = 6

LOG2E = math.log2(math.e)
NEG_BIG = -1e30
LAZY_MAX_JUMP = 60.0
N_TILED = 3

LANES = 128
MXU_DIM = 256
VMEM_LIMIT = 56 * 1024 * 1024

TV = MXU_DIM

B_KR_PAD = LANES - B_ROPE
COL_A_Q = 0
COL_A_K = COL_A_Q + A_HEADS * HEAD_DIM
COL_A_V = COL_A_K + A_KV_HEADS * HEAD_DIM
COL_B_CQ = COL_A_V + A_KV_HEADS * HEAD_DIM
COL_B_CKV = COL_B_CQ + B_Q_RANK
COL_B_KR = COL_B_CKV + B_KV_RANK
COL_C_Q = COL_B_KR + B_ROPE + B_KR_PAD
COL_C_K = COL_C_Q + C_HEADS * HEAD_DIM
COL_C_V = COL_C_K + C_KV_HEADS * HEAD_DIM
COL_D_Q = COL_C_V + C_KV_HEADS * HEAD_DIM
COL_D_K = COL_D_Q + D_HEADS * HEAD_DIM
COL_D_V = COL_D_K + D_HEADS * HEAD_DIM
IN_COLS_PAD = COL_D_V + D_HEADS * HEAD_DIM
B_QK_PAD = 2 * LANES


def _params(sem):
    return pltpu.CompilerParams(dimension_semantics=sem, vmem_limit_bytes=VMEM_LIMIT)


def _ada_kernel(c_ref, w_ref, b_ref, o_ref):
    c = c_ref[...]
    cs = c * (1.0 / (1.0 + jnp.exp(-c)))
    cs8 = jnp.broadcast_to(cs, (8, cs.shape[1])).astype(BF16)
    r = jnp.dot(cs8, w_ref[0].astype(BF16), preferred_element_type=F32)
    o_ref[0] = r[0:1] + b_ref[0]


def _ada_mod(c, w_ada, b_ada, tn=1024):
    depth, d, n = w_ada.shape
    out = pl.pallas_call(
        _ada_kernel,
        grid=(depth, n // tn),
        in_specs=[pl.BlockSpec((1, d), lambda l, j: (0, 0)),
                  pl.BlockSpec((1, d, tn), lambda l, j: (l, 0, j)),
                  pl.BlockSpec((1, 1, tn), lambda l, j: (l, 0, j))],
        out_specs=pl.BlockSpec((1, 1, tn), lambda l, j: (l, 0, j)),
        out_shape=jax.ShapeDtypeStruct((depth, 1, n), F32),
        compiler_params=_params(("arbitrary", "arbitrary")),
        name="ada_mod",
    )(c, w_ada, b_ada.reshape(depth, 1, n))
    return out.reshape(depth, N_MOD, d)


def _norm_mm_kernel(x_ref, g_ref, mod_ref, w_ref, o_ref, h_ref, *, shift_row, scale_row, relu2):
    @pl.when(pl.program_id(1) == 0)
    def _():
        x = x_ref[...]
        y = x * lax.rsqrt(jnp.mean(x * x, axis=-1, keepdims=True) + EPS) * g_ref[...]
        h = y * (1.0 + mod_ref[scale_row:scale_row + 1, :]) + mod_ref[shift_row:shift_row + 1, :]
        h_ref[...] = h.astype(BF16)

    acc = jnp.dot(h_ref[...], w_ref[...], preferred_element_type=F32)
    if relu2:
        acc = jnp.square(jnp.maximum(acc, 0.0))
    o_ref[...] = acc.astype(o_ref.dtype)


def _norm_matmul(x, gain, mod, w, *, shift_row, scale_row, relu2, out_dtype, tm, tn, name):
    s, d = x.shape
    n = w.shape[1]
    return pl.pallas_call(
        functools.partial(_norm_mm_kernel, shift_row=shift_row, scale_row=scale_row, relu2=relu2),
        grid=(s // tm, n // tn),
        in_specs=[pl.BlockSpec((tm, d), lambda i, j: (i, 0)),
                  pl.BlockSpec((1, d), lambda i, j: (0, 0)),
                  pl.BlockSpec((N_MOD, d), lambda i, j: (0, 0)),
                  pl.BlockSpec((d, tn), lambda i, j: (0, j), pipeline_mode=pl.Buffered(1 if tn == n else 2))],
        out_specs=pl.BlockSpec((tm, tn), lambda i, j: (i, j)),
        out_shape=jax.ShapeDtypeStruct((s, n), out_dtype),
        scratch_shapes=[pltpu.VMEM((tm, d), BF16)],
        compiler_params=_params(("arbitrary", "arbitrary")),
        name=name,
    )(x, gain.reshape(1, d), mod, w)


def _mm_res_kernel(*refs, n_in, gate_row, final_norm):
    a_refs, w_refs = refs[:n_in], refs[n_in:2 * n_in]
    x_ref, mod_ref = refs[2 * n_in:2 * n_in + 2]
    o_ref = refs[-1]
    acc = jnp.dot(a_refs[0][...], w_refs[0][...], preferred_element_type=F32)
    for a_ref, w_ref in zip(a_refs[1:], w_refs[1:]):
        acc = acc + jnp.dot(a_ref[...], w_ref[...], preferred_element_type=F32)
    y = x_ref[...] + mod_ref[gate_row:gate_row + 1, :] * acc
    if final_norm:
        y = _rms(y, refs[-2][...])
    o_ref[...] = y


def _matmul_res(a_list, w_list, x, mod, *, gate_row, tm, tn, name, final_gain=None):
    s, d = x.shape
    n_in = len(a_list)
    assert final_gain is None or tn == d
    w_mode = pl.Buffered(1 if tn == d else 2)
    in_specs = ([pl.BlockSpec((tm, a.shape[1]), lambda i, j: (i, 0)) for a in a_list]
                + [pl.BlockSpec((w.shape[0], tn), lambda i, j: (0, j), pipeline_mode=w_mode) for w in w_list]
                + [pl.BlockSpec((tm, tn), lambda i, j: (i, j)),
                   pl.BlockSpec((N_MOD, tn), lambda i, j: (0, j))])
    args = [*a_list, *w_list, x, mod]
    if final_gain is not None:
        in_specs.append(pl.BlockSpec((1, d), lambda i, j: (0, 0)))
        args.append(final_gain.reshape(1, d))
    return pl.pallas_call(
        functools.partial(_mm_res_kernel, n_in=n_in, gate_row=gate_row, final_norm=final_gain is not None),
        grid=(s // tm, d // tn),
        in_specs=in_specs,
        out_specs=pl.BlockSpec((tm, tn), lambda i, j: (i, j)),
        out_shape=jax.ShapeDtypeStruct((s, d), F32),
        compiler_params=_params(("arbitrary", "arbitrary")),
        name=name,
    )(*args)


def _rope(x, cos, sin_lo, sin_hi):
    return (x * cos + pltpu.roll(x, LANES - 32, 1) * sin_lo + pltpu.roll(x, 32, 1) * sin_hi)


def _rms(x, gain):
    return x * lax.rsqrt(jnp.mean(x * x, axis=-1, keepdims=True) + EPS) * gain


def _store_val_t(ref, h, v):
    v_t = v.T.astype(BF16)
    n_blk, tv = ref.shape[1], ref.shape[3]
    for j in range(n_blk):
        ref[h, j] = v_t[:, j * tv:(j + 1) * tv]


def _prep_kernel(qkv_ref, ax_ref, sq_ref, aqn_ref, akn_ref, bqn_ref, bkvn_ref, wuq_ref, wukv_ref,
                 qa_ref, ka_ref, va_ref, qb_ref, kb_ref, vb_ref, qc_ref, kc_ref, vc_ref,
                 qd_ref, kd_ref, vd_ref):
    hd = HEAD_DIM
    ax_cos, ax_lo, ax_hi = ax_ref[0], ax_ref[1], ax_ref[2]
    sq_cos, sq_lo, sq_hi = sq_ref[0], sq_ref[1], sq_ref[2]

    def col(start, width=hd):
        return qkv_ref[:, start:start + width]

    sc_a = HEAD_DIM ** -0.5 * LOG2E
    for h in range(A_HEADS):
        q = _rope(_rms(col(COL_A_Q + h * hd), aqn_ref[...]), ax_cos, ax_lo, ax_hi)
        qa_ref[h * hd:(h + 1) * hd, :] = (q * sc_a).T.astype(BF16)
    for h in range(A_KV_HEADS):
        k = _rope(_rms(col(COL_A_K + h * hd), akn_ref[...]), ax_cos, ax_lo, ax_hi)
        ka_ref[:, h * hd:(h + 1) * hd] = k.astype(BF16)
        _store_val_t(va_ref, h, col(COL_A_V + h * hd))

    sc_b = (B_NOPE + B_ROPE) ** -0.5 * LOG2E
    cq = _rms(col(COL_B_CQ, B_Q_RANK), bqn_ref[...]).astype(BF16)
    qb = jnp.dot(cq, wuq_ref[...], preferred_element_type=F32)
    ckv = _rms(col(COL_B_CKV, B_KV_RANK), bkvn_ref[...]).astype(BF16)
    kvb = jnp.dot(ckv, wukv_ref[...], preferred_element_type=F32)
    kr = _rope(col(COL_B_KR, LANES), sq_cos, sq_lo, sq_hi).astype(BF16)
    for h in range(B_HEADS):
        base = h * B_QK_PAD
        qb_ref[base:base + hd, :] = (qb[:, base:base + hd] * sc_b).T.astype(BF16)
        q_r = _rope(qb[:, base + hd:base + 2 * hd], sq_cos, sq_lo, sq_hi)
        qb_ref[base + hd:base + 2 * hd, :] = (q_r * sc_b).T.astype(BF16)
        kb_ref[:, base:base + hd] = kvb[:, base:base + hd].astype(BF16)
        kb_ref[:, base + hd:base + 2 * hd] = kr
        _store_val_t(vb_ref, h, kvb[:, base + hd:base + 2 * hd])

    sc_c = HEAD_DIM ** -0.5 * LOG2E
    qc_ref[...] = (col(COL_C_Q, C_HEADS * hd) * sc_c).astype(BF16)
    kc_ref[...] = col(COL_C_K, C_KV_HEADS * hd).astype(BF16)
    vc_ref[...] = col(COL_C_V, C_KV_HEADS * hd).astype(BF16)

    sc_d = D_HALF ** -0.5 * LOG2E
    lane = lax.broadcasted_iota(jnp.int32, (qkv_ref.shape[0], hd), 1)
    lo = lane < D_HALF
    for h in range(D_HEADS):
        q = col(COL_D_Q + h * hd) * sc_d
        qd_ref[2 * h * hd:(2 * h + 1) * hd, :] = jnp.where(lo, q, 0.0).T.astype(BF16)
        qd_ref[(2 * h + 1) * hd:(2 * h + 2) * hd, :] = jnp.where(lo, 0.0, q).T.astype(BF16)
        _store_val_t(vd_ref, h, col(COL_D_V + h * hd))
    kd_ref[...] = col(COL_D_K, D_HEADS * hd).astype(BF16)


def _prep(qkv, ax_tab, sq_tab, aqn, akn, bqn, bkvn, wuq, wukv, tm=512):
    s = qkv.shape[0]
    hd = HEAD_DIM
    row = lambda i: (i, 0)
    full = lambda i: (0, 0)

    def row_major(w):
        return pl.BlockSpec((tm, w), row), jax.ShapeDtypeStruct((s, w), BF16)

    def feat_major(w):
        return pl.BlockSpec((w, tm), lambda i: (0, i)), jax.ShapeDtypeStruct((w, s), BF16)

    def val_t(h):
        return (pl.BlockSpec((h, tm // TV, hd, TV), lambda i: (0, i, 0, 0)),
                jax.ShapeDtypeStruct((h, s // TV, hd, TV), BF16))

    outs = [feat_major(A_HEADS * hd), row_major(A_KV_HEADS * hd), val_t(A_KV_HEADS),
            feat_major(B_HEADS * B_QK_PAD), row_major(B_HEADS * B_QK_PAD), val_t(B_HEADS),
            row_major(C_HEADS * hd), row_major(C_KV_HEADS * hd), row_major(C_KV_HEADS * hd),
            feat_major(2 * D_HEADS * hd), row_major(D_HEADS * hd), val_t(D_HEADS)]
    return pl.pallas_call(
        _prep_kernel,
        grid=(s // tm,),
        in_specs=[pl.BlockSpec((tm, IN_COLS_PAD), row),
                  pl.BlockSpec((3, tm, LANES), lambda i: (0, i, 0)),
                  pl.BlockSpec((3, tm, LANES), lambda i: (0, i, 0)),
                  pl.BlockSpec((1, hd), full), pl.BlockSpec((1, hd), full),
                  pl.BlockSpec((1, B_Q_RANK), full), pl.BlockSpec((1, B_KV_RANK), full),
                  pl.BlockSpec(wuq.shape, full), pl.BlockSpec(wukv.shape, full)],
        out_specs=[spec for spec, _ in outs],
        out_shape=[shape for _, shape in outs],
        compiler_params=_params(("arbitrary",)),
        name="prep",
    )(qkv, ax_tab, sq_tab, aqn.reshape(1, hd), akn.reshape(1, hd),
      bqn.reshape(1, B_Q_RANK), bkvn.reshape(1, B_KV_RANK), wuq, wukv)


def _flash_kernel(*refs, groups_stacked, dk, tq, tk, tv, seq, unroll, diff):
    if diff:
        q_ref, k_ref, v_ref, bias_ref, far_ref, scal_ref, gain_ref, o_ref = refs[:8]
    else:
        q_ref, k_ref, v_ref, o_ref = refs[:4]
    s_scr, acc_scr, p_scr = refs[-5:-3], refs[-3], refs[-2:]
    r_stack = groups_stacked
    cols = r_stack * tq
    i = pl.program_id(1)
    n_chunks = seq // tk
    n_sub = tk // tv
    q0 = i * tq
    q_t = jnp.concatenate([q_ref[r * dk:(r + 1) * dk, :] for r in range(r_stack)], axis=1)

    def raw_scores(c, far=False, n_keys=tk):
        start = c * tk
        if not isinstance(start, int):
            start = pl.multiple_of(start, tk)
        s = jnp.dot(k_ref[pl.ds(start, n_keys), :], q_t, preferred_element_type=F32)
        if not diff:
            return [(slice(0, cols), s)], 0.0
        if far:
            g = pl.program_id(0)
            return [(slice(0, cols), s)], jnp.where((c + 1) * tk <= q0, far_ref[g, 0], far_ref[g, 1])
        n_near = tk // tq + 2
        off = c * tk - q0
        idx = jnp.where(off < -tk, n_near, jnp.where(off > tq, n_near + 1, (off + tk) // tq))
        tile = bias_ref[0, idx, 0:n_keys, :]
        return [(slice(r * tq, (r + 1) * tq), s[:, r * tq:(r + 1) * tq] + tile) for r in range(r_stack)], 0.0

    def finalize(acc_t, l):
        o_t = acc_t / l
        o = [o_t[:, r * tq:(r + 1) * tq].T for r in range(r_stack)]
        if not diff:
            for r in range(r_stack):
                o_ref[:, r * HEAD_DIM:(r + 1) * HEAD_DIM] = o[r].astype(o_ref.dtype)
            return
        lam = scal_ref[0]
        od = o[0] - lam * o[1]
        y = od * lax.rsqrt(jnp.mean(od * od, axis=-1, keepdims=True) + EPS) * gain_ref[...]
        o_ref[...] = (y * scal_ref[1]).astype(o_ref.dtype)

    def values(c, p_ref, acc):
        for j in range(n_sub):
            acc = acc + jnp.dot(v_ref[0, c * n_sub + j], p_ref[j * tv:(j + 1) * tv, :], preferred_element_type=F32)
        return acc

    n_groups = n_chunks // unroll

    def scores_probs(c, slot, m_ref_row, far):
        pieces, offset = raw_scores(c, far)
        ref_row = m_ref_row - offset
        m_parts, l_parts = [], []
        for cs, s_piece in pieces:
            e = jnp.exp2(s_piece - ref_row[:, cs])
            p_scr[slot][:, cs] = e.astype(BF16)
            m_parts.append(jnp.max(s_piece, axis=0, keepdims=True))
            l_parts.append(jnp.sum(e, axis=0, keepdims=True))
        return jnp.concatenate(m_parts, axis=1) + offset, jnp.concatenate(l_parts, axis=1)

    first_tiled = jnp.clip(q0 // tk - 1, 0, n_chunks - N_TILED)

    def is_far(pos):
        return diff and not (isinstance(pos, int) and pos < N_TILED)

    def chunk_at(pos):
        if not diff:
            return pos
        if not is_far(pos):
            return first_tiled + pos
        j = pos - N_TILED
        return j + jnp.where(j >= first_tiled, N_TILED, 0)

    def fast_step(pos, par, state, do_scores):
        m_run, beta, l, jump = state
        c = chunk_at(pos)
        if do_scores:
            m_chunk, l_chunk = scores_probs(chunk_at(pos + 1), 1 - par, m_run, is_far(pos + 1))
        acc_scr[...] = beta * values(c, p_scr[par], acc_scr[...])
        if not do_scores:
            return state
        m_new = jnp.maximum(m_run, m_chunk)
        beta_new = jnp.exp2(m_run - m_new)
        return m_new, beta_new, beta_new * (l + l_chunk), jnp.maximum(jump, m_chunk - m_run)

    def fast_group(base, state, last, size=unroll):
        for u in range(size):
            state = fast_step(base + u, u % 2, state, not (last and u == size - 1))
        return state

    m_first, l_first = scores_probs(chunk_at(0), 0, jnp.zeros((1, cols), F32), False)
    beta_first = jnp.exp2(-m_first)
    acc_scr[...] = jnp.zeros((HEAD_DIM, cols), F32)
    state = (m_first, beta_first, beta_first * l_first, jnp.abs(m_first))
    first_loop_group = 0
    if diff and n_groups > 1:
        state = fast_group(0, state, False)
        first_loop_group = 1
    n_ahead = (n_groups - 1 - first_loop_group) * unroll
    trip = FLASH_LONG_TRIP if (n_ahead % FLASH_LONG_TRIP == 0 and n_ahead // FLASH_LONG_TRIP > 1) else unroll
    if n_ahead > 0:
        base0 = first_loop_group * unroll
        state = lax.fori_loop(0, n_ahead // trip,
                              lambda t, st: fast_group(base0 + t * trip, st, False, trip), state)
    _, _, l_fast, jump = fast_group((n_groups - 1) * unroll, state, True)
    finalize(acc_scr[...], l_fast)

    def scores(c, slot):
        m_parts = []
        for cs, s_piece in raw_scores(c)[0]:
            s_scr[slot][:, cs] = s_piece
            m_parts.append(jnp.max(s_piece, axis=0, keepdims=True))
        return jnp.concatenate(m_parts, axis=1)

    def step(c, par, m_chunk, state, do_scores):
        m, l = state
        m_new = jnp.maximum(m, m_chunk)
        alpha = jnp.exp2(m - m_new)
        l = alpha * l
        m_next = scores(c + 1, 1 - par) if do_scores else None
        acc = alpha * acc_scr[...]
        for j in range(n_sub):
            p = jnp.exp2(s_scr[par][j * tv:(j + 1) * tv, :] - m_new)
            l = l + jnp.sum(p, axis=0, keepdims=True)
            acc = acc + jnp.dot(v_ref[0, c * n_sub + j], p.astype(BF16), preferred_element_type=F32)
        acc_scr[...] = acc
        return m_next, (m_new, l)

    def group(base, carry, last):
        m_chunk, state = carry
        for u in range(unroll):
            m_chunk, state = step(base + u, u % 2, m_chunk, state, not (last and u == unroll - 1))
        return m_chunk, state

    @pl.when(jnp.max(jump) > LAZY_MAX_JUMP)
    def _():
        acc_scr[...] = jnp.zeros((HEAD_DIM, cols), F32)
        carry = (scores(0, 0), (jnp.full((1, cols), NEG_BIG, F32), jnp.zeros((1, cols), F32)))
        if n_groups > 1:
            carry = lax.fori_loop(0, n_groups - 1, lambda t, cr: group(t * unroll, cr, False), carry)
        _, (_, l_exact) = group((n_groups - 1) * unroll, carry, True)
        finalize(acc_scr[...], l_exact)


def _flash(q_t, k, v_t, *, groups, groups_stacked, dk, tq, tk, name, diff_args=None):
    s = k.shape[0]
    n_vblk, tv = v_t.shape[1], v_t.shape[3]
    unroll = min(FLASH_KEYS_PER_TRIP, s) // tk
    assert unroll % 2 == 0 and (s // tk) % unroll == 0 and tk % tv == 0
    diff = diff_args is not None
    kern = functools.partial(_flash_kernel, groups_stacked=groups_stacked, dk=dk, tq=tq, tk=tk, tv=tv,
                             seq=s, unroll=unroll, diff=diff)
    in_specs = [pl.BlockSpec((groups_stacked * dk, tq), lambda g, i: (g, i)),
                pl.BlockSpec((s, dk), lambda g, i: (0, g)),
                pl.BlockSpec((1, n_vblk, HEAD_DIM, tv), lambda g, i: (g, 0, 0, 0))]
    args = [q_t, k, v_t]
    if diff:
        bias, far, scal, gain = diff_args
        in_specs += [pl.BlockSpec((1,) + bias.shape[1:], lambda g, i: (g, 0, 0, 0),
                                  pipeline_mode=pl.Buffered(1)),
                     pl.BlockSpec(memory_space=pltpu.SMEM),
                     pl.BlockSpec(memory_space=pltpu.SMEM),
                     pl.BlockSpec((1, HEAD_DIM), lambda g, i: (0, 0))]
        args += [bias, far, scal, gain.reshape(1, HEAD_DIM)]
        out_w = HEAD_DIM
    else:
        out_w = groups_stacked * HEAD_DIM
    return pl.pallas_call(
        kern,
        grid=(groups, s // tq),
        in_specs=in_specs,
        out_specs=pl.BlockSpec((tq, out_w), lambda g, i: (i, g)),
        out_shape=jax.ShapeDtypeStruct((s, groups * out_w), BF16),
        scratch_shapes=([pltpu.VMEM((tk, groups_stacked * tq), F32)] * 2
                        + [pltpu.VMEM((HEAD_DIM, groups_stacked * tq), F32)]
                        + [pltpu.VMEM((tk, groups_stacked * tq), BF16)] * 2),
        compiler_params=_params(("arbitrary", "arbitrary")),
        name=name,
    )(*args)


def _window_kernel(q_ref, kp_ref, kc_ref, kn_ref, vp_ref, vc_ref, vn_ref, bias_ref, sink_ref, o_ref,
                   kbuf, vbuf, *, tw, seq):
    w = WINDOW
    h = pl.program_id(0)
    i = pl.program_id(1)
    kbuf[0:w] = kp_ref[tw - w:tw]
    kbuf[w:w + tw] = kc_ref[...]
    kbuf[w + tw:w + tw + w] = kn_ref[0:w]
    vbuf[0:w] = vp_ref[tw - w:tw]
    vbuf[w:w + tw] = vc_ref[...]
    vbuf[w + tw:w + tw + w] = vn_ref[0:w]
    sink = sink_ref[h] * LOG2E
    nq = tw // w
    q3 = q_ref[...].reshape(nq, w, HEAD_DIM)
    k3 = jnp.stack([kbuf[j * w:(j + 3) * w] for j in range(nq)])
    v3 = jnp.stack([vbuf[j * w:(j + 3) * w] for j in range(nq)])
    s = jnp.einsum("jqd,jkd->jqk", q3, k3, preferred_element_type=F32) + bias_ref[...]
    kpos = (i * tw - w + w * lax.broadcasted_iota(jnp.int32, (nq, w, 3 * w), 0)
            + lax.broadcasted_iota(jnp.int32, (nq, w, 3 * w), 2))
    s = jnp.where((kpos >= 0) & (kpos < seq), s, NEG_BIG)
    m = jnp.maximum(jnp.max(s, axis=-1, keepdims=True), sink)
    p = jnp.exp2(s - m)
    denom = jnp.sum(p, axis=-1, keepdims=True) + jnp.exp2(sink - m)
    o = jnp.einsum("jqk,jkd->jqd", p.astype(BF16), v3, preferred_element_type=F32) / denom
    o_ref[...] = o.reshape(tw, HEAD_DIM).astype(o_ref.dtype)


def _window(q, k, v, bias, sink, tw=1024):
    s = q.shape[0]
    hd = HEAD_DIM
    nb = s // tw
    grp = C_HEADS // C_KV_HEADS
    prev = lambda h, i: (jnp.maximum(i - 1, 0), h // grp)
    cur = lambda h, i: (i, h // grp)
    nxt = lambda h, i: (jnp.minimum(i + 1, nb - 1), h // grp)
    kv_specs = [pl.BlockSpec((tw, hd), f) for f in (prev, cur, nxt)]
    return pl.pallas_call(
        functools.partial(_window_kernel, tw=tw, seq=s),
        grid=(C_HEADS, nb),
        in_specs=[pl.BlockSpec((tw, hd), lambda h, i: (i, h))] + kv_specs + kv_specs
                 + [pl.BlockSpec((1, WINDOW, 3 * WINDOW), lambda h, i: (h, 0, 0)),
                    pl.BlockSpec(memory_space=pltpu.SMEM)],
        out_specs=pl.BlockSpec((tw, hd), lambda h, i: (i, h)),
        out_shape=jax.ShapeDtypeStruct((s, C_HEADS * hd), BF16),
        scratch_shapes=[pltpu.VMEM((tw + 2 * WINDOW, hd), BF16),
                        pltpu.VMEM((tw + 2 * WINDOW, hd), BF16)],
        compiler_params=_params(("arbitrary", "arbitrary")),
        name="window_attn",
    )(q, k, k, k, v, v, v, bias, sink)


def _final_norm_kernel(x_ref, g_ref, o_ref):
    o_ref[...] = _rms(x_ref[...], g_ref[...])


def _final_norm(x, gain, tm=512):
    s, d = x.shape
    return pl.pallas_call(
        _final_norm_kernel,
        grid=(s // tm,),
        in_specs=[pl.BlockSpec((tm, d), lambda i: (i, 0)), pl.BlockSpec((1, d), lambda i: (0, 0))],
        out_specs=pl.BlockSpec((tm, d), lambda i: (i, 0)),
        out_shape=jax.ShapeDtypeStruct((s, d), F32),
        compiler_params=_params(("arbitrary",)),
        name="final_norm",
    )(x, gain.reshape(1, d))


def _rope_tables(pos, dim):
    inv = ROPE_THETA ** (-jnp.arange(0, dim, 2, dtype=F32) / dim)
    ang = pos.astype(F32)[:, None] * inv[None, :]
    ang = jnp.concatenate([ang, ang], axis=-1)
    return jnp.cos(ang), jnp.sin(ang)


def _rope_pack(cos, sin, half):
    width = cos.shape[1]
    lane = jnp.arange(width)
    first = (lane % (2 * half)) < half
    tabs = [cos, jnp.where(first, -sin, 0.0), jnp.where(first, 0.0, sin)]
    return jnp.stack([jnp.pad(t, ((0, 0), (0, LANES - width))) for t in tabs])


def _t5_bucket(rel):
    half = N_BUCKETS // 2
    max_exact = half // 2
    n = jnp.abs(rel)
    large = max_exact + (jnp.log(jnp.maximum(n, 1).astype(F32) / max_exact)
                         / math.log(MAX_DISTANCE / max_exact) * (half - max_exact)).astype(jnp.int32)
    large = jnp.minimum(large, half - 1)
    return jnp.where(rel > 0, half, 0) + jnp.where(n < max_exact, n, large)


def _toeplitz(fn, rows, cols, offset):
    n = rows + cols + 1
    vec = fn(jnp.arange(n) - rows + offset)
    flat = jnp.tile(vec, (1, rows))[:, :rows * (n - 1)]
    return flat.reshape(vec.shape[0], rows, n - 1)[:, :, rows:rows + cols]


TQ_A, TQ_B, TQ_D = 256, 512, 256
TK_DENSE = 1024
FLASH_LONG_TRIP = 6
FLASH_KEYS_PER_TRIP = 4096


def kernel(x, c, w_ada, b_ada, norm1, w_in, a_q_norm, a_k_norm, b_q_norm, b_kv_norm, b_w_uq, b_w_ukv,
           c_sink, d_lambda, d_sub_norm, w_out, norm2, w_ff1, w_ff2, rel_bias, final_norm):
    _, s, d = x.shape
    depth = w_ada.shape[0]
    hd = HEAD_DIM
    x2 = x.reshape(s, d)

    t = jnp.arange(s)
    grid_h = s // GRID_W
    cos_r, sin_r = (jnp.repeat(tab, GRID_W, axis=0) for tab in _rope_tables(jnp.arange(grid_h), hd // 2))
    cos_c, sin_c = (jnp.tile(tab, (grid_h, 1)) for tab in _rope_tables(jnp.arange(GRID_W), hd // 2))
    ax_tab = _rope_pack(jnp.concatenate([cos_r, cos_c], -1), jnp.concatenate([sin_r, sin_c], -1), hd // 4)
    sq_tab = _rope_pack(*_rope_tables(t, B_ROPE), B_ROPE // 2)

    bias_fn = lambda heads: (lambda rel: (rel_bias[_t5_bucket(rel)][:, heads] * LOG2E).T)
    c_heads = jnp.arange(C_HEADS)
    d_heads = C_HEADS + jnp.arange(D_HEADS)
    rel_c = jnp.arange(3 * WINDOW)[None, :] - WINDOW - jnp.arange(WINDOW)[:, None]
    bias_c = jnp.where(jnp.abs(rel_c)[None] <= WINDOW, _toeplitz(bias_fn(c_heads), WINDOW, 3 * WINDOW, -WINDOW),
                       NEG_BIG)
    n_off = TK_DENSE // TQ_D + 2
    bias_d = jnp.stack([_toeplitz(bias_fn(d_heads), TQ_D, TK_DENSE, o * TQ_D - TK_DENSE)
                        for o in range(n_off)], axis=1)
    far_d = bias_fn(d_heads)(jnp.array([-MAX_DISTANCE, MAX_DISTANCE]))
    bias_d = jnp.concatenate(
        [bias_d, jnp.broadcast_to(far_d[:, :, None, None], (D_HEADS, 2, TQ_D, TK_DENSE))], axis=1)
    bias_d = jnp.swapaxes(bias_d, 2, 3)

    mod_all = _ada_mod(c, w_ada, b_ada)

    for l in range(depth):
        mod = mod_all[l]
        lam_init = 0.8 - 0.6 * math.exp(-0.3 * l)
        lf = d_lambda[l].astype(F32)
        lam = jnp.exp(jnp.sum(lf[0] * lf[1])) - jnp.exp(jnp.sum(lf[2] * lf[3])) + lam_init
        scal = jnp.stack([lam, jnp.asarray(1.0 - lam_init, F32)])

        w_in_l = w_in[l]
        w_in_p = jnp.concatenate([w_in_l[:, :COL_B_KR + B_ROPE], jnp.zeros((d, B_KR_PAD), F32),
                                  w_in_l[:, COL_B_KR + B_ROPE:]], axis=1).astype(BF16)
        wuq = jnp.pad(b_w_uq[l].reshape(B_Q_RANK, B_HEADS, B_NOPE + B_ROPE),
                      ((0, 0), (0, 0), (0, B_QK_PAD - B_NOPE - B_ROPE))).reshape(B_Q_RANK, -1).astype(BF16)
        wukv = b_w_ukv[l].astype(BF16)

        qkv = _norm_matmul(x2, norm1[l], mod, w_in_p, shift_row=0, scale_row=1, relu2=False,
                           out_dtype=F32, tm=512, tn=IN_COLS_PAD, name="in_proj")
        (qa, ka, va, qb, kb, vb, qc, kc, vc, qd, kd, vd) = _prep(
            qkv, ax_tab, sq_tab, a_q_norm[l], a_k_norm[l], b_q_norm[l], b_kv_norm[l], wuq, wukv)

        oa = _flash(qa, ka, va, groups=A_KV_HEADS, groups_stacked=A_HEADS // A_KV_HEADS, dk=hd,
                    tq=TQ_A, tk=TK_DENSE, name="flash_a")
        ob = _flash(qb, kb, vb, groups=B_HEADS, groups_stacked=1, dk=B_QK_PAD,
                    tq=TQ_B, tk=TK_DENSE, name="flash_b")
        oc = _window(qc, kc, vc, bias_c, c_sink[l])
        od = _flash(qd, kd, vd, groups=D_HEADS, groups_stacked=2, dk=hd, tq=TQ_D, tk=TK_DENSE,
                    name="flash_d", diff_args=(bias_d, far_d, scal, d_sub_norm[l]))

        w_out_l = w_out[l].astype(BF16)
        heads = [oa, ob, oc, od]
        w_parts = [w_out_l[k * 4 * hd:(k + 1) * 4 * hd] for k in range(4)]
        x2 = _matmul_res(heads, w_parts, x2, mod, gate_row=2, tm=512, tn=d, name="out_proj")

        hid = _norm_matmul(x2, norm2[l], mod, w_ff1[l].astype(BF16), shift_row=3, scale_row=4, relu2=True,
                           out_dtype=BF16, tm=256, tn=w_ff1.shape[2], name="ff1")
        x2 = _matmul_res([hid], [w_ff2[l].astype(BF16)], x2, mod, gate_row=5, tm=256, tn=d, name="ff2",
                         final_gain=final_norm if l == depth - 1 else None)

    return x2.reshape(1, s, d)
```

```python
import functools
import math

import jax
import jax.numpy as jnp
from jax import lax
from jax.experimental import pallas as pl
from jax.experimental.pallas import tpu as pltpu

F32 = jnp.float32
BF16 = jnp.bfloat16

HEAD_DIM = 128
GRID_W = 64
ROPE_THETA = 10000.0
EPS = 1e-6
A_HEADS, A_KV_HEADS = 4, 2
B_HEADS, B_NOPE, B_ROPE, B_Q_RANK, B_KV_RANK = 4, 128, 64, 384, 256
C_HEADS, C_KV_HEADS, WINDOW = 4, 2, 128
D_HEADS = 4
D_HALF = HEAD_DIM // 2
N_BUCKETS, MAX_DISTANCE = 32, 128
N_MOD = 6

LOG2E = math.log2(math.e)
NEG_BIG = -1e30
LAZY_MAX_JUMP = 60.0

LANES = 128
MXU_DIM = 256
VMEM_LIMIT = 56 * 1024 * 1024

TV = MXU_DIM

B_KR_PAD = LANES - B_ROPE
COL_A_Q = 0
COL_A_K = COL_A_Q + A_HEADS * HEAD_DIM
COL_A_V = COL_A_K + A_KV_HEADS * HEAD_DIM
COL_B_CQ = COL_A_V + A_KV_HEADS * HEAD_DIM
COL_B_CKV = COL_B_CQ + B_Q_RANK
COL_B_KR = COL_B_CKV + B_KV_RANK
COL_C_Q = COL_B_KR + B_ROPE + B_KR_PAD
COL_C_K = COL_C_Q + C_HEADS * HEAD_DIM
COL_C_V = COL_C_K + C_KV_HEADS * HEAD_DIM
COL_D_Q = COL_C_V + C_KV_HEADS * HEAD_DIM
COL_D_K = COL_D_Q + D_HEADS * HEAD_DIM
COL_D_V = COL_D_K + D_HEADS * HEAD_DIM
IN_COLS_PAD = COL_D_V + D_HEADS * HEAD_DIM
B_QK_PAD = 2 * LANES


def _params(sem):
    return pltpu.CompilerParams(dimension_semantics=sem, vmem_limit_bytes=VMEM_LIMIT)


def _ada_kernel(c_ref, w_ref, b_ref, o_ref):
    c = c_ref[...]
    cs = c * (1.0 / (1.0 + jnp.exp(-c)))
    cs8 = jnp.broadcast_to(cs, (8, cs.shape[1])).astype(BF16)
    r = jnp.dot(cs8, w_ref[0].astype(BF16), preferred_element_type=F32)
    o_ref[0] = r[0:1] + b_ref[0]


def _ada_mod(c, w_ada, b_ada, tn=1024):
    depth, d, n = w_ada.shape
    out = pl.pallas_call(
        _ada_kernel,
        grid=(depth, n // tn),
        in_specs=[pl.BlockSpec((1, d), lambda l, j: (0, 0)),
                  pl.BlockSpec((1, d, tn), lambda l, j: (l, 0, j)),
                  pl.BlockSpec((1, 1, tn), lambda l, j: (l, 0, j))],
        out_specs=pl.BlockSpec((1, 1, tn), lambda l, j: (l, 0, j)),
        out_shape=jax.ShapeDtypeStruct((depth, 1, n), F32),
        compiler_params=_params(("arbitrary", "arbitrary")),
        name="ada_mod",
    )(c, w_ada, b_ada.reshape(depth, 1, n))
    return out.reshape(depth, N_MOD, d)


def _norm_mm_kernel(x_ref, g_ref, mod_ref, w_ref, o_ref, h_ref, *, shift_row, scale_row, relu2):
    @pl.when(pl.program_id(1) == 0)
    def _():
        x = x_ref[...]
        y = x * lax.rsqrt(jnp.mean(x * x, axis=-1, keepdims=True) + EPS) * g_ref[...]
        h = y * (1.0 + mod_ref[scale_row:scale_row + 1, :]) + mod_ref[shift_row:shift_row + 1, :]
        h_ref[...] = h.astype(BF16)

    acc = jnp.dot(h_ref[...], w_ref[...], preferred_element_type=F32)
    if relu2:
        acc = jnp.square(jnp.maximum(acc, 0.0))
    o_ref[...] = acc.astype(o_ref.dtype)


def _norm_matmul(x, gain, mod, w, *, shift_row, scale_row, relu2, out_dtype, tm, tn, name):
    s, d = x.shape
    n = w.shape[1]
    return pl.pallas_call(
        functools.partial(_norm_mm_kernel, shift_row=shift_row, scale_row=scale_row, relu2=relu2),
        grid=(s // tm, n // tn),
        in_specs=[pl.BlockSpec((tm, d), lambda i, j: (i, 0)),
                  pl.BlockSpec((1, d), lambda i, j: (0, 0)),
                  pl.BlockSpec((N_MOD, d), lambda i, j: (0, 0)),
                  pl.BlockSpec((d, tn), lambda i, j: (0, j), pipeline_mode=pl.Buffered(1 if tn == n else 2))],
        out_specs=pl.BlockSpec((tm, tn), lambda i, j: (i, j)),
        out_shape=jax.ShapeDtypeStruct((s, n), out_dtype),
        scratch_shapes=[pltpu.VMEM((tm, d), BF16)],
        compiler_params=_params(("arbitrary", "arbitrary")),
        name=name,
    )(x, gain.reshape(1, d), mod, w)


def _mm_res_kernel(*refs, n_in, gate_row, final_norm):
    a_refs, w_refs = refs[:n_in], refs[n_in:2 * n_in]
    x_ref, mod_ref = refs[2 * n_in:2 * n_in + 2]
    o_ref = refs[-1]
    acc = jnp.dot(a_refs[0][...], w_refs[0][...], preferred_element_type=F32)
    for a_ref, w_ref in zip(a_refs[1:], w_refs[1:]):
        acc = acc + jnp.dot(a_ref[...], w_ref[...], preferred_element_type=F32)
    y = x_ref[...] + mod_ref[gate_row:gate_row + 1, :] * acc
    if final_norm:
        y = _rms(y, refs[-2][...])
    o_ref[...] = y


def _matmul_res(a_list, w_list, x, mod, *, gate_row, tm, tn, name, final_gain=None):
    s, d = x.shape
    n_in = len(a_list)
    assert final_gain is None or tn == d
    w_mode = pl.Buffered(1 if tn == d else 2)
    in_specs = ([pl.BlockSpec((tm, a.shape[1]), lambda i, j: (i, 0)) for a in a_list]
                + [pl.BlockSpec((w.shape[0], tn), lambda i, j: (0, j), pipeline_mode=w_mode) for w in w_list]
                + [pl.BlockSpec((tm, tn), lambda i, j: (i, j)),
                   pl.BlockSpec((N_MOD, tn), lambda i, j: (0, j))])
    args = [*a_list, *w_list, x, mod]
    if final_gain is not None:
        in_specs.append(pl.BlockSpec((1, d), lambda i, j: (0, 0)))
        args.append(final_gain.reshape(1, d))
    return pl.pallas_call(
        functools.partial(_mm_res_kernel, n_in=n_in, gate_row=gate_row, final_norm=final_gain is not None),
        grid=(s // tm, d // tn),
        in_specs=in_specs,
        out_specs=pl.BlockSpec((tm, tn), lambda i, j: (i, j)),
        out_shape=jax.ShapeDtypeStruct((s, d), F32),
        compiler_params=_params(("arbitrary", "arbitrary")),
        name=name,
    )(*args)


def _rope(x, cos, sin_lo, sin_hi):
    return (x * cos + pltpu.roll(x, LANES - 32, 1) * sin_lo + pltpu.roll(x, 32, 1) * sin_hi)


def _rms(x, gain):
    return x * lax.rsqrt(jnp.mean(x * x, axis=-1, keepdims=True) + EPS) * gain


def _store_val_t(ref, h, v):
    v_t = v.T.astype(BF16)
    n_blk, tv = ref.shape[1], ref.shape[3]
    for j in range(n_blk):
        ref[h, j] = v_t[:, j * tv:(j + 1) * tv]


def _prep_kernel(qkv_ref, ax_ref, sq_ref, aqn_ref, akn_ref, bqn_ref, bkvn_ref, wuq_ref, wukv_ref,
                 qa_ref, ka_ref, va_ref, qb_ref, kb_ref, vb_ref, qc_ref, kc_ref, vc_ref,
                 qd_ref, kd_ref, vd_ref):
    hd = HEAD_DIM
    ax_cos, ax_lo, ax_hi = ax_ref[0], ax_ref[1], ax_ref[2]
    sq_cos, sq_lo, sq_hi = sq_ref[0], sq_ref[1], sq_ref[2]

    def col(start, width=hd):
        return qkv_ref[:, start:start + width]

    sc_a = HEAD_DIM ** -0.5 * LOG2E
    for h in range(A_HEADS):
        q = _rope(_rms(col(COL_A_Q + h * hd), aqn_ref[...]), ax_cos, ax_lo, ax_hi)
        qa_ref[h * hd:(h + 1) * hd, :] = (q * sc_a).T.astype(BF16)
    for h in range(A_KV_HEADS):
        k = _rope(_rms(col(COL_A_K + h * hd), akn_ref[...]), ax_cos, ax_lo, ax_hi)
        ka_ref[:, h * hd:(h + 1) * hd] = k.astype(BF16)
        _store_val_t(va_ref, h, col(COL_A_V + h * hd))

    sc_b = (B_NOPE + B_ROPE) ** -0.5 * LOG2E
    cq = _rms(col(COL_B_CQ, B_Q_RANK), bqn_ref[...]).astype(BF16)
    qb = jnp.dot(cq, wuq_ref[...], preferred_element_type=F32)
    ckv = _rms(col(COL_B_CKV, B_KV_RANK), bkvn_ref[...]).astype(BF16)
    kvb = jnp.dot(ckv, wukv_ref[...], preferred_element_type=F32)
    kr = _rope(col(COL_B_KR, LANES), sq_cos, sq_lo, sq_hi).astype(BF16)
    for h in range(B_HEADS):
        base = h * B_QK_PAD
        qb_ref[base:base + hd, :] = (qb[:, base:base + hd] * sc_b).T.astype(BF16)
        q_r = _rope(qb[:, base + hd:base + 2 * hd], sq_cos, sq_lo, sq_hi)
        qb_ref[base + hd:base + 2 * hd, :] = (q_r * sc_b).T.astype(BF16)
        kb_ref[:, base:base + hd] = kvb[:, base:base + hd].astype(BF16)
        kb_ref[:, base + hd:base + 2 * hd] = kr
        _store_val_t(vb_ref, h, kvb[:, base + hd:base + 2 * hd])

    sc_c = HEAD_DIM ** -0.5 * LOG2E
    qc_ref[...] = (col(COL_C_Q, C_HEADS * hd) * sc_c).astype(BF16)
    kc_ref[...] = col(COL_C_K, C_KV_HEADS * hd).astype(BF16)
    vc_ref[...] = col(COL_C_V, C_KV_HEADS * hd).astype(BF16)

    sc_d = D_HALF ** -0.5 * LOG2E
    lane = lax.broadcasted_iota(jnp.int32, (qkv_ref.shape[0], hd), 1)
    lo = lane < D_HALF
    for h in range(D_HEADS):
        q = col(COL_D_Q + h * hd) * sc_d
        qd_ref[2 * h * hd:(2 * h + 1) * hd, :] = jnp.where(lo, q, 0.0).T.astype(BF16)
        qd_ref[(2 * h + 1) * hd:(2 * h + 2) * hd, :] = jnp.where(lo, 0.0, q).T.astype(BF16)
        _store_val_t(vd_ref, h, col(COL_D_V + h * hd))
    kd_ref[...] = col(COL_D_K, D_HEADS * hd).astype(BF16)


def _prep(qkv, ax_tab, sq_tab, aqn, akn, bqn, bkvn, wuq, wukv, tm=512):
    s = qkv.shape[0]
    hd = HEAD_DIM
    row = lambda i: (i, 0)
    full = lambda i: (0, 0)

    def row_major(w):
        return pl.BlockSpec((tm, w), row), jax.ShapeDtypeStruct((s, w), BF16)

    def feat_major(w):
        return pl.BlockSpec((w, tm), lambda i: (0, i)), jax.ShapeDtypeStruct((w, s), BF16)

    def val_t(h):
        return (pl.BlockSpec((h, tm // TV, hd, TV), lambda i: (0, i, 0, 0)),
                jax.ShapeDtypeStruct((h, s // TV, hd, TV), BF16))

    outs = [feat_major(A_HEADS * hd), row_major(A_KV_HEADS * hd), val_t(A_KV_HEADS),
            feat_major(B_HEADS * B_QK_PAD), row_major(B_HEADS * B_QK_PAD), val_t(B_HEADS),
            row_major(C_HEADS * hd), row_major(C_KV_HEADS * hd), row_major(C_KV_HEADS * hd),
            feat_major(2 * D_HEADS * hd), row_major(D_HEADS * hd), val_t(D_HEADS)]
    return pl.pallas_call(
        _prep_kernel,
        grid=(s // tm,),
        in_specs=[pl.BlockSpec((tm, IN_COLS_PAD), row),
                  pl.BlockSpec((3, tm, LANES), lambda i: (0, i, 0)),
                  pl.BlockSpec((3, tm, LANES), lambda i: (0, i, 0)),
                  pl.BlockSpec((1, hd), full), pl.BlockSpec((1, hd), full),
                  pl.BlockSpec((1, B_Q_RANK), full), pl.BlockSpec((1, B_KV_RANK), full),
                  pl.BlockSpec(wuq.shape, full), pl.BlockSpec(wukv.shape, full)],
        out_specs=[spec for spec, _ in outs],
        out_shape=[shape for _, shape in outs],
        compiler_params=_params(("arbitrary",)),
        name="prep",
    )(qkv, ax_tab, sq_tab, aqn.reshape(1, hd), akn.reshape(1, hd),
      bqn.reshape(1, B_Q_RANK), bkvn.reshape(1, B_KV_RANK), wuq, wukv)


def _flash_kernel(*refs, groups_stacked, dk, tq, tk, tv, seq, unroll, diff):
    if diff:
        q_ref, k_ref, v_ref, bias_ref, far_ref, scal_ref, gain_ref, o_ref = refs[:8]
    else:
        q_ref, k_ref, v_ref, o_ref = refs[:4]
    s_scr, acc_scr, p_scr = refs[-5:-3], refs[-3], refs[-2:]
    r_stack = groups_stacked
    cols = r_stack * tq
    i = pl.program_id(1)
    n_chunks = seq // tk
    n_sub = tk // tv
    q0 = i * tq
    q_t = jnp.concatenate([q_ref[r * dk:(r + 1) * dk, :] for r in range(r_stack)], axis=1)

    def raw_scores(c, far=False, n_keys=tk):
        start = c * tk
        if not isinstance(start, int):
            start = pl.multiple_of(start, tk)
        s = jnp.dot(k_ref[pl.ds(start, n_keys), :], q_t, preferred_element_type=F32)
        if not diff:
            return [(slice(0, cols), s)], 0.0
        if far:
            g = pl.program_id(0)
            return [(slice(0, cols), s)], jnp.where((c + 1) * tk <= q0, far_ref[g, 0], far_ref[g, 1])
        n_near = tk // tq + 2
        off = c * tk - q0
        idx = jnp.where(off < -tk, n_near, jnp.where(off > tq, n_near + 1, (off + tk) // tq))
        tile = bias_ref[0, idx, 0:n_keys, :]
        return [(slice(r * tq, (r + 1) * tq), s[:, r * tq:(r + 1) * tq] + tile) for r in range(r_stack)], 0.0

    def finalize(acc_t, l):
        o_t = acc_t / l
        o = [o_t[:, r * tq:(r + 1) * tq].T for r in range(r_stack)]
        if not diff:
            for r in range(r_stack):
                o_ref[:, r * HEAD_DIM:(r + 1) * HEAD_DIM] = o[r].astype(o_ref.dtype)
            return
        lam = scal_ref[0]
        od = o[0] - lam * o[1]
        y = od * lax.rsqrt(jnp.mean(od * od, axis=-1, keepdims=True) + EPS) * gain_ref[...]
        o_ref[...] = (y * scal_ref[1]).astype(o_ref.dtype)

    def values(c, p_ref, acc):
        for j in range(n_sub):
            acc = acc + jnp.dot(v_ref[0, c * n_sub + j], p_ref[j * tv:(j + 1) * tv, :], preferred_element_type=F32)
        return acc

    n_groups = n_chunks // unroll

    def scores_probs(c, slot, m_ref_row, far):
        pieces, offset = raw_scores(c, far)
        ref_row = m_ref_row - offset
        m_parts, l_parts = [], []
        for cs, s_piece in pieces:
            e = jnp.exp2(s_piece - ref_row[:, cs])
            p_scr[slot][:, cs] = e.astype(BF16)
            m_parts.append(jnp.max(s_piece, axis=0, keepdims=True))
            l_parts.append(jnp.sum(e, axis=0, keepdims=True))
        return jnp.concatenate(m_parts, axis=1) + offset, jnp.concatenate(l_parts, axis=1)

    n_tiled = pl.cdiv(tq + 2 * MAX_DISTANCE - 1, tk) + 1
    first_tiled = jnp.clip((q0 - MAX_DISTANCE) // tk, 0, n_chunks - n_tiled)

    def is_far(pos):
        return diff and not (isinstance(pos, int) and pos < n_tiled)

    def chunk_at(pos):
        if not diff:
            return pos
        if not is_far(pos):
            return first_tiled + pos
        j = pos - n_tiled
        return j + jnp.where(j >= first_tiled, n_tiled, 0)

    def fast_step(pos, par, state, do_scores):
        m_run, beta, l, jump = state
        c = chunk_at(pos)
        if do_scores:
            m_chunk, l_chunk = scores_probs(chunk_at(pos + 1), 1 - par, m_run, is_far(pos + 1))
        acc_scr[...] = beta * values(c, p_scr[par], acc_scr[...])
        if not do_scores:
            return state
        m_new = jnp.maximum(m_run, m_chunk)
        beta_new = jnp.exp2(m_run - m_new)
        return m_new, beta_new, beta_new * (l + l_chunk), jnp.maximum(jump, m_chunk - m_run)

    def fast_group(base, state, last, size=unroll):
        for u in range(size):
            state = fast_step(base + u, u % 2, state, not (last and u == size - 1))
        return state

    m_first, l_first = scores_probs(chunk_at(0), 0, jnp.zeros((1, cols), F32), False)
    beta_first = jnp.exp2(-m_first)
    acc_scr[...] = jnp.zeros((HEAD_DIM, cols), F32)
    state = (m_first, beta_first, beta_first * l_first, jnp.abs(m_first))
    head = (n_tiled + 1) // 2 * 2 if diff else 0
    rest = n_chunks - head
    plans = [(trip, tail) for trip in FLASH_TRIPS for tail in FLASH_TAILS
             if rest > tail and (rest - tail) % trip == 0 and (rest - tail) // trip > 1]
    trip, tail = plans[0] if plans else (2, rest)
    n_trips = (rest - tail) // trip
    if head:
        state = fast_group(0, state, False, head)
    if n_trips:
        state = lax.fori_loop(0, n_trips, lambda t, st: fast_group(head + t * trip, st, False, trip), state)
    _, _, l_fast, jump = fast_group(n_chunks - tail, state, True, tail)
    finalize(acc_scr[...], l_fast)

    def scores(c, slot):
        m_parts = []
        for cs, s_piece in raw_scores(c)[0]:
            s_scr[slot][:, cs] = s_piece
            m_parts.append(jnp.max(s_piece, axis=0, keepdims=True))
        return jnp.concatenate(m_parts, axis=1)

    def step(c, par, m_chunk, state, do_scores):
        m, l = state
        m_new = jnp.maximum(m, m_chunk)
        alpha = jnp.exp2(m - m_new)
        l = alpha * l
        m_next = scores(c + 1, 1 - par) if do_scores else None
        acc = alpha * acc_scr[...]
        for j in range(n_sub):
            p = jnp.exp2(s_scr[par][j * tv:(j + 1) * tv, :] - m_new)
            l = l + jnp.sum(p, axis=0, keepdims=True)
            acc = acc + jnp.dot(v_ref[0, c * n_sub + j], p.astype(BF16), preferred_element_type=F32)
        acc_scr[...] = acc
        return m_next, (m_new, l)

    def group(base, carry, last):
        m_chunk, state = carry
        for u in range(unroll):
            m_chunk, state = step(base + u, u % 2, m_chunk, state, not (last and u == unroll - 1))
        return m_chunk, state

    @pl.when(jnp.max(jump) > LAZY_MAX_JUMP)
    def _():
        acc_scr[...] = jnp.zeros((HEAD_DIM, cols), F32)
        carry = (scores(0, 0), (jnp.full((1, cols), NEG_BIG, F32), jnp.zeros((1, cols), F32)))
        if n_groups > 1:
            carry = lax.fori_loop(0, n_groups - 1, lambda t, cr: group(t * unroll, cr, False), carry)
        _, (_, l_exact) = group((n_groups - 1) * unroll, carry, True)
        finalize(acc_scr[...], l_exact)


def _flash(q_t, k, v_t, *, groups, groups_stacked, dk, tq, tk, name, diff_args=None):
    s = k.shape[0]
    n_vblk, tv = v_t.shape[1], v_t.shape[3]
    unroll = min(FLASH_KEYS_PER_TRIP, s) // tk
    assert unroll % 2 == 0 and (s // tk) % unroll == 0 and tk % tv == 0
    diff = diff_args is not None
    kern = functools.partial(_flash_kernel, groups_stacked=groups_stacked, dk=dk, tq=tq, tk=tk, tv=tv,
                             seq=s, unroll=unroll, diff=diff)
    in_specs = [pl.BlockSpec((groups_stacked * dk, tq), lambda g, i: (g, i)),
                pl.BlockSpec((s, dk), lambda g, i: (0, g)),
                pl.BlockSpec((1, n_vblk, HEAD_DIM, tv), lambda g, i: (g, 0, 0, 0))]
    args = [q_t, k, v_t]
    if diff:
        bias, far, scal, gain = diff_args
        in_specs += [pl.BlockSpec((1,) + bias.shape[1:], lambda g, i: (g, 0, 0, 0),
                                  pipeline_mode=pl.Buffered(1)),
                     pl.BlockSpec(memory_space=pltpu.SMEM),
                     pl.BlockSpec(memory_space=pltpu.SMEM),
                     pl.BlockSpec((1, HEAD_DIM), lambda g, i: (0, 0))]
        args += [bias, far, scal, gain.reshape(1, HEAD_DIM)]
        out_w = HEAD_DIM
    else:
        out_w = groups_stacked * HEAD_DIM
    return pl.pallas_call(
        kern,
        grid=(groups, s // tq),
        in_specs=in_specs,
        out_specs=pl.BlockSpec((tq, out_w), lambda g, i: (i, g)),
        out_shape=jax.ShapeDtypeStruct((s, groups * out_w), BF16),
        scratch_shapes=([pltpu.VMEM((tk, groups_stacked * tq), F32)] * 2
                        + [pltpu.VMEM((HEAD_DIM, groups_stacked * tq), F32)]
                        + [pltpu.VMEM((tk, groups_stacked * tq), BF16)] * 2),
        compiler_params=_params(("arbitrary", "arbitrary")),
        name=name,
    )(*args)


def _window_kernel(q_ref, kp_ref, kc_ref, kn_ref, vp_ref, vc_ref, vn_ref, bias_ref, sink_ref, o_ref,
                   kbuf, vbuf, *, tw, seq):
    w = WINDOW
    h = pl.program_id(0)
    i = pl.program_id(1)
    kbuf[0:w] = kp_ref[tw - w:tw]
    kbuf[w:w + tw] = kc_ref[...]
    kbuf[w + tw:w + tw + w] = kn_ref[0:w]
    vbuf[0:w] = vp_ref[tw - w:tw]
    vbuf[w:w + tw] = vc_ref[...]
    vbuf[w + tw:w + tw + w] = vn_ref[0:w]
    sink = sink_ref[h] * LOG2E
    nq = tw // w
    q3 = q_ref[...].reshape(nq, w, HEAD_DIM)
    k3 = jnp.stack([kbuf[j * w:(j + 3) * w] for j in range(nq)])
    v3 = jnp.stack([vbuf[j * w:(j + 3) * w] for j in range(nq)])
    s = jnp.einsum("jqd,jkd->jqk", q3, k3, preferred_element_type=F32) + bias_ref[...]
    kpos = (i * tw - w + w * lax.broadcasted_iota(jnp.int32, (nq, w, 3 * w), 0)
            + lax.broadcasted_iota(jnp.int32, (nq, w, 3 * w), 2))
    s = jnp.where((kpos >= 0) & (kpos < seq), s, NEG_BIG)
    m = jnp.maximum(jnp.max(s, axis=-1, keepdims=True), sink)
    p = jnp.exp2(s - m)
    denom = jnp.sum(p, axis=-1, keepdims=True) + jnp.exp2(sink - m)
    o = jnp.einsum("jqk,jkd->jqd", p.astype(BF16), v3, preferred_element_type=F32) / denom
    o_ref[...] = o.reshape(tw, HEAD_DIM).astype(o_ref.dtype)


def _window(q, k, v, bias, sink, tw=1024):
    s = q.shape[0]
    hd = HEAD_DIM
    nb = s // tw
    grp = C_HEADS // C_KV_HEADS
    prev = lambda h, i: (jnp.maximum(i - 1, 0), h // grp)
    cur = lambda h, i: (i, h // grp)
    nxt = lambda h, i: (jnp.minimum(i + 1, nb - 1), h // grp)
    kv_specs = [pl.BlockSpec((tw, hd), f) for f in (prev, cur, nxt)]
    return pl.pallas_call(
        functools.partial(_window_kernel, tw=tw, seq=s),
        grid=(C_HEADS, nb),
        in_specs=[pl.BlockSpec((tw, hd), lambda h, i: (i, h))] + kv_specs + kv_specs
                 + [pl.BlockSpec((1, WINDOW, 3 * WINDOW), lambda h, i: (h, 0, 0)),
                    pl.BlockSpec(memory_space=pltpu.SMEM)],
        out_specs=pl.BlockSpec((tw, hd), lambda h, i: (i, h)),
        out_shape=jax.ShapeDtypeStruct((s, C_HEADS * hd), BF16),
        scratch_shapes=[pltpu.VMEM((tw + 2 * WINDOW, hd), BF16),
                        pltpu.VMEM((tw + 2 * WINDOW, hd), BF16)],
        compiler_params=_params(("arbitrary", "arbitrary")),
        name="window_attn",
    )(q, k, k, k, v, v, v, bias, sink)


def _final_norm_kernel(x_ref, g_ref, o_ref):
    o_ref[...] = _rms(x_ref[...], g_ref[...])


def _final_norm(x, gain, tm=512):
    s, d = x.shape
    return pl.pallas_call(
        _final_norm_kernel,
        grid=(s // tm,),
        in_specs=[pl.BlockSpec((tm, d), lambda i: (i, 0)), pl.BlockSpec((1, d), lambda i: (0, 0))],
        out_specs=pl.BlockSpec((tm, d), lambda i: (i, 0)),
        out_shape=jax.ShapeDtypeStruct((s, d), F32),
        compiler_params=_params(("arbitrary",)),
        name="final_norm",
    )(x, gain.reshape(1, d))


def _rope_tables(pos, dim):
    inv = ROPE_THETA ** (-jnp.arange(0, dim, 2, dtype=F32) / dim)
    ang = pos.astype(F32)[:, None] * inv[None, :]
    ang = jnp.concatenate([ang, ang], axis=-1)
    return jnp.cos(ang), jnp.sin(ang)


def _rope_pack(cos, sin, half):
    width = cos.shape[1]
    lane = jnp.arange(width)
    first = (lane % (2 * half)) < half
    tabs = [cos, jnp.where(first, -sin, 0.0), jnp.where(first, 0.0, sin)]
    return jnp.stack([jnp.pad(t, ((0, 0), (0, LANES - width))) for t in tabs])


def _t5_bucket(rel):
    half = N_BUCKETS // 2
    max_exact = half // 2
    n = jnp.abs(rel)
    large = max_exact + (jnp.log(jnp.maximum(n, 1).astype(F32) / max_exact)
                         / math.log(MAX_DISTANCE / max_exact) * (half - max_exact)).astype(jnp.int32)
    large = jnp.minimum(large, half - 1)
    return jnp.where(rel > 0, half, 0) + jnp.where(n < max_exact, n, large)


def _toeplitz(fn, rows, cols, offset):
    n = rows + cols + 1
    vec = fn(jnp.arange(n) - rows + offset)
    flat = jnp.tile(vec, (1, rows))[:, :rows * (n - 1)]
    return flat.reshape(vec.shape[0], rows, n - 1)[:, :, rows:rows + cols]


TQ_A, TQ_B, TQ_D = 256, 512, 256
TK_DENSE = 1024
FLASH_TRIPS, FLASH_TAILS = (6, 4, 2), (4, 2)
FLASH_KEYS_PER_TRIP = 4096


def kernel(x, c, w_ada, b_ada, norm1, w_in, a_q_norm, a_k_norm, b_q_norm, b_kv_norm, b_w_uq, b_w_ukv,
           c_sink, d_lambda, d_sub_norm, w_out, norm2, w_ff1, w_ff2, rel_bias, final_norm):
    _, s, d = x.shape
    depth = w_ada.shape[0]
    hd = HEAD_DIM
    x2 = x.reshape(s, d)

    t = jnp.arange(s)
    grid_h = s // GRID_W
    cos_r, sin_r = (jnp.repeat(tab, GRID_W, axis=0) for tab in _rope_tables(jnp.arange(grid_h), hd // 2))
    cos_c, sin_c = (jnp.tile(tab, (grid_h, 1)) for tab in _rope_tables(jnp.arange(GRID_W), hd // 2))
    ax_tab = _rope_pack(jnp.concatenate([cos_r, cos_c], -1), jnp.concatenate([sin_r, sin_c], -1), hd // 4)
    sq_tab = _rope_pack(*_rope_tables(t, B_ROPE), B_ROPE // 2)

    bias_fn = lambda heads: (lambda rel: (rel_bias[_t5_bucket(rel)][:, heads] * LOG2E).T)
    c_heads = jnp.arange(C_HEADS)
    d_heads = C_HEADS + jnp.arange(D_HEADS)
    rel_c = jnp.arange(3 * WINDOW)[None, :] - WINDOW - jnp.arange(WINDOW)[:, None]
    bias_c = jnp.where(jnp.abs(rel_c)[None] <= WINDOW, _toeplitz(bias_fn(c_heads), WINDOW, 3 * WINDOW, -WINDOW),
                       NEG_BIG)
    n_off = TK_DENSE // TQ_D + 2
    bias_d = jnp.stack([_toeplitz(bias_fn(d_heads), TQ_D, TK_DENSE, o * TQ_D - TK_DENSE)
                        for o in range(n_off)], axis=1)
    far_d = bias_fn(d_heads)(jnp.array([-MAX_DISTANCE, MAX_DISTANCE]))
    bias_d = jnp.concatenate(
        [bias_d, jnp.broadcast_to(far_d[:, :, None, None], (D_HEADS, 2, TQ_D, TK_DENSE))], axis=1)
    bias_d = jnp.swapaxes(bias_d, 2, 3)

    mod_all = _ada_mod(c, w_ada, b_ada)

    for l in range(depth):
        mod = mod_all[l]
        lam_init = 0.8 - 0.6 * math.exp(-0.3 * l)
        lf = d_lambda[l].astype(F32)
        lam = jnp.exp(jnp.sum(lf[0] * lf[1])) - jnp.exp(jnp.sum(lf[2] * lf[3])) + lam_init
        scal = jnp.stack([lam, jnp.asarray(1.0 - lam_init, F32)])

        w_in_l = w_in[l]
        w_in_p = jnp.concatenate([w_in_l[:, :COL_B_KR + B_ROPE], jnp.zeros((d, B_KR_PAD), F32),
                                  w_in_l[:, COL_B_KR + B_ROPE:]], axis=1).astype(BF16)
        wuq = jnp.pad(b_w_uq[l].reshape(B_Q_RANK, B_HEADS, B_NOPE + B_ROPE),
                      ((0, 0), (0, 0), (0, B_QK_PAD - B_NOPE - B_ROPE))).reshape(B_Q_RANK, -1).astype(BF16)
        wukv = b_w_ukv[l].astype(BF16)

        qkv = _norm_matmul(x2, norm1[l], mod, w_in_p, shift_row=0, scale_row=1, relu2=False,
                           out_dtype=F32, tm=512, tn=IN_COLS_PAD, name="in_proj")
        (qa, ka, va, qb, kb, vb, qc, kc, vc, qd, kd, vd) = _prep(
            qkv, ax_tab, sq_tab, a_q_norm[l], a_k_norm[l], b_q_norm[l], b_kv_norm[l], wuq, wukv)

        oa = _flash(qa, ka, va, groups=A_KV_HEADS, groups_stacked=A_HEADS // A_KV_HEADS, dk=hd,
                    tq=TQ_A, tk=TK_DENSE, name="flash_a")
        ob = _flash(qb, kb, vb, groups=B_HEADS, groups_stacked=1, dk=B_QK_PAD,
                    tq=TQ_B, tk=TK_DENSE, name="flash_b")
        oc = _window(qc, kc, vc, bias_c, c_sink[l])
        od = _flash(qd, kd, vd, groups=D_HEADS, groups_stacked=2, dk=hd, tq=TQ_D, tk=TK_DENSE,
                    name="flash_d", diff_args=(bias_d, far_d, scal, d_sub_norm[l]))

        w_out_l = w_out[l].astype(BF16)
        heads = [oa, ob, oc, od]
        w_parts = [w_out_l[k * 4 * hd:(k + 1) * 4 * hd] for k in range(4)]
        x2 = _matmul_res(heads, w_parts, x2, mod, gate_row=2, tm=512, tn=d, name="out_proj")

        hid = _norm_matmul(x2, norm2[l], mod, w_ff1[l].astype(BF16), shift_row=3, scale_row=4, relu2=True,
                           out_dtype=BF16, tm=256, tn=w_ff1.shape[2], name="ff1")
        x2 = _matmul_res([hid], [w_ff2[l].astype(BF16)], x2, mod, gate_row=5, tm=256, tn=d, name="ff2",
                         final_gain=final_norm if l == depth - 1 else None)

    return x2.reshape(1, s, d)
```

```python
import functools
import math

import jax
import jax.numpy as jnp
from jax import lax
from jax.experimental import pallas as pl
from jax.experimental.pallas import tpu as pltpu

F32 = jnp.float32
BF16 = jnp.bfloat16

HEAD_DIM = 128
GRID_W = 64
ROPE_THETA = 10000.0
EPS = 1e-6
A_HEADS, A_KV_HEADS = 4, 2
B_HEADS, B_NOPE, B_ROPE, B_Q_RANK, B_KV_RANK = 4, 128, 64, 384, 256
C_HEADS, C_KV_HEADS, WINDOW = 4, 2, 128
D_HEADS = 4
D_HALF = HEAD_DIM // 2
N_BUCKETS, MAX_DISTANCE = 32, 128
N_MOD = 6

LOG2E = math.log2(math.e)
NEG_BIG = -1e30
LAZY_MAX_JUMP = 60.0

LANES = 128
MXU_DIM = 256
VMEM_LIMIT = 56 * 1024 * 1024

TV = MXU_DIM

B_KR_PAD = LANES - B_ROPE
COL_A_Q = 0
COL_A_K = COL_A_Q + A_HEADS * HEAD_DIM
COL_A_V = COL_A_K + A_KV_HEADS * HEAD_DIM
COL_B_CQ = COL_A_V + A_KV_HEADS * HEAD_DIM
COL_B_CKV = COL_B_CQ + B_Q_RANK
COL_B_KR = COL_B_CKV + B_KV_RANK
COL_C_Q = COL_B_KR + B_ROPE + B_KR_PAD
COL_C_K = COL_C_Q + C_HEADS * HEAD_DIM
COL_C_V = COL_C_K + C_KV_HEADS * HEAD_DIM
COL_D_Q = COL_C_V + C_KV_HEADS * HEAD_DIM
COL_D_K = COL_D_Q + D_HEADS * HEAD_DIM
COL_D_V = COL_D_K + D_HEADS * HEAD_DIM
IN_COLS_PAD = COL_D_V + D_HEADS * HEAD_DIM
B_QK_PAD = 2 * LANES


def _params(sem):
    return pltpu.CompilerParams(dimension_semantics=sem, vmem_limit_bytes=VMEM_LIMIT)


def _ada_kernel(c_ref, w_ref, b_ref, o_ref):
    c = c_ref[...]
    cs = c * (1.0 / (1.0 + jnp.exp(-c)))
    cs8 = jnp.broadcast_to(cs, (8, cs.shape[1])).astype(BF16)
    r = jnp.dot(cs8, w_ref[0].astype(BF16), preferred_element_type=F32)
    o_ref[0] = r[0:1] + b_ref[0]


def _ada_mod(c, w_ada, b_ada, tn=2048):
    depth, d, n = w_ada.shape
    out = pl.pallas_call(
        _ada_kernel,
        grid=(depth, n // tn),
        in_specs=[pl.BlockSpec((1, d), lambda l, j: (0, 0)),
                  pl.BlockSpec((1, d, tn), lambda l, j: (l, 0, j)),
                  pl.BlockSpec((1, 1, tn), lambda l, j: (l, 0, j))],
        out_specs=pl.BlockSpec((1, 1, tn), lambda l, j: (l, 0, j)),
        out_shape=jax.ShapeDtypeStruct((depth, 1, n), F32),
        compiler_params=_params(("arbitrary", "arbitrary")),
        name="ada_mod",
    )(c, w_ada, b_ada.reshape(depth, 1, n))
    return out.reshape(depth, N_MOD, d)


def _norm_mm_kernel(x_ref, g_ref, mod_ref, w_ref, o_ref, h_ref, *, shift_row, scale_row, relu2):
    @pl.when(pl.program_id(1) == 0)
    def _():
        x = x_ref[...]
        y = x * lax.rsqrt(jnp.mean(x * x, axis=-1, keepdims=True) + EPS) * g_ref[...]
        h = y * (1.0 + mod_ref[scale_row:scale_row + 1, :]) + mod_ref[shift_row:shift_row + 1, :]
        h_ref[...] = h.astype(BF16)

    acc = jnp.dot(h_ref[...], w_ref[...], preferred_element_type=F32)
    if relu2:
        acc = jnp.square(jnp.maximum(acc, 0.0))
    o_ref[...] = acc.astype(o_ref.dtype)


def _norm_matmul(x, gain, mod, w, *, shift_row, scale_row, relu2, out_dtype, tm, tn, name):
    s, d = x.shape
    n = w.shape[1]
    return pl.pallas_call(
        functools.partial(_norm_mm_kernel, shift_row=shift_row, scale_row=scale_row, relu2=relu2),
        grid=(s // tm, n // tn),
        in_specs=[pl.BlockSpec((tm, d), lambda i, j: (i, 0)),
                  pl.BlockSpec((1, d), lambda i, j: (0, 0)),
                  pl.BlockSpec((N_MOD, d), lambda i, j: (0, 0)),
                  pl.BlockSpec((d, tn), lambda i, j: (0, j), pipeline_mode=pl.Buffered(1 if tn == n else 2))],
        out_specs=pl.BlockSpec((tm, tn), lambda i, j: (i, j)),
        out_shape=jax.ShapeDtypeStruct((s, n), out_dtype),
        scratch_shapes=[pltpu.VMEM((tm, d), BF16)],
        compiler_params=_params(("arbitrary", "arbitrary")),
        name=name,
    )(x, gain.reshape(1, d), mod, w)


def _mm_res_kernel(*refs, n_in, gate_row, final_norm):
    a_refs, w_refs = refs[:n_in], refs[n_in:2 * n_in]
    x_ref, mod_ref = refs[2 * n_in:2 * n_in + 2]
    o_ref = refs[-1]
    acc = jnp.dot(a_refs[0][...], w_refs[0][...], preferred_element_type=F32)
    for a_ref, w_ref in zip(a_refs[1:], w_refs[1:]):
        acc = acc + jnp.dot(a_ref[...], w_ref[...], preferred_element_type=F32)
    y = x_ref[...] + mod_ref[gate_row:gate_row + 1, :] * acc
    if final_norm:
        y = _rms(y, refs[-2][...])
    o_ref[...] = y


def _matmul_res(a_list, w_list, x, mod, *, gate_row, tm, tn, name, final_gain=None):
    s, d = x.shape
    n_in = len(a_list)
    assert final_gain is None or tn == d
    w_mode = pl.Buffered(1 if tn == d else 2)
    in_specs = ([pl.BlockSpec((tm, a.shape[1]), lambda i, j: (i, 0)) for a in a_list]
                + [pl.BlockSpec((w.shape[0], tn), lambda i, j: (0, j), pipeline_mode=w_mode) for w in w_list]
                + [pl.BlockSpec((tm, tn), lambda i, j: (i, j)),
                   pl.BlockSpec((N_MOD, tn), lambda i, j: (0, j))])
    args = [*a_list, *w_list, x, mod]
    if final_gain is not None:
        in_specs.append(pl.BlockSpec((1, d), lambda i, j: (0, 0)))
        args.append(final_gain.reshape(1, d))
    return pl.pallas_call(
        functools.partial(_mm_res_kernel, n_in=n_in, gate_row=gate_row, final_norm=final_gain is not None),
        grid=(s // tm, d // tn),
        in_specs=in_specs,
        out_specs=pl.BlockSpec((tm, tn), lambda i, j: (i, j)),
        out_shape=jax.ShapeDtypeStruct((s, d), F32),
        compiler_params=_params(("arbitrary", "arbitrary")),
        name=name,
    )(*args)


def _rope(x, cos, sin_lo, sin_hi):
    return (x * cos + pltpu.roll(x, LANES - 32, 1) * sin_lo + pltpu.roll(x, 32, 1) * sin_hi)


def _rms(x, gain):
    return x * lax.rsqrt(jnp.mean(x * x, axis=-1, keepdims=True) + EPS) * gain


def _store_val_t(ref, h, v):
    v_t = v.T.astype(BF16)
    n_blk, tv = ref.shape[1], ref.shape[3]
    for j in range(n_blk):
        ref[h, j] = v_t[:, j * tv:(j + 1) * tv]


def _prep_kernel(qkv_ref, ax_ref, sq_ref, aqn_ref, akn_ref, bqn_ref, bkvn_ref, wuq_ref, wukv_ref,
                 qa_ref, ka_ref, va_ref, qb_ref, kb_ref, vb_ref, qc_ref, kc_ref, vc_ref,
                 qd_ref, kd_ref, vd_ref):
    hd = HEAD_DIM
    ax_cos, ax_lo, ax_hi = ax_ref[0], ax_ref[1], ax_ref[2]
    sq_cos, sq_lo, sq_hi = sq_ref[0], sq_ref[1], sq_ref[2]

    def col(start, width=hd):
        return qkv_ref[:, start:start + width]

    sc_a = HEAD_DIM ** -0.5 * LOG2E
    for h in range(A_HEADS):
        q = _rope(_rms(col(COL_A_Q + h * hd), aqn_ref[...]), ax_cos, ax_lo, ax_hi)
        qa_ref[h * hd:(h + 1) * hd, :] = (q * sc_a).T.astype(BF16)
    for h in range(A_KV_HEADS):
        k = _rope(_rms(col(COL_A_K + h * hd), akn_ref[...]), ax_cos, ax_lo, ax_hi)
        ka_ref[:, h * hd:(h + 1) * hd] = k.astype(BF16)
        _store_val_t(va_ref, h, col(COL_A_V + h * hd))

    sc_b = (B_NOPE + B_ROPE) ** -0.5 * LOG2E
    cq = _rms(col(COL_B_CQ, B_Q_RANK), bqn_ref[...]).astype(BF16)
    qb = jnp.dot(cq, wuq_ref[...], preferred_element_type=F32)
    ckv = _rms(col(COL_B_CKV, B_KV_RANK), bkvn_ref[...]).astype(BF16)
    kvb = jnp.dot(ckv, wukv_ref[...], preferred_element_type=F32)
    kr = _rope(col(COL_B_KR, LANES), sq_cos, sq_lo, sq_hi).astype(BF16)
    for h in range(B_HEADS):
        base = h * B_QK_PAD
        qb_ref[base:base + hd, :] = (qb[:, base:base + hd] * sc_b).T.astype(BF16)
        q_r = _rope(qb[:, base + hd:base + 2 * hd], sq_cos, sq_lo, sq_hi)
        qb_ref[base + hd:base + 2 * hd, :] = (q_r * sc_b).T.astype(BF16)
        kb_ref[:, base:base + hd] = kvb[:, base:base + hd].astype(BF16)
        kb_ref[:, base + hd:base + 2 * hd] = kr
        _store_val_t(vb_ref, h, kvb[:, base + hd:base + 2 * hd])

    sc_c = HEAD_DIM ** -0.5 * LOG2E
    qc_ref[...] = (col(COL_C_Q, C_HEADS * hd) * sc_c).astype(BF16)
    kc_ref[...] = col(COL_C_K, C_KV_HEADS * hd).astype(BF16)
    vc_ref[...] = col(COL_C_V, C_KV_HEADS * hd).astype(BF16)

    sc_d = D_HALF ** -0.5 * LOG2E
    lane = lax.broadcasted_iota(jnp.int32, (qkv_ref.shape[0], hd), 1)
    lo = lane < D_HALF
    for h in range(D_HEADS):
        q = col(COL_D_Q + h * hd) * sc_d
        qd_ref[2 * h * hd:(2 * h + 1) * hd, :] = jnp.where(lo, q, 0.0).T.astype(BF16)
        qd_ref[(2 * h + 1) * hd:(2 * h + 2) * hd, :] = jnp.where(lo, 0.0, q).T.astype(BF16)
        _store_val_t(vd_ref, h, col(COL_D_V + h * hd))
    kd_ref[...] = col(COL_D_K, D_HEADS * hd).astype(BF16)


def _prep(qkv, ax_tab, sq_tab, aqn, akn, bqn, bkvn, wuq, wukv, tm=512):
    s = qkv.shape[0]
    hd = HEAD_DIM
    row = lambda i: (i, 0)
    full = lambda i: (0, 0)

    def row_major(w):
        return pl.BlockSpec((tm, w), row), jax.ShapeDtypeStruct((s, w), BF16)

    def feat_major(w):
        return pl.BlockSpec((w, tm), lambda i: (0, i)), jax.ShapeDtypeStruct((w, s), BF16)

    def val_t(h):
        return (pl.BlockSpec((h, tm // TV, hd, TV), lambda i: (0, i, 0, 0)),
                jax.ShapeDtypeStruct((h, s // TV, hd, TV), BF16))

    outs = [feat_major(A_HEADS * hd), row_major(A_KV_HEADS * hd), val_t(A_KV_HEADS),
            feat_major(B_HEADS * B_QK_PAD), row_major(B_HEADS * B_QK_PAD), val_t(B_HEADS),
            row_major(C_HEADS * hd), row_major(C_KV_HEADS * hd), row_major(C_KV_HEADS * hd),
            feat_major(2 * D_HEADS * hd), row_major(D_HEADS * hd), val_t(D_HEADS)]
    return pl.pallas_call(
        _prep_kernel,
        grid=(s // tm,),
        in_specs=[pl.BlockSpec((tm, IN_COLS_PAD), row),
                  pl.BlockSpec((3, tm, LANES), lambda i: (0, i, 0)),
                  pl.BlockSpec((3, tm, LANES), lambda i: (0, i, 0)),
                  pl.BlockSpec((1, hd), full), pl.BlockSpec((1, hd), full),
                  pl.BlockSpec((1, B_Q_RANK), full), pl.BlockSpec((1, B_KV_RANK), full),
                  pl.BlockSpec(wuq.shape, full), pl.BlockSpec(wukv.shape, full)],
        out_specs=[spec for spec, _ in outs],
        out_shape=[shape for _, shape in outs],
        compiler_params=_params(("arbitrary",)),
        name="prep",
    )(qkv, ax_tab, sq_tab, aqn.reshape(1, hd), akn.reshape(1, hd),
      bqn.reshape(1, B_Q_RANK), bkvn.reshape(1, B_KV_RANK), wuq, wukv)


def _flash_kernel(*refs, groups_stacked, dk, tq, tk, tv, seq, unroll, diff):
    if diff:
        q_ref, k_ref, v_ref, bias_ref, far_ref, scal_ref, gain_ref, o_ref = refs[:8]
    else:
        q_ref, k_ref, v_ref, o_ref = refs[:4]
    s_scr, acc_scr, p_scr = refs[-5:-3], refs[-3], refs[-2:]
    r_stack = groups_stacked
    cols = r_stack * tq
    i = pl.program_id(1)
    n_chunks = seq // tk
    n_sub = tk // tv
    q0 = i * tq
    q_t = jnp.concatenate([q_ref[r * dk:(r + 1) * dk, :] for r in range(r_stack)], axis=1)

    def raw_scores(c, far=False, n_keys=tk):
        start = c * tk
        if not isinstance(start, int):
            start = pl.multiple_of(start, tk)
        s = jnp.dot(k_ref[pl.ds(start, n_keys), :], q_t, preferred_element_type=F32)
        if not diff:
            return [(slice(0, cols), s)], 0.0
        if far:
            g = pl.program_id(0)
            return [(slice(0, cols), s)], jnp.where((c + 1) * tk <= q0, far_ref[g, 0], far_ref[g, 1])
        n_near = tk // tq + 2
        off = c * tk - q0
        idx = jnp.where(off < -tk, n_near, jnp.where(off > tq, n_near + 1, (off + tk) // tq))
        tile = bias_ref[0, idx, 0:n_keys, :]
        return [(slice(r * tq, (r + 1) * tq), s[:, r * tq:(r + 1) * tq] + tile) for r in range(r_stack)], 0.0

    def finalize(acc_t, l):
        o_t = acc_t / l
        o = [o_t[:, r * tq:(r + 1) * tq].T for r in range(r_stack)]
        if not diff:
            for r in range(r_stack):
                o_ref[:, r * HEAD_DIM:(r + 1) * HEAD_DIM] = o[r].astype(o_ref.dtype)
            return
        lam = scal_ref[0]
        od = o[0] - lam * o[1]
        y = od * lax.rsqrt(jnp.mean(od * od, axis=-1, keepdims=True) + EPS) * gain_ref[...]
        o_ref[...] = (y * scal_ref[1]).astype(o_ref.dtype)

    def values(c, p_ref, acc):
        for j in range(n_sub):
            acc = acc + jnp.dot(v_ref[0, c * n_sub + j], p_ref[j * tv:(j + 1) * tv, :], preferred_element_type=F32)
        return acc

    n_groups = n_chunks // unroll

    def scores_probs(c, slot, m_ref_row, far):
        pieces, offset = raw_scores(c, far)
        ref_row = m_ref_row - offset
        m_parts, l_parts = [], []
        for cs, s_piece in pieces:
            e = jnp.exp2(s_piece - ref_row[:, cs])
            p_scr[slot][:, cs] = e.astype(BF16)
            m_parts.append(jnp.max(s_piece, axis=0, keepdims=True))
            l_parts.append(jnp.sum(e, axis=0, keepdims=True))
        return jnp.concatenate(m_parts, axis=1) + offset, jnp.concatenate(l_parts, axis=1)

    n_tiled = pl.cdiv(tq + 2 * MAX_DISTANCE - 1, tk) + 1
    first_tiled = jnp.clip((q0 - MAX_DISTANCE) // tk, 0, n_chunks - n_tiled)

    def is_far(pos):
        return diff and not (isinstance(pos, int) and pos < n_tiled)

    def chunk_at(pos):
        if not diff:
            return pos
        if not is_far(pos):
            return first_tiled + pos
        j = pos - n_tiled
        return j + jnp.where(j >= first_tiled, n_tiled, 0)

    def fast_step(pos, par, state, do_scores):
        m_run, beta, l, jump = state
        c = chunk_at(pos)
        if do_scores:
            m_chunk, l_chunk = scores_probs(chunk_at(pos + 1), 1 - par, m_run, is_far(pos + 1))
        acc_scr[...] = beta * values(c, p_scr[par], acc_scr[...])
        if not do_scores:
            return state
        m_new = jnp.maximum(m_run, m_chunk)
        beta_new = jnp.exp2(m_run - m_new)
        return m_new, beta_new, beta_new * (l + l_chunk), jnp.maximum(jump, m_chunk - m_run)

    def fast_group(base, state, last, size=unroll):
        for u in range(size):
            state = fast_step(base + u, u % 2, state, not (last and u == size - 1))
        return state

    m_first, l_first = scores_probs(chunk_at(0), 0, jnp.zeros((1, cols), F32), False)
    beta_first = jnp.exp2(-m_first)
    acc_scr[...] = jnp.zeros((HEAD_DIM, cols), F32)
    state = (m_first, beta_first, beta_first * l_first, jnp.abs(m_first))
    head = (n_tiled + 1) // 2 * 2 if diff else 0
    rest = n_chunks - head
    plans = [(trip, tail) for trip in FLASH_TRIPS for tail in FLASH_TAILS
             if rest > tail and (rest - tail) % trip == 0 and (rest - tail) // trip > 1]
    trip, tail = plans[0] if plans else (2, rest)
    n_trips = (rest - tail) // trip
    if head:
        state = fast_group(0, state, False, head)
    if n_trips:
        state = lax.fori_loop(0, n_trips, lambda t, st: fast_group(head + t * trip, st, False, trip), state)
    _, _, l_fast, jump = fast_group(n_chunks - tail, state, True, tail)
    finalize(acc_scr[...], l_fast)

    def scores(c, slot):
        m_parts = []
        for cs, s_piece in raw_scores(c)[0]:
            s_scr[slot][:, cs] = s_piece
            m_parts.append(jnp.max(s_piece, axis=0, keepdims=True))
        return jnp.concatenate(m_parts, axis=1)

    def step(c, par, m_chunk, state, do_scores):
        m, l = state
        m_new = jnp.maximum(m, m_chunk)
        alpha = jnp.exp2(m - m_new)
        l = alpha * l
        m_next = scores(c + 1, 1 - par) if do_scores else None
        acc = alpha * acc_scr[...]
        for j in range(n_sub):
            p = jnp.exp2(s_scr[par][j * tv:(j + 1) * tv, :] - m_new)
            l = l + jnp.sum(p, axis=0, keepdims=True)
            acc = acc + jnp.dot(v_ref[0, c * n_sub + j], p.astype(BF16), preferred_element_type=F32)
        acc_scr[...] = acc
        return m_next, (m_new, l)

    def group(base, carry, last):
        m_chunk, state = carry
        for u in range(unroll):
            m_chunk, state = step(base + u, u % 2, m_chunk, state, not (last and u == unroll - 1))
        return m_chunk, state

    @pl.when(jnp.max(jump) > LAZY_MAX_JUMP)
    def _():
        acc_scr[...] = jnp.zeros((HEAD_DIM, cols), F32)
        carry = (scores(0, 0), (jnp.full((1, cols), NEG_BIG, F32), jnp.zeros((1, cols), F32)))
        if n_groups > 1:
            carry = lax.fori_loop(0, n_groups - 1, lambda t, cr: group(t * unroll, cr, False), carry)
        _, (_, l_exact) = group((n_groups - 1) * unroll, carry, True)
        finalize(acc_scr[...], l_exact)


def _flash(q_t, k, v_t, *, groups, groups_stacked, dk, tq, tk, name, diff_args=None):
    s = k.shape[0]
    n_vblk, tv = v_t.shape[1], v_t.shape[3]
    unroll = min(FLASH_KEYS_PER_TRIP, s) // tk
    assert unroll % 2 == 0 and (s // tk) % unroll == 0 and tk % tv == 0
    diff = diff_args is not None
    kern = functools.partial(_flash_kernel, groups_stacked=groups_stacked, dk=dk, tq=tq, tk=tk, tv=tv,
                             seq=s, unroll=unroll, diff=diff)
    in_specs = [pl.BlockSpec((groups_stacked * dk, tq), lambda g, i: (g, i)),
                pl.BlockSpec((s, dk), lambda g, i: (0, g)),
                pl.BlockSpec((1, n_vblk, HEAD_DIM, tv), lambda g, i: (g, 0, 0, 0))]
    args = [q_t, k, v_t]
    if diff:
        bias, far, scal, gain = diff_args
        in_specs += [pl.BlockSpec((1,) + bias.shape[1:], lambda g, i: (g, 0, 0, 0),
                                  pipeline_mode=pl.Buffered(1)),
                     pl.BlockSpec(memory_space=pltpu.SMEM),
                     pl.BlockSpec(memory_space=pltpu.SMEM),
                     pl.BlockSpec((1, HEAD_DIM), lambda g, i: (0, 0))]
        args += [bias, far, scal, gain.reshape(1, HEAD_DIM)]
        out_w = HEAD_DIM
    else:
        out_w = groups_stacked * HEAD_DIM
    return pl.pallas_call(
        kern,
        grid=(groups, s // tq),
        in_specs=in_specs,
        out_specs=pl.BlockSpec((tq, out_w), lambda g, i: (i, g)),
        out_shape=jax.ShapeDtypeStruct((s, groups * out_w), BF16),
        scratch_shapes=([pltpu.VMEM((tk, groups_stacked * tq), F32)] * 2
                        + [pltpu.VMEM((HEAD_DIM, groups_stacked * tq), F32)]
                        + [pltpu.VMEM((tk, groups_stacked * tq), BF16)] * 2),
        compiler_params=_params(("arbitrary", "arbitrary")),
        name=name,
    )(*args)


def _window_kernel(q_ref, kp_ref, kc_ref, kn_ref, vp_ref, vc_ref, vn_ref, bias_ref, sink_ref, o_ref,
                   kbuf, vbuf, *, tw, seq):
    w = WINDOW
    h = pl.program_id(0)
    i = pl.program_id(1)
    kbuf[0:w] = kp_ref[tw - w:tw]
    kbuf[w:w + tw] = kc_ref[...]
    kbuf[w + tw:w + tw + w] = kn_ref[0:w]
    vbuf[0:w] = vp_ref[tw - w:tw]
    vbuf[w:w + tw] = vc_ref[...]
    vbuf[w + tw:w + tw + w] = vn_ref[0:w]
    sink = sink_ref[h] * LOG2E
    nq = tw // w
    q3 = q_ref[...].reshape(nq, w, HEAD_DIM)
    k3 = jnp.stack([kbuf[j * w:(j + 3) * w] for j in range(nq)])
    v3 = jnp.stack([vbuf[j * w:(j + 3) * w] for j in range(nq)])
    s = jnp.einsum("jqd,jkd->jqk", q3, k3, preferred_element_type=F32) + bias_ref[...]
    kpos = (i * tw - w + w * lax.broadcasted_iota(jnp.int32, (nq, w, 3 * w), 0)
            + lax.broadcasted_iota(jnp.int32, (nq, w, 3 * w), 2))
    s = jnp.where((kpos >= 0) & (kpos < seq), s, NEG_BIG)
    m = jnp.maximum(jnp.max(s, axis=-1, keepdims=True), sink)
    p = jnp.exp2(s - m)
    denom = jnp.sum(p, axis=-1, keepdims=True) + jnp.exp2(sink - m)
    o = jnp.einsum("jqk,jkd->jqd", p.astype(BF16), v3, preferred_element_type=F32) / denom
    o_ref[...] = o.reshape(tw, HEAD_DIM).astype(o_ref.dtype)


def _window(q, k, v, bias, sink, tw=2048):
    s = q.shape[0]
    hd = HEAD_DIM
    nb = s // tw
    grp = C_HEADS // C_KV_HEADS
    prev = lambda h, i: (jnp.maximum(i - 1, 0), h // grp)
    cur = lambda h, i: (i, h // grp)
    nxt = lambda h, i: (jnp.minimum(i + 1, nb - 1), h // grp)
    kv_specs = [pl.BlockSpec((tw, hd), f) for f in (prev, cur, nxt)]
    return pl.pallas_call(
        functools.partial(_window_kernel, tw=tw, seq=s),
        grid=(C_HEADS, nb),
        in_specs=[pl.BlockSpec((tw, hd), lambda h, i: (i, h))] + kv_specs + kv_specs
                 + [pl.BlockSpec((1, WINDOW, 3 * WINDOW), lambda h, i: (h, 0, 0)),
                    pl.BlockSpec(memory_space=pltpu.SMEM)],
        out_specs=pl.BlockSpec((tw, hd), lambda h, i: (i, h)),
        out_shape=jax.ShapeDtypeStruct((s, C_HEADS * hd), BF16),
        scratch_shapes=[pltpu.VMEM((tw + 2 * WINDOW, hd), BF16),
                        pltpu.VMEM((tw + 2 * WINDOW, hd), BF16)],
        compiler_params=_params(("arbitrary", "arbitrary")),
        name="window_attn",
    )(q, k, k, k, v, v, v, bias, sink)


def _final_norm_kernel(x_ref, g_ref, o_ref):
    o_ref[...] = _rms(x_ref[...], g_ref[...])


def _final_norm(x, gain, tm=512):
    s, d = x.shape
    return pl.pallas_call(
        _final_norm_kernel,
        grid=(s // tm,),
        in_specs=[pl.BlockSpec((tm, d), lambda i: (i, 0)), pl.BlockSpec((1, d), lambda i: (0, 0))],
        out_specs=pl.BlockSpec((tm, d), lambda i: (i, 0)),
        out_shape=jax.ShapeDtypeStruct((s, d), F32),
        compiler_params=_params(("arbitrary",)),
        name="final_norm",
    )(x, gain.reshape(1, d))


def _rope_tables(pos, dim):
    inv = ROPE_THETA ** (-jnp.arange(0, dim, 2, dtype=F32) / dim)
    ang = pos.astype(F32)[:, None] * inv[None, :]
    ang = jnp.concatenate([ang, ang], axis=-1)
    return jnp.cos(ang), jnp.sin(ang)


def _rope_pack(cos, sin, half):
    width = cos.shape[1]
    lane = jnp.arange(width)
    first = (lane % (2 * half)) < half
    tabs = [cos, jnp.where(first, -sin, 0.0), jnp.where(first, 0.0, sin)]
    return jnp.stack([jnp.pad(t, ((0, 0), (0, LANES - width))) for t in tabs])


def _t5_bucket(rel):
    half = N_BUCKETS // 2
    max_exact = half // 2
    n = jnp.abs(rel)
    large = max_exact + (jnp.log(jnp.maximum(n, 1).astype(F32) / max_exact)
                         / math.log(MAX_DISTANCE / max_exact) * (half - max_exact)).astype(jnp.int32)
    large = jnp.minimum(large, half - 1)
    return jnp.where(rel > 0, half, 0) + jnp.where(n < max_exact, n, large)


def _toeplitz(fn, rows, cols, offset):
    n = rows + cols + 1
    vec = fn(jnp.arange(n) - rows + offset)
    flat = jnp.tile(vec, (1, rows))[:, :rows * (n - 1)]
    return flat.reshape(vec.shape[0], rows, n - 1)[:, :, rows:rows + cols]


TQ_A, TQ_B, TQ_D = 256, 512, 256
TK_DENSE = 1024
FLASH_TRIPS, FLASH_TAILS = (6, 4, 2), (4, 2)
FLASH_KEYS_PER_TRIP = 4096


def kernel(x, c, w_ada, b_ada, norm1, w_in, a_q_norm, a_k_norm, b_q_norm, b_kv_norm, b_w_uq, b_w_ukv,
           c_sink, d_lambda, d_sub_norm, w_out, norm2, w_ff1, w_ff2, rel_bias, final_norm):
    _, s, d = x.shape
    depth = w_ada.shape[0]
    hd = HEAD_DIM
    x2 = x.reshape(s, d)

    t = jnp.arange(s)
    grid_h = s // GRID_W
    cos_r, sin_r = (jnp.repeat(tab, GRID_W, axis=0) for tab in _rope_tables(jnp.arange(grid_h), hd // 2))
    cos_c, sin_c = (jnp.tile(tab, (grid_h, 1)) for tab in _rope_tables(jnp.arange(GRID_W), hd // 2))
    ax_tab = _rope_pack(jnp.concatenate([cos_r, cos_c], -1), jnp.concatenate([sin_r, sin_c], -1), hd // 4)
    sq_tab = _rope_pack(*_rope_tables(t, B_ROPE), B_ROPE // 2)

    bias_fn = lambda heads: (lambda rel: (rel_bias[_t5_bucket(rel)][:, heads] * LOG2E).T)
    c_heads = jnp.arange(C_HEADS)
    d_heads = C_HEADS + jnp.arange(D_HEADS)
    rel_c = jnp.arange(3 * WINDOW)[None, :] - WINDOW - jnp.arange(WINDOW)[:, None]
    bias_c = jnp.where(jnp.abs(rel_c)[None] <= WINDOW, _toeplitz(bias_fn(c_heads), WINDOW, 3 * WINDOW, -WINDOW),
                       NEG_BIG)
    n_off = TK_DENSE // TQ_D + 2
    bias_d = jnp.stack([_toeplitz(bias_fn(d_heads), TQ_D, TK_DENSE, o * TQ_D - TK_DENSE)
                        for o in range(n_off)], axis=1)
    far_d = bias_fn(d_heads)(jnp.array([-MAX_DISTANCE, MAX_DISTANCE]))
    bias_d = jnp.concatenate(
        [bias_d, jnp.broadcast_to(far_d[:, :, None, None], (D_HEADS, 2, TQ_D, TK_DENSE))], axis=1)
    bias_d = jnp.swapaxes(bias_d, 2, 3)

    mod_all = _ada_mod(c, w_ada, b_ada)

    for l in range(depth):
        mod = mod_all[l]
        lam_init = 0.8 - 0.6 * math.exp(-0.3 * l)
        lf = d_lambda[l].astype(F32)
        lam = jnp.exp(jnp.sum(lf[0] * lf[1])) - jnp.exp(jnp.sum(lf[2] * lf[3])) + lam_init
        scal = jnp.stack([lam, jnp.asarray(1.0 - lam_init, F32)])

        w_in_l = w_in[l]
        w_in_p = jnp.concatenate([w_in_l[:, :COL_B_KR + B_ROPE], jnp.zeros((d, B_KR_PAD), F32),
                                  w_in_l[:, COL_B_KR + B_ROPE:]], axis=1).astype(BF16)
        wuq = jnp.pad(b_w_uq[l].reshape(B_Q_RANK, B_HEADS, B_NOPE + B_ROPE),
                      ((0, 0), (0, 0), (0, B_QK_PAD - B_NOPE - B_ROPE))).reshape(B_Q_RANK, -1).astype(BF16)
        wukv = b_w_ukv[l].astype(BF16)

        qkv = _norm_matmul(x2, norm1[l], mod, w_in_p, shift_row=0, scale_row=1, relu2=False,
                           out_dtype=F32, tm=512, tn=IN_COLS_PAD, name="in_proj")
        (qa, ka, va, qb, kb, vb, qc, kc, vc, qd, kd, vd) = _prep(
            qkv, ax_tab, sq_tab, a_q_norm[l], a_k_norm[l], b_q_norm[l], b_kv_norm[l], wuq, wukv)

        oa = _flash(qa, ka, va, groups=A_KV_HEADS, groups_stacked=A_HEADS // A_KV_HEADS, dk=hd,
                    tq=TQ_A, tk=TK_DENSE, name="flash_a")
        ob = _flash(qb, kb, vb, groups=B_HEADS, groups_stacked=1, dk=B_QK_PAD,
                    tq=TQ_B, tk=TK_DENSE, name="flash_b")
        oc = _window(qc, kc, vc, bias_c, c_sink[l])
        od = _flash(qd, kd, vd, groups=D_HEADS, groups_stacked=2, dk=hd, tq=TQ_D, tk=TK_DENSE,
                    name="flash_d", diff_args=(bias_d, far_d, scal, d_sub_norm[l]))

        w_out_l = w_out[l].astype(BF16)
        heads = [oa, ob, oc, od]
        w_parts = [w_out_l[k * 4 * hd:(k + 1) * 4 * hd] for k in range(4)]
        x2 = _matmul_res(heads, w_parts, x2, mod, gate_row=2, tm=512, tn=d, name="out_proj")

        hid = _norm_matmul(x2, norm2[l], mod, w_ff1[l].astype(BF16), shift_row=3, scale_row=4, relu2=True,
                           out_dtype=BF16, tm=256, tn=w_ff1.shape[2], name="ff1")
        x2 = _matmul_res([hid], [w_ff2[l].astype(BF16)], x2, mod, gate_row=5, tm=256, tn=d, name="ff2",
                         final_gain=final_norm if l == depth - 1 else None)

    return x2.reshape(1, s, d)
```
